```python
import jax, jax.numpy as jnp
from jax import lax
import numpy as np

D_MODEL = 1024
BATCH = 1
SEQ = 16384
DEPTH = 1

GRID_W = 64
CTX_LEN = 256
N_MLA_HEADS = 8
MLA_Q_RANK = 256
MLA_KV_RANK = 128
MLA_NOPE_DIM = 64
MLA_ROPE_DIM = 32
MLA_QK_DIM = MLA_NOPE_DIM + MLA_ROPE_DIM
MLA_V_DIM = 64
N_GLA_HEADS = 4
GLA_DK = 64
GLA_DV = 128
GLA_GATE_RANK = 16
GLA_GATE_NORM = 16.0
GLA_CHUNK = 64
D_FF = 2816
MACARON_WEIGHT = 0.5
Q_BLOCK = 128
ROPE_BASE = 10000.0
EPS = 1e-6
N_MOD = 9

D_MLA_OUT = N_MLA_HEADS * MLA_V_DIM
D_GLA_OUT = N_GLA_HEADS * GLA_DV
D_MIX = D_MLA_OUT + D_GLA_OUT
IN_SPLITS = (MLA_Q_RANK, MLA_KV_RANK + MLA_ROPE_DIM, N_GLA_HEADS * GLA_DK, N_GLA_HEADS * GLA_DK,
             D_GLA_OUT, D_GLA_OUT, 2 * GLA_GATE_RANK)
D_IN = 256 + 160 + 256 + 256 + 512 + 512 + 32

kernel_name = "hybrid_mla_gla_macaron_dit_block"


def split_cols(t, sizes):
    out, start = [], 0
    for s in sizes:
        out.append(t[..., start:start + s])
        start += s
    return out


def rmsnorm(x, g):
    xf = x.astype(jnp.float32)
    y = xf * lax.rsqrt(jnp.mean(xf * xf, axis=-1, keepdims=True) + EPS)
    return (y * g.astype(jnp.float32)).astype(x.dtype)


def modulate(h, shift, scale):
    return h * (1 + scale[:, None, :]) + shift[:, None, :]


def ffn_sublayer(x, g_pre, g_post, shift, scale, gate, w_in, w_out):
    h = modulate(rmsnorm(x, g_pre), shift, scale)
    a, b = jnp.split(h @ w_in, 2, axis=-1)
    y = (jax.nn.silu(a) * b) @ w_out
    return x + MACARON_WEIGHT * gate[:, None, :] * rmsnorm(y, g_post)


def axial_rope_angles(n):
    rows = n // GRID_W
    row = jnp.repeat(jnp.arange(rows, dtype=jnp.float32), GRID_W)
    col = jnp.tile(jnp.arange(GRID_W, dtype=jnp.float32), rows)
    axis_dim = MLA_ROPE_DIM // 2
    inv_freq = ROPE_BASE ** (-jnp.arange(0, axis_dim, 2, dtype=jnp.float32) / axis_dim)
    return row[:, None] * inv_freq, col[:, None] * inv_freq


def rotate_axis(x, ang):
    x1, x2 = jnp.split(x, 2, axis=-1)
    cos, sin = jnp.cos(ang), jnp.sin(ang)
    return jnp.concatenate([x1 * cos - x2 * sin, x2 * cos + x1 * sin], axis=-1)


def apply_rope_2d(x, ang_row, ang_col):
    xr, xc = jnp.split(x.astype(jnp.float32), 2, axis=-1)
    return jnp.concatenate([rotate_axis(xr, ang_row), rotate_axis(xc, ang_col)], axis=-1).astype(x.dtype)


def gla_log_decay(a_low, w_b, b_b):
    b, t, _ = a_low.shape
    la = jax.nn.log_sigmoid((a_low @ w_b + b_b).astype(jnp.float32)) / GLA_GATE_NORM
    return la.reshape(b, t, N_GLA_HEADS, GLA_DK)


def mixer_inputs(h, p, angles=None):
    b, t, _ = h.shape
    q_a, kv_a, gq, gk, gv, gg, ga = split_cols(h @ p["w_in"], IN_SPLITS)
    q = (rmsnorm(q_a, p["mla_q_norm"]) @ p["mla_w_qb"]).reshape(b, t, N_MLA_HEADS, MLA_QK_DIM)
    q_nope, q_rope = q[..., :MLA_NOPE_DIM], q[..., MLA_NOPE_DIM:]
    kv = (rmsnorm(kv_a[..., :MLA_KV_RANK], p["mla_kv_norm"]) @ p["mla_w_kvb"]).reshape(
        b, t, N_MLA_HEADS, MLA_NOPE_DIM + MLA_V_DIM)
    k_nope, v = kv[..., :MLA_NOPE_DIM], kv[..., MLA_NOPE_DIM:]
    k_rope = kv_a[..., MLA_KV_RANK:]
    if angles is not None:
        ang_row, ang_col = angles
        q_rope = apply_rope_2d(q_rope, ang_row[:, None, :], ang_col[:, None, :])
        k_rope = apply_rope_2d(k_rope, ang_row, ang_col)
    q = jnp.concatenate([q_nope, q_rope], axis=-1)
    k = jnp.concatenate(
        [k_nope, jnp.broadcast_to(k_rope[:, :, None, :], (b, t, N_MLA_HEADS, MLA_ROPE_DIM))], axis=-1)
    gla = (gq.reshape(b, t, N_GLA_HEADS, GLA_DK) * (GLA_DK ** -0.5),
           gk.reshape(b, t, N_GLA_HEADS, GLA_DK),
           gv.reshape(b, t, N_GLA_HEADS, GLA_DV),
           gg,
           gla_log_decay(ga[..., :GLA_GATE_RANK], p["gla_w_a_fwd"], p["gla_b_a_fwd"]),
           gla_log_decay(ga[..., GLA_GATE_RANK:], p["gla_w_a_bwd"], p["gla_b_a_bwd"]))
    return q, k, v, gla


def softmax_attend(q, k, v):
    s = jnp.einsum('bqhd,bkhd->bhqk', q, k).astype(jnp.float32) * (MLA_QK_DIM ** -0.5)
    probs = jax.nn.softmax(s, axis=-1).astype(v.dtype)
    return jnp.einsum('bhqk,bkhd->bqhd', probs, v)


def blocked_attend(q, k, v):
    b, n, h, d = q.shape
    nb = n // Q_BLOCK
    qb = jnp.moveaxis(q.reshape(b, nb, Q_BLOCK, h, d), 1, 0)
    out = lax.map(lambda blk: softmax_attend(blk, k, v), qb)
    return jnp.moveaxis(out, 0, 1).reshape(b, n, h, v.shape[-1])


def gla_chunked(q, k, v, log_a, s0):
    b, t, h, dk = q.shape
    dv = v.shape[-1]
    nc = t // GLA_CHUNK

    def to_chunks(a):
        return a.reshape(b, nc, GLA_CHUNK, h, a.shape[-1]).transpose(1, 0, 3, 2, 4)

    lower = jnp.tril(jnp.ones((GLA_CHUNK, GLA_CHUNK), dtype=bool))

    def step(state, inp):
        qc, kc, vc, gc = inp
        qc, kc, vc = qc.astype(jnp.float32), kc.astype(jnp.float32), vc.astype(jnp.float32)
        cum = jnp.cumsum(gc, axis=-2)
        o_inter = jnp.einsum('bhcd,bhde->bhce', qc * jnp.exp(cum), state)
        diff = cum[..., :, None, :] - cum[..., None, :, :]
        decay = jnp.exp(jnp.where(lower[:, :, None], diff, -jnp.inf))
        attn = jnp.einsum('bhid,bhjd,bhijd->bhij', qc, kc, decay)
        o_intra = jnp.einsum('bhij,bhje->bhie', attn, vc)
        last = cum[..., -1:, :]
        new_state = jnp.exp(last)[..., 0, :, None] * state + jnp.einsum(
            'bhjd,bhje->bhde', kc * jnp.exp(last - cum), vc)
        return new_state, o_inter + o_intra

    s_final, o = lax.scan(step, s0, (to_chunks(q), to_chunks(k), to_chunks(v), to_chunks(log_a)))
    o = o.transpose(1, 0, 3, 2, 4).reshape(b, t, h, dv)
    return o.astype(v.dtype), s_final


def gla_bidirectional(gq, gk, gv, la_f, la_b, s_f, s_b):
    flip = lambda a: jnp.flip(a, axis=1)
    o_f, s_f_new = gla_chunked(gq, gk, gv, la_f, s_f)
    o_b, s_b_new = gla_chunked(flip(gq), flip(gk), flip(gv), flip(la_b), s_b)
    return o_f + flip(o_b), s_f_new, s_b_new


def gla_output(o, gg, gain):
    b, t = o.shape[:2]
    gate = jax.nn.silu(gg.reshape(b, t, N_GLA_HEADS, GLA_DV))
    return (rmsnorm(o, gain) * gate).reshape(b, t, D_GLA_OUT)


def token_mixing(a_lat, a_ctx, angles, p, need_ctx_out):
    b, n, _ = a_lat.shape
    q_c, k_c, v_c, gla_c = mixer_inputs(a_ctx, p)
    q_l, k_l, v_l, gla_l = mixer_inputs(a_lat, p, angles)
    k_all = jnp.concatenate([k_c, k_l], axis=1)
    v_all = jnp.concatenate([v_c, v_l], axis=1)
    mla_lat = blocked_attend(q_l, k_all, v_all).reshape(b, n, D_MLA_OUT)
    s0 = jnp.zeros((a_ctx.shape[0], N_GLA_HEADS, GLA_DK, GLA_DV), jnp.float32)
    cq, ck, cv, cg, cla_f, cla_b = gla_c
    o_c, s_cf, s_cb = gla_bidirectional(cq, ck, cv, cla_f, cla_b, s0, s0)
    lq, lk, lv, lg, lla_f, lla_b = gla_l
    o_l, _, _ = gla_bidirectional(lq, lk, lv, lla_f, lla_b, s_cf, s_cb)
    gla_lat = gla_output(o_l, lg, p["gla_norm"])
    y_lat = jnp.concatenate([mla_lat, gla_lat], axis=-1) @ p["w_out"]
    if need_ctx_out:
        m = a_ctx.shape[1]
        mla_ctx = softmax_attend(q_c, k_c, v_c).reshape(a_ctx.shape[0], m, D_MLA_OUT)
        gla_ctx = gla_output(o_c, cg, p["gla_norm"])
        y_ctx = jnp.concatenate([mla_ctx, gla_ctx], axis=-1) @ p["w_out"]
    else:
        y_ctx = None
    return y_lat, y_ctx


def setup_inputs(seed: int = 0) -> dict:
    key = jax.random.key(seed)
    ks = jax.random.split(key, 24)
    f32 = jnp.float32
    nrm = lambda k, shape, s: jax.random.normal(k, shape, f32) * s
    L, D = DEPTH, D_MODEL
    return {
        "x": nrm(ks[0], (BATCH, SEQ, D), 1.0),
        "c": nrm(ks[1], (BATCH, D), 1.0),
        "ctx": nrm(ks[2], (BATCH, CTX_LEN, D), 1.0),
        "c_ctx": nrm(ks[3], (D,), 1.0),
        "w_ada": nrm(ks[4], (L, D, N_MOD * D), 0.5 * D ** -0.5),
        "b_ada": nrm(ks[5], (L, N_MOD * D), 0.02),
        "norm_pre": 1.0 + nrm(ks[6], (L, 3, D), 0.05),
        "norm_post": 1.0 + nrm(ks[7], (L, 3, D), 0.05),
        "ffn1_w_in": nrm(ks[8], (L, D, 2 * D_FF), D ** -0.5),
        "ffn1_w_out": nrm(ks[9], (L, D_FF, D), D_FF ** -0.5),
        "ffn2_w_in": nrm(ks[10], (L, D, 2 * D_FF), D ** -0.5),
        "ffn2_w_out": nrm(ks[11], (L, D_FF, D), D_FF ** -0.5),
        "w_in": nrm(ks[12], (L, D, D_IN), D ** -0.5),
        "mla_q_norm": 1.0 + nrm(ks[13], (L, MLA_Q_RANK), 0.05),
        "mla_w_qb": nrm(ks[14], (L, MLA_Q_RANK, N_MLA_HEADS * MLA_QK_DIM), MLA_Q_RANK ** -0.5),
        "mla_kv_norm": 1.0 + nrm(ks[15], (L, MLA_KV_RANK), 0.05),
        "mla_w_kvb": nrm(ks[16], (L, MLA_KV_RANK, N_MLA_HEADS * (MLA_NOPE_DIM + MLA_V_DIM)), MLA_KV_RANK ** -0.5),
        "gla_w_a_fwd": nrm(ks[17], (L, GLA_GATE_RANK, N_GLA_HEADS * GLA_DK), GLA_GATE_RANK ** -0.5),
        "gla_b_a_fwd": nrm(ks[18], (L, N_GLA_HEADS * GLA_DK), 0.1),
        "gla_w_a_bwd": nrm(ks[19], (L, GLA_GATE_RANK, N_GLA_HEADS * GLA_DK), GLA_GATE_RANK ** -0.5),
        "gla_b_a_bwd": nrm(ks[20], (L, N_GLA_HEADS * GLA_DK), 0.1),
        "gla_norm": 1.0 + nrm(ks[21], (L, GLA_DV), 0.05),
        "w_out": nrm(ks[22], (L, D_MIX, D), D_MIX ** -0.5),
    }


def reference(x, c, ctx, c_ctx, w_ada, b_ada, norm_pre, norm_post, ffn1_w_in, ffn1_w_out,
              ffn2_w_in, ffn2_w_out, w_in, mla_q_norm, mla_w_qb, mla_kv_norm, mla_w_kvb,
              gla_w_a_fwd, gla_b_a_fwd, gla_w_a_bwd, gla_b_a_bwd, gla_norm, w_out):
    n = x.shape[1]
    angles = axial_rope_angles(n)
    h_lat, h_ctx = x, ctx
    for l in range(DEPTH):
        last = l == DEPTH - 1
        m_lat = jnp.split(jax.nn.silu(c) @ w_ada[l] + b_ada[l], N_MOD, axis=-1)
        m_ctx = jnp.split((jax.nn.silu(c_ctx) @ w_ada[l] + b_ada[l])[None], N_MOD, axis=-1)
        h_lat = ffn_sublayer(h_lat, norm_pre[l, 0], norm_post[l, 0], m_lat[0], m_lat[1], m_lat[2],
                             ffn1_w_in[l], ffn1_w_out[l])
        h_ctx = ffn_sublayer(h_ctx, norm_pre[l, 0], norm_post[l, 0], m_ctx[0], m_ctx[1], m_ctx[2],
                             ffn1_w_in[l], ffn1_w_out[l])
        p = {"w_in": w_in[l], "mla_q_norm": mla_q_norm[l], "mla_w_qb": mla_w_qb[l],
             "mla_kv_norm": mla_kv_norm[l], "mla_w_kvb": mla_w_kvb[l],
             "gla_w_a_fwd": gla_w_a_fwd[l], "gla_b_a_fwd": gla_b_a_fwd[l],
             "gla_w_a_bwd": gla_w_a_bwd[l], "gla_b_a_bwd": gla_b_a_bwd[l],
             "gla_norm": gla_norm[l], "w_out": w_out[l]}
        a_lat = modulate(rmsnorm(h_lat, norm_pre[l, 1]), m_lat[3], m_lat[4])
        a_ctx = modulate(rmsnorm(h_ctx, norm_pre[l, 1]), m_ctx[3], m_ctx[4])
        y_lat, y_ctx = token_mixing(a_lat, a_ctx, angles, p, not last)
        h_lat = h_lat + m_lat[5][:, None, :] * rmsnorm(y_lat, norm_post[l, 1])
        h_lat = ffn_sublayer(h_lat, norm_pre[l, 2], norm_post[l, 2], m_lat[6], m_lat[7], m_lat[8],
                             ffn2_w_in[l], ffn2_w_out[l])
        if not last:
            h_ctx = h_ctx + m_ctx[5][:, None, :] * rmsnorm(y_ctx, norm_post[l, 1])
            h_ctx = ffn_sublayer(h_ctx, norm_pre[l, 2], norm_post[l, 2], m_ctx[6], m_ctx[7], m_ctx[8],
                                 ffn2_w_in[l], ffn2_w_out[l])
    return h_lat
```

```python
import functools
import math

import jax
import jax.numpy as jnp
from jax import lax
from jax.experimental import pallas as pl
from jax.experimental.pallas import tpu as pltpu

F32 = jnp.float32
BF16 = jnp.bfloat16

D_MODEL = 1024
GRID_W = 64
N_MLA_HEADS = 8
MLA_Q_RANK = 256
MLA_KV_RANK = 128
MLA_NOPE_DIM = 64
MLA_ROPE_DIM = 32
MLA_QK_DIM = MLA_NOPE_DIM + MLA_ROPE_DIM
MLA_V_DIM = 64
N_GLA_HEADS = 4
GLA_DK = 64
GLA_DV = 128
GLA_GATE_RANK = 16
GLA_GATE_NORM = 16.0
GLA_CHUNK = 64
D_FF = 2816
MACARON_WEIGHT = 0.5
ROPE_BASE = 10000.0
EPS = 1e-6
N_MOD = 9

D_MLA_OUT = N_MLA_HEADS * MLA_V_DIM
D_GLA_OUT = N_GLA_HEADS * GLA_DV
D_GLA_QK = N_GLA_HEADS * GLA_DK

LANES = 128
HEAD_PAD = 128
ROW_TILE = 256
VMEM_LIMIT = 56 * 1024 * 1024
C_QA = 0
C_KV = C_QA + MLA_Q_RANK
C_GQ = C_KV + MLA_KV_RANK
C_GK = C_GQ + D_GLA_QK
C_GV = C_GK + D_GLA_QK
C_GG = C_GV + D_GLA_OUT
C_MISC = C_GG + D_GLA_OUT
C_KSW = C_MISC + LANES
C_TOTAL = C_KSW + LANES
ROPE_LANE0 = MLA_NOPE_DIM
GLA_LEVELS = (64, 32, 16, 8, 4, 2, 1)


def _rms(x, g):
    return x * lax.rsqrt(jnp.mean(x * x, axis=-1, keepdims=True) + EPS) * g


def _silu(x):
    return x * jax.nn.sigmoid(x)


def _params(*sem):
    return pltpu.CompilerParams(dimension_semantics=sem, vmem_limit_bytes=VMEM_LIMIT)


def _ada_kernel(ct_ref, w_ref, b_ref, o_ref):
    s = _silu(ct_ref[...])
    w = w_ref[...]
    r0 = jnp.sum(s[:, 0:1] * w, axis=0, keepdims=True)
    r1 = jnp.sum(s[:, 1:2] * w, axis=0, keepdims=True)
    o_ref[...] = jnp.concatenate([r0, r1], axis=0) + b_ref[...]


def _ada(ct, w, b):
    d, n = w.shape
    tn = n // 8
    return pl.pallas_call(
        _ada_kernel,
        grid=(n // tn,),
        in_specs=[pl.BlockSpec((d, 2), lambda j: (0, 0)),
                  pl.BlockSpec((d, tn), lambda j: (0, j)),
                  pl.BlockSpec((1, tn), lambda j: (0, j))],
        out_specs=pl.BlockSpec((2, tn), lambda j: (0, j)),
        out_shape=jax.ShapeDtypeStruct((2, n), F32),
        compiler_params=_params("arbitrary"),
        name="adaln",
    )(ct, w, b)


def _ffn_kernel(x_ref, mod_ref, gpre_ref, gpost_ref, win_ref, wout_ref, o_ref, *, mod0):
    x = x_ref[...]
    m = mod_ref[0]
    shift, scale, gate = m[mod0:mod0 + 1], m[mod0 + 1:mod0 + 2], m[mod0 + 2:mod0 + 3]
    h = _rms(x, gpre_ref[...]) * (1.0 + scale) + shift
    ab = jnp.dot(h.astype(BF16), win_ref[...], preferred_element_type=F32)
    a, b = ab[:, :D_FF], ab[:, D_FF:]
    g = (_silu(a) * b).astype(BF16)
    y = jnp.dot(g, wout_ref[...], preferred_element_type=F32)
    o_ref[...] = x + (MACARON_WEIGHT * gate) * _rms(y, gpost_ref[...])


def _mod_spec(n_lat_blocks):
    return pl.BlockSpec((1, N_MOD, D_MODEL), lambda i: (jnp.where(i < n_lat_blocks, 0, 1), 0, 0))


def _ffn(x, mod, g_pre, g_post, w_in, w_out, *, mod0, n_rows, n_lat):
    tm = ROW_TILE
    row = lambda i: (i, 0)
    const = lambda i: (0, 0)
    return pl.pallas_call(
        functools.partial(_ffn_kernel, mod0=mod0),
        grid=(n_rows // tm,),
        in_specs=[pl.BlockSpec((tm, D_MODEL), row),
                  _mod_spec(n_lat // tm),
                  pl.BlockSpec((1, D_MODEL), const),
                  pl.BlockSpec((1, D_MODEL), const),
                  pl.BlockSpec((D_MODEL, 2 * D_FF), const),
                  pl.BlockSpec((D_FF, D_MODEL), const)],
        out_specs=pl.BlockSpec((tm, D_MODEL), row),
        out_shape=jax.ShapeDtypeStruct((n_rows, D_MODEL), F32),
        compiler_params=_params("arbitrary"),
        name="ffn",
    )(x, mod, g_pre, g_post, w_in, w_out)


def _rope_tables(i, tm, is_lat):
    lane = lax.broadcasted_iota(jnp.int32, (tm, LANES), 1)
    rowi = lax.broadcasted_iota(jnp.int32, (tm, LANES), 0)
    e = lane - ROPE_LANE0
    in_rope = (e >= 0) & (e < MLA_ROPE_DIM)
    axis_dim = MLA_ROPE_DIM // 2
    grp = e >> 4
    w = e & (axis_dim - 1)
    j = w & (axis_dim // 2 - 1)
    first_half = w < axis_dim // 2
    inv_freq = jnp.exp(j.astype(F32) * (-math.log(ROPE_BASE) * 2.0 / axis_dim))
    pos = i * tm + rowi
    grid_row = (pos // GRID_W).astype(F32)
    grid_col = (pos % GRID_W).astype(F32)
    ang = jnp.where(grp == 0, grid_row, grid_col) * inv_freq
    cosv = jnp.where(is_lat, jnp.cos(ang), 1.0)
    sinv = jnp.where(is_lat, jnp.sin(ang), 0.0)
    cos_k = jnp.where(in_rope, cosv, 0.0)
    sin_s = jnp.where(in_rope, jnp.where(first_half, -sinv, sinv), 0.0)
    cos_q = jnp.where(lane < ROPE_LANE0, 1.0, cos_k)
    return cos_q, cos_k, sin_s


def _mixin_kernel(h_ref, mod_ref, gpre_ref, win_ref, qn_ref, wq_ref, kvn_ref, wkv_ref, wa_ref, ba_ref,
                  q_ref, k_ref, v_ref, gq_ref, gk_ref, gv_ref, gg_ref, la_ref, *, n_lat_blocks):
    i = pl.program_id(0)
    tm = h_ref.shape[0]
    m = mod_ref[0]
    shift, scale = m[3:4], m[4:5]
    a = _rms(h_ref[...], gpre_ref[...]) * (1.0 + scale) + shift
    z = jnp.dot(a.astype(BF16), win_ref[...], preferred_element_type=F32)

    cos_q, cos_k, sin_s = _rope_tables(i, tm, i < n_lat_blocks)

    qn = _rms(z[:, C_QA:C_QA + MLA_Q_RANK], qn_ref[...]).astype(BF16)
    qq = jnp.dot(qn, wq_ref[...], preferred_element_type=F32)
    hw = N_MLA_HEADS * HEAD_PAD
    sm_scale = MLA_QK_DIM ** -0.5
    for h in range(N_MLA_HEADS):
        lo = h * HEAD_PAD
        qh = qq[:, lo:lo + HEAD_PAD] * cos_q + qq[:, hw + lo:hw + lo + HEAD_PAD] * sin_s
        q_ref[h] = (qh * sm_scale).astype(BF16)

    kvn = _rms(z[:, C_KV:C_KV + MLA_KV_RANK], kvn_ref[...]).astype(BF16)
    kv = jnp.dot(kvn, wkv_ref[...], preferred_element_type=F32)
    misc = z[:, C_MISC:C_MISC + LANES]
    k_rope = misc * cos_k + z[:, C_KSW:C_KSW + LANES] * sin_s
    for h in range(N_MLA_HEADS):
        lo = h * HEAD_PAD
        k_ref[h] = (kv[:, lo:lo + HEAD_PAD] + k_rope).astype(BF16)
    v_ref[...] = kv[:, hw:].astype(BF16)

    gq_ref[...] = z[:, C_GQ:C_GQ + D_GLA_QK] * (GLA_DK ** -0.5)
    gk_ref[...] = z[:, C_GK:C_GK + D_GLA_QK]
    gv_ref[...] = z[:, C_GV:C_GV + D_GLA_OUT]
    gg_ref[...] = z[:, C_GG:C_GG + D_GLA_OUT]
    xg = jnp.dot(misc, wa_ref[...], preferred_element_type=F32,
                 precision=lax.Precision.HIGHEST) + ba_ref[...]
    la = (jnp.minimum(xg, 0.0) - jnp.log1p(jnp.exp(-jnp.abs(xg)))) * (1.0 / GLA_GATE_NORM)
    la_ref[0] = la[:, :D_GLA_QK]
    la_ref[1] = la[:, D_GLA_QK:]


def _mixin(h_all, mod, g_pre, w_in, q_norm, w_q, kv_norm, w_kv, w_a, b_a, *, n_lat):
    n_all = h_all.shape[0]
    tm = ROW_TILE
    row = lambda i: (i, 0)
    const = lambda i: (0, 0)
    hrow = lambda i: (0, i, 0)
    full = lambda arr: pl.BlockSpec(arr.shape, const)
    out_shape = (
        jax.ShapeDtypeStruct((N_MLA_HEADS, n_all, HEAD_PAD), BF16),
        jax.ShapeDtypeStruct((N_MLA_HEADS, n_all, HEAD_PAD), BF16),
        jax.ShapeDtypeStruct((n_all, D_MLA_OUT), BF16),
        jax.ShapeDtypeStruct((n_all, D_GLA_QK), F32),
        jax.ShapeDtypeStruct((n_all, D_GLA_QK), F32),
        jax.ShapeDtypeStruct((n_all, D_GLA_OUT), F32),
        jax.ShapeDtypeStruct((n_all, D_GLA_OUT), F32),
        jax.ShapeDtypeStruct((2, n_all, D_GLA_QK), F32),
    )
    out_specs = (
        pl.BlockSpec((N_MLA_HEADS, tm, HEAD_PAD), hrow),
        pl.BlockSpec((N_MLA_HEADS, tm, HEAD_PAD), hrow),
        pl.BlockSpec((tm, D_MLA_OUT), row),
        pl.BlockSpec((tm, D_GLA_QK), row),
        pl.BlockSpec((tm, D_GLA_QK), row),
        pl.BlockSpec((tm, D_GLA_OUT), row),
        pl.BlockSpec((tm, D_GLA_OUT), row),
        pl.BlockSpec((2, tm, D_GLA_QK), hrow),
    )
    return pl.pallas_call(
        functools.partial(_mixin_kernel, n_lat_blocks=n_lat // tm),
        grid=(n_all // tm,),
        in_specs=[pl.BlockSpec((tm, D_MODEL), row), _mod_spec(n_lat // tm),
                  full(g_pre), full(w_in), full(q_norm), full(w_q), full(kv_norm), full(w_kv),
                  full(w_a), full(b_a)],
        out_specs=out_specs,
        out_shape=out_shape,
        compiler_params=_params("arbitrary"),
        name="mixer_in",
    )(h_all, mod, g_pre, w_in, q_norm, w_q, kv_norm, w_kv, w_a, b_a)


def _attn_kernel(q_ref, k_ref, v_ref, o_ref, *, bk):
    n_keys = k_ref.shape[1]
    bq = q_ref.shape[1]
    nt = (((1,), (1,)), ((), ()))
    outs = []
    for h in range(2):
        q = q_ref[h]

        def body(c, carry):
            m, l, acc = carry
            off = pl.multiple_of(c * bk, bk)
            s = lax.dot_general(q, k_ref[h, pl.ds(off, bk), :], nt, preferred_element_type=F32)
            m_new = jnp.maximum(m, jnp.max(s, axis=-1, keepdims=True))
            alpha = jnp.exp(m - m_new)
            p = jnp.exp(s - m_new)
            l = alpha * l + jnp.sum(p, axis=-1, keepdims=True)
            acc = alpha * acc + jnp.dot(p.astype(BF16), v_ref[pl.ds(off, bk), :],
                                        preferred_element_type=F32)
            return m_new, l, acc

        init = (jnp.full((bq, 1), -jnp.inf, F32), jnp.zeros((bq, 1), F32),
                jnp.zeros((bq, 2 * MLA_V_DIM), F32))
        _, l, acc = lax.fori_loop(0, n_keys // bk, body, init)
        outs.append(acc / l)
    lane = lax.broadcasted_iota(jnp.int32, outs[0].shape, 1)
    o_ref[...] = jnp.where(lane < MLA_V_DIM, outs[0], outs[1]).astype(o_ref.dtype)


def _attn(q, k, v, *, n_lat, bq, bk):
    n_all = k.shape[1]
    return pl.pallas_call(
        functools.partial(_attn_kernel, bk=bk),
        grid=(N_MLA_HEADS // 2, n_lat // bq),
        in_specs=[pl.BlockSpec((2, bq, HEAD_PAD), lambda hp, i: (hp, i, 0)),
                  pl.BlockSpec((2, n_all, HEAD_PAD), lambda hp, i: (hp, 0, 0)),
                  pl.BlockSpec((n_all, 2 * MLA_V_DIM), lambda hp, i: (0, hp))],
        out_specs=pl.BlockSpec((bq, 2 * MLA_V_DIM), lambda hp, i: (i, hp)),
        out_shape=jax.ShapeDtypeStruct((n_lat, D_MLA_OUT), BF16),
        compiler_params=_params("arbitrary", "arbitrary"),
        name="mla_attention",
    )(q, k, v)


def _gla_level_matrices(fwd):
    c = GLA_CHUNK
    ri = lax.broadcasted_iota(jnp.int32, (c, c), 0)
    ci = lax.broadcasted_iota(jnp.int32, (c, c), 1)
    q_mats, k_mats = [], []
    for b in GLA_LEVELS:
        bs = ri & (~(b - 1))
        be = bs + (b - 1)
        tq_f = (ci >= bs) & (ci <= ri)
        tk_f = (ci > ri) & (ci <= be)
        tq_b = (ci >= ri) & (ci <= be)
        tk_b = (ci < ri) & (ci >= bs)
        q_mats.append(jnp.where(fwd, tq_f.astype(F32), tq_b.astype(F32)))
        k_mats.append(jnp.where(fwd, tk_f.astype(F32), tk_b.astype(F32)))
    return jnp.concatenate(q_mats + k_mats, axis=0)


def _gla_kernel(q_ref, k_ref, v_ref, la_ref, o_ref, st_ref):
    d = pl.program_id(0)
    s = pl.program_id(1)
    fwd = d == 0
    c = GLA_CHUNK
    tm = q_ref.shape[0]
    n_chunks = tm // c
    n_lv = len(GLA_LEVELS)
    nt = (((1,), (1,)), ((), ()))
    tn = (((0,), (0,)), ((), ()))

    @pl.when(s == 0)
    def _():
        st_ref[...] = jnp.zeros_like(st_ref)

    w_lv = _gla_level_matrices(fwd)
    ri = lax.broadcasted_iota(jnp.int32, (c, c), 0)
    ci = lax.broadcasted_iota(jnp.int32, (c, c), 1)
    hi = jnp.where(fwd, ri, ci)
    xr = ri ^ ci
    lane = lax.broadcasted_iota(jnp.int32, (c, LANES), 1)
    head_mask = (lane < GLA_DK, lane >= GLA_DK)
    lane_v = lax.broadcasted_iota(jnp.int32, (GLA_DV, LANES), 1)

    def chunk(t, carry):
        cidx = jnp.where(fwd, t, n_chunks - 1 - t)
        off = pl.multiple_of(cidx * c, c)
        g = la_ref[0, pl.ds(off, c), :]
        e_all = jnp.exp(jnp.dot(w_lv, g, preferred_element_type=F32,
                                precision=lax.Precision.HIGHEST))
        e_tot = jnp.exp(jnp.sum(g, axis=0, keepdims=True))
        q_all = q_ref[pl.ds(off, c), :]
        k_all = k_ref[pl.ds(off, c), :]
        for p in range(N_GLA_HEADS // 2):
            ls = slice(p * LANES, (p + 1) * LANES)
            qp, kp = q_all[:, ls], k_all[:, ls]
            st = st_ref[p]
            k_inter = (kp * e_all[n_lv * c:(n_lv + 1) * c, ls]).astype(BF16)
            upd = []
            for hh in range(2):
                h = 2 * p + hh
                v = v_ref[pl.ds(off, c), h * GLA_DV:(h + 1) * GLA_DV].astype(BF16)
                qm = jnp.where(head_mask[hh], qp, 0.0)
                att = jnp.where(ri == ci,
                                lax.dot_general(qm.astype(BF16), kp.astype(BF16), nt,
                                                preferred_element_type=F32), 0.0)
                for li, b in enumerate(GLA_LEVELS):
                    if b == c:
                        continue
                    ql = (qm * e_all[li * c:(li + 1) * c, ls]).astype(BF16)
                    kl = (kp * e_all[(n_lv + li) * c:(n_lv + li + 1) * c, ls]).astype(BF16)
                    sel = (xr >= b) & (xr < 2 * b) & ((hi & b) != 0)
                    att = att + jnp.where(sel, lax.dot_general(ql, kl, nt, preferred_element_type=F32), 0.0)
                q_inter = (qm * e_all[0:c, ls]).astype(BF16)
                o = lax.dot_general(q_inter, st.astype(BF16), nt, preferred_element_type=F32)
                o = o + jnp.dot(att.astype(BF16), v, preferred_element_type=F32)
                o_ref[0, pl.ds(off, c), h * GLA_DV:(h + 1) * GLA_DV] = o
                upd.append(lax.dot_general(v, k_inter, tn, preferred_element_type=F32))
            st_ref[p] = e_tot[:, ls] * st + jnp.where(lane_v < GLA_DK, upd[0], upd[1])
        return carry

    lax.fori_loop(0, n_chunks, chunk, 0)


def _gla(gq, gk, gv, la, *, n_lat):
    n_all = gq.shape[0]
    tm = ROW_TILE
    nl = n_lat // tm
    nb = n_all // tm
    nc = nb - nl

    def blk(d, s):
        ctx = jnp.where(d == 0, nl + s, nb - 1 - s)
        lat = jnp.where(d == 0, s - nc, nb - 1 - s)
        return jnp.where(s < nc, ctx, lat)

    row = lambda d, s: (blk(d, s), 0)
    return pl.pallas_call(
        _gla_kernel,
        grid=(2, nb),
        in_specs=[pl.BlockSpec((tm, D_GLA_QK), row),
                  pl.BlockSpec((tm, D_GLA_QK), row),
                  pl.BlockSpec((tm, D_GLA_OUT), row),
                  pl.BlockSpec((1, tm, D_GLA_QK), lambda d, s: (d, blk(d, s), 0))],
        out_specs=pl.BlockSpec((1, tm, D_GLA_OUT), lambda d, s: (d, blk(d, s), 0)),
        out_shape=jax.ShapeDtypeStruct((2, n_all, D_GLA_OUT), F32),
        scratch_shapes=[pltpu.VMEM((N_GLA_HEADS // 2, GLA_DV, 2 * GLA_DK), F32)],
        compiler_params=_params("arbitrary", "arbitrary"),
        name="gla_scan",
    )(gq, gk, gv, la)


def _mixout_kernel(h_ref, mla_ref, og_ref, gg_ref, gn_ref, wout_ref, mod_ref, gpost_ref, o_ref):
    o = og_ref[0] + og_ref[1]
    gate = _silu(gg_ref[...])
    parts = [mla_ref[...]]
    for h in range(N_GLA_HEADS):
        sl = slice(h * GLA_DV, (h + 1) * GLA_DV)
        parts.append((_rms(o[:, sl], gn_ref[...]) * gate[:, sl]).astype(BF16))
    y = jnp.dot(jnp.concatenate(parts, axis=-1), wout_ref[...], preferred_element_type=F32)
    o_ref[...] = h_ref[...] + mod_ref[0][5:6] * _rms(y, gpost_ref[...])


def _mixout(h_all, mla, og, gg, g_norm, w_out, mod, g_post, *, n_lat):
    tm = ROW_TILE
    row = lambda i: (i, 0)
    const = lambda i: (0, 0)
    return pl.pallas_call(
        _mixout_kernel,
        grid=(n_lat // tm,),
        in_specs=[pl.BlockSpec((tm, D_MODEL), row),
                  pl.BlockSpec((tm, D_MLA_OUT), row),
                  pl.BlockSpec((2, tm, D_GLA_OUT), lambda i: (0, i, 0)),
                  pl.BlockSpec((tm, D_GLA_OUT), row),
                  pl.BlockSpec((1, GLA_DV), const),
                  pl.BlockSpec(w_out.shape, const),
                  _mod_spec(n_lat // tm),
                  pl.BlockSpec((1, D_MODEL), const)],
        out_specs=pl.BlockSpec((tm, D_MODEL), row),
        out_shape=jax.ShapeDtypeStruct((n_lat, D_MODEL), F32),
        compiler_params=_params("arbitrary"),
        name="mixer_out",
    )(h_all, mla, og, gg, g_norm, w_out, mod, g_post)


def _rope_half_swap(w):
    half = MLA_ROPE_DIM // 4
    g = w.reshape(w.shape[:-1] + (2, 2, half))
    return g[..., ::-1, :].reshape(w.shape)


def _layout_w_in(w):
    d = w.shape[0]
    o_kv = MLA_Q_RANK
    o_kr = o_kv + MLA_KV_RANK
    o_gq = o_kr + MLA_ROPE_DIM
    o_gk = o_gq + D_GLA_QK
    o_gv = o_gk + D_GLA_QK
    o_gg = o_gv + D_GLA_OUT
    o_ga = o_gg + D_GLA_OUT
    k_rope = w[:, o_kr:o_gq]
    z = lambda n: jnp.zeros((d, n), w.dtype)
    pad = LANES - ROPE_LANE0 - MLA_ROPE_DIM
    return jnp.concatenate([
        w[:, :o_kr], w[:, o_gq:o_ga],
        w[:, o_ga:o_ga + 2 * GLA_GATE_RANK], z(ROPE_LANE0 - 2 * GLA_GATE_RANK), k_rope, z(pad),
        z(ROPE_LANE0), _rope_half_swap(k_rope), z(pad)], axis=1)


def _layout_w_q(w):
    r = w.shape[0]
    w = w.reshape(r, N_MLA_HEADS, MLA_QK_DIM)
    nope, rope = w[..., :MLA_NOPE_DIM], w[..., MLA_NOPE_DIM:]
    zp = jnp.zeros((r, N_MLA_HEADS, HEAD_PAD - MLA_QK_DIM), w.dtype)
    zn = jnp.zeros_like(nope)
    plain = jnp.concatenate([nope, rope, zp], axis=-1).reshape(r, -1)
    swapped = jnp.concatenate([zn, _rope_half_swap(rope), zp], axis=-1).reshape(r, -1)
    return jnp.concatenate([plain, swapped], axis=1)


def _layout_w_kv(w):
    r = w.shape[0]
    w = w.reshape(r, N_MLA_HEADS, MLA_NOPE_DIM + MLA_V_DIM)
    k_nope, v = w[..., :MLA_NOPE_DIM], w[..., MLA_NOPE_DIM:]
    zk = jnp.zeros((r, N_MLA_HEADS, HEAD_PAD - MLA_NOPE_DIM), w.dtype)
    return jnp.concatenate([jnp.concatenate([k_nope, zk], axis=-1).reshape(r, -1), v.reshape(r, -1)], axis=1)


def _layout_w_a(w_f, w_b):
    z = jnp.zeros((LANES, 2 * D_GLA_QK), w_f.dtype)
    z = z.at[:GLA_GATE_RANK, :D_GLA_QK].set(w_f)
    return z.at[GLA_GATE_RANK:2 * GLA_GATE_RANK, D_GLA_QK:].set(w_b)


def _attn_key_block(n_all):
    for bk in (1280, 1024, 512, 256):
        if n_all % bk == 0:
            return bk
    raise ValueError(f"unsupported key count {n_all}")


def kernel(x, c, ctx, c_ctx, w_ada, b_ada, norm_pre, norm_post, ffn1_w_in, ffn1_w_out, ffn2_w_in, ffn2_w_out, w_in, mla_q_norm, mla_w_qb, mla_kv_norm, mla_w_kvb, gla_w_a_fwd, gla_b_a_fwd, gla_w_a_bwd, gla_b_a_bwd, gla_norm, w_out):
    assert x.shape[0] == 1 and ctx.shape[0] == 1 and w_ada.shape[0] == 1
    n_lat, n_ctx = x.shape[1], ctx.shape[1]
    assert n_lat % ROW_TILE == 0 and n_ctx % ROW_TILE == 0 and n_lat % GRID_W == 0
    n_all = n_lat + n_ctx
    row2 = lambda a: a.reshape(1, -1)

    x_all = jnp.concatenate([x[0], ctx[0]], axis=0)
    ct = jnp.stack([c[0], c_ctx], axis=1)
    mod = _ada(ct, w_ada[0], row2(b_ada[0])).reshape(2, N_MOD, D_MODEL)

    h_all = _ffn(x_all, mod, row2(norm_pre[0, 0]), row2(norm_post[0, 0]),
                 ffn1_w_in[0].astype(BF16), ffn1_w_out[0].astype(BF16), mod0=0, n_rows=n_all, n_lat=n_lat)

    q, k, v, gq, gk, gv, gg, la = _mixin(
        h_all, mod, row2(norm_pre[0, 1]), _layout_w_in(w_in[0]).astype(BF16),
        row2(mla_q_norm[0]), _layout_w_q(mla_w_qb[0]).astype(BF16),
        row2(mla_kv_norm[0]), _layout_w_kv(mla_w_kvb[0]).astype(BF16),
        _layout_w_a(gla_w_a_fwd[0], gla_w_a_bwd[0]),
        jnp.concatenate([gla_b_a_fwd[0], gla_b_a_bwd[0]]).reshape(1, -1), n_lat=n_lat)

    mla = _attn(q, k, v, n_lat=n_lat, bq=ROW_TILE, bk=_attn_key_block(n_all))
    og = _gla(gq, gk, gv, la, n_lat=n_lat)

    h_lat = _mixout(h_all, mla, og, gg, row2(gla_norm[0]), w_out[0].astype(BF16), mod,
                    row2(norm_post[0, 1]), n_lat=n_lat)
    out = _ffn(h_lat, mod, row2(norm_pre[0, 2]), row2(norm_post[0, 2]),
               ffn2_w_in[0].astype(BF16), ffn2_w_out[0].astype(BF16), mod0=6, n_rows=n_lat, n_lat=n_lat)
    return out[None]
```

```python
import functools
import math

import jax
import jax.numpy as jnp
from jax import lax
from jax.experimental import pallas as pl
from jax.experimental.pallas import tpu as pltpu

F32 = jnp.float32
BF16 = jnp.bfloat16

D_MODEL = 1024
GRID_W = 64
N_MLA_HEADS = 8
MLA_Q_RANK = 256
MLA_KV_RANK = 128
MLA_NOPE_DIM = 64
MLA_ROPE_DIM = 32
MLA_QK_DIM = MLA_NOPE_DIM + MLA_ROPE_DIM
MLA_V_DIM = 64
N_GLA_HEADS = 4
GLA_DK = 64
GLA_DV = 128
GLA_GATE_RANK = 16
GLA_GATE_NORM = 16.0
GLA_CHUNK = 64
D_FF = 2816
MACARON_WEIGHT = 0.5
ROPE_BASE = 10000.0
EPS = 1e-6
N_MOD = 9

D_MLA_OUT = N_MLA_HEADS * MLA_V_DIM
D_GLA_OUT = N_GLA_HEADS * GLA_DV
D_GLA_QK = N_GLA_HEADS * GLA_DK

LANES = 128
HEAD_PAD = 128
ROW_TILE = 256
ATTN_Q_TILE = 512
VMEM_LIMIT = 56 * 1024 * 1024
C_QA = 0
C_KV = C_QA + MLA_Q_RANK
C_GQ = C_KV + MLA_KV_RANK
C_GK = C_GQ + D_GLA_QK
C_GV = C_GK + D_GLA_QK
C_GG = C_GV + D_GLA_OUT
C_MISC = C_GG + D_GLA_OUT
C_KSW = C_MISC + LANES
C_TOTAL = C_KSW + LANES
ROPE_LANE0 = MLA_NOPE_DIM
GLA_LEVELS = (64, 32, 16, 8, 4, 2, 1)


def _rms(x, g):
    return x * lax.rsqrt(jnp.mean(x * x, axis=-1, keepdims=True) + EPS) * g


def _silu(x):
    return x * jax.nn.sigmoid(x)


def _params(*sem):
    return pltpu.CompilerParams(dimension_semantics=sem, vmem_limit_bytes=VMEM_LIMIT)


def _ada_kernel(ct_ref, w_ref, b_ref, o_ref):
    s = _silu(ct_ref[...])
    w = w_ref[...]
    r0 = jnp.sum(s[:, 0:1] * w, axis=0, keepdims=True)
    r1 = jnp.sum(s[:, 1:2] * w, axis=0, keepdims=True)
    o_ref[...] = jnp.concatenate([r0, r1], axis=0) + b_ref[...]


def _ada(ct, w, b):
    d, n = w.shape
    tn = n // 8
    return pl.pallas_call(
        _ada_kernel,
        grid=(n // tn,),
        in_specs=[pl.BlockSpec((d, 2), lambda j: (0, 0)),
                  pl.BlockSpec((d, tn), lambda j: (0, j)),
                  pl.BlockSpec((1, tn), lambda j: (0, j))],
        out_specs=pl.BlockSpec((2, tn), lambda j: (0, j)),
        out_shape=jax.ShapeDtypeStruct((2, n), F32),
        compiler_params=_params("arbitrary"),
        name="adaln",
    )(ct, w, b)


def _ffn_kernel(x_ref, mod_ref, gpre_ref, gpost_ref, win_ref, wout_ref, o_ref, *, mod0):
    x = x_ref[...]
    m = mod_ref[0]
    shift, scale, gate = m[mod0:mod0 + 1], m[mod0 + 1:mod0 + 2], m[mod0 + 2:mod0 + 3]
    h = _rms(x, gpre_ref[...]) * (1.0 + scale) + shift
    ab = jnp.dot(h.astype(BF16), win_ref[...], preferred_element_type=F32)
    a, b = ab[:, :D_FF], ab[:, D_FF:]
    g = (_silu(a) * b).astype(BF16)
    y = jnp.dot(g, wout_ref[...], preferred_element_type=F32)
    o_ref[...] = x + (MACARON_WEIGHT * gate) * _rms(y, gpost_ref[...])


def _mod_spec(n_lat_blocks):
    return pl.BlockSpec((1, N_MOD, D_MODEL), lambda i: (jnp.where(i < n_lat_blocks, 0, 1), 0, 0))


def _ffn(x, mod, g_pre, g_post, w_in, w_out, *, mod0, n_rows, n_lat):
    tm = ROW_TILE
    row = lambda i: (i, 0)
    const = lambda i: (0, 0)
    return pl.pallas_call(
        functools.partial(_ffn_kernel, mod0=mod0),
        grid=(n_rows // tm,),
        in_specs=[pl.BlockSpec((tm, D_MODEL), row),
                  _mod_spec(n_lat // tm),
                  pl.BlockSpec((1, D_MODEL), const),
                  pl.BlockSpec((1, D_MODEL), const),
                  pl.BlockSpec((D_MODEL, 2 * D_FF), const),
                  pl.BlockSpec((D_FF, D_MODEL), const)],
        out_specs=pl.BlockSpec((tm, D_MODEL), row),
        out_shape=jax.ShapeDtypeStruct((n_rows, D_MODEL), F32),
        compiler_params=_params("arbitrary"),
        name="ffn",
    )(x, mod, g_pre, g_post, w_in, w_out)


def _rope_tables(i, tm, is_lat):
    lane = lax.broadcasted_iota(jnp.int32, (tm, LANES), 1)
    rowi = lax.broadcasted_iota(jnp.int32, (tm, LANES), 0)
    e = lane - ROPE_LANE0
    in_rope = (e >= 0) & (e < MLA_ROPE_DIM)
    axis_dim = MLA_ROPE_DIM // 2
    grp = e >> 4
    w = e & (axis_dim - 1)
    j = w & (axis_dim // 2 - 1)
    first_half = w < axis_dim // 2
    inv_freq = jnp.exp(j.astype(F32) * (-math.log(ROPE_BASE) * 2.0 / axis_dim))
    pos = i * tm + rowi
    grid_row = (pos // GRID_W).astype(F32)
    grid_col = (pos % GRID_W).astype(F32)
    ang = jnp.where(grp == 0, grid_row, grid_col) * inv_freq
    cosv = jnp.where(is_lat, jnp.cos(ang), 1.0)
    sinv = jnp.where(is_lat, jnp.sin(ang), 0.0)
    cos_k = jnp.where(in_rope, cosv, 0.0)
    sin_s = jnp.where(in_rope, jnp.where(first_half, -sinv, sinv), 0.0)
    cos_q = jnp.where(lane < ROPE_LANE0, 1.0, cos_k)
    return cos_q, cos_k, sin_s


def _mixin_kernel(h_ref, mod_ref, gpre_ref, win_ref, qn_ref, wq_ref, kvn_ref, wkv_ref, wa_ref, ba_ref,
                  q_ref, k_ref, v_ref, gq_ref, gk_ref, gv_ref, gg_ref, la_ref, *, n_lat_blocks):
    i = pl.program_id(0)
    tm = h_ref.shape[0]
    m = mod_ref[0]
    shift, scale = m[3:4], m[4:5]
    a = _rms(h_ref[...], gpre_ref[...]) * (1.0 + scale) + shift
    z = jnp.dot(a.astype(BF16), win_ref[...], preferred_element_type=F32)

    cos_q, cos_k, sin_s = _rope_tables(i, tm, i < n_lat_blocks)

    qn = _rms(z[:, C_QA:C_QA + MLA_Q_RANK], qn_ref[...]).astype(BF16)
    qq = jnp.dot(qn, wq_ref[...], preferred_element_type=F32)
    hw = N_MLA_HEADS * HEAD_PAD
    sm_scale = MLA_QK_DIM ** -0.5 * math.log2(math.e)
    for h in range(N_MLA_HEADS):
        lo = h * HEAD_PAD
        qh = qq[:, lo:lo + HEAD_PAD] * cos_q + qq[:, hw + lo:hw + lo + HEAD_PAD] * sin_s
        q_ref[h] = (qh * sm_scale).astype(BF16)

    kvn = _rms(z[:, C_KV:C_KV + MLA_KV_RANK], kvn_ref[...]).astype(BF16)
    kv = jnp.dot(kvn, wkv_ref[...], preferred_element_type=F32)
    misc = z[:, C_MISC:C_MISC + LANES]
    k_rope = misc * cos_k + z[:, C_KSW:C_KSW + LANES] * sin_s
    lane = lax.broadcasted_iota(jnp.int32, (tm, HEAD_PAD), 1)
    ones_col = (lane == MLA_V_DIM).astype(F32)
    for h in range(N_MLA_HEADS):
        lo = h * HEAD_PAD
        k_ref[h] = (kv[:, lo:lo + HEAD_PAD] + k_rope).astype(BF16)
        v_ref[h] = (kv[:, hw + lo:hw + lo + HEAD_PAD] + ones_col).T.astype(BF16)

    gq_ref[...] = z[:, C_GQ:C_GQ + D_GLA_QK] * (GLA_DK ** -0.5)
    gk_ref[...] = z[:, C_GK:C_GK + D_GLA_QK]
    gv_ref[...] = z[:, C_GV:C_GV + D_GLA_OUT]
    gg_ref[...] = z[:, C_GG:C_GG + D_GLA_OUT]
    xg = jnp.dot(misc, wa_ref[...], preferred_element_type=F32,
                 precision=lax.Precision.HIGHEST) + ba_ref[...]
    la = (jnp.minimum(xg, 0.0) - jnp.log1p(jnp.exp(-jnp.abs(xg)))) * (1.0 / GLA_GATE_NORM)
    la_ref[0] = la[:, :D_GLA_QK]
    la_ref[1] = la[:, D_GLA_QK:]


def _mixin(h_all, mod, g_pre, w_in, q_norm, w_q, kv_norm, w_kv, w_a, b_a, *, n_lat):
    n_all = h_all.shape[0]
    tm = ROW_TILE
    row = lambda i: (i, 0)
    const = lambda i: (0, 0)
    hrow = lambda i: (0, i, 0)
    full = lambda arr: pl.BlockSpec(arr.shape, const)
    out_shape = (
        jax.ShapeDtypeStruct((N_MLA_HEADS, n_all, HEAD_PAD), BF16),
        jax.ShapeDtypeStruct((N_MLA_HEADS, n_all, HEAD_PAD), BF16),
        jax.ShapeDtypeStruct((N_MLA_HEADS, HEAD_PAD, n_all), BF16),
        jax.ShapeDtypeStruct((n_all, D_GLA_QK), F32),
        jax.ShapeDtypeStruct((n_all, D_GLA_QK), F32),
        jax.ShapeDtypeStruct((n_all, D_GLA_OUT), F32),
        jax.ShapeDtypeStruct((n_all, D_GLA_OUT), F32),
        jax.ShapeDtypeStruct((2, n_all, D_GLA_QK), F32),
    )
    out_specs = (
        pl.BlockSpec((N_MLA_HEADS, tm, HEAD_PAD), hrow),
        pl.BlockSpec((N_MLA_HEADS, tm, HEAD_PAD), hrow),
        pl.BlockSpec((N_MLA_HEADS, HEAD_PAD, tm), lambda i: (0, 0, i)),
        pl.BlockSpec((tm, D_GLA_QK), row),
        pl.BlockSpec((tm, D_GLA_QK), row),
        pl.BlockSpec((tm, D_GLA_OUT), row),
        pl.BlockSpec((tm, D_GLA_OUT), row),
        pl.BlockSpec((2, tm, D_GLA_QK), hrow),
    )
    return pl.pallas_call(
        functools.partial(_mixin_kernel, n_lat_blocks=n_lat // tm),
        grid=(n_all // tm,),
        in_specs=[pl.BlockSpec((tm, D_MODEL), row), _mod_spec(n_lat // tm),
                  full(g_pre), full(w_in), full(q_norm), full(w_q), full(kv_norm), full(w_kv),
                  full(w_a), full(b_a)],
        out_specs=out_specs,
        out_shape=out_shape,
        compiler_params=_params("arbitrary"),
        name="mixer_in",
    )(h_all, mod, g_pre, w_in, q_norm, w_q, kv_norm, w_kv, w_a, b_a)


def _attn_kernel(q_ref, k_ref, vt_ref, o_ref, sa_ref, sb_ref, *, bk):
    n_keys = k_ref.shape[1]
    bq = q_ref.shape[1]
    nt = (((1,), (1,)), ((), ()))
    qs = [q_ref[h] for h in range(2)]

    n_trips = n_keys // bk

    def scores(h, c):
        off = pl.multiple_of(c * bk, bk)
        return lax.dot_general(k_ref[h, pl.ds(off, bk), :], qs[h], nt,
                               preferred_element_type=F32)

    def fill(s_ref, c):
        for h in range(2):
            s_ref[h] = scores(h, c)

    def update(carry, s_ref, c):
        off = pl.multiple_of(c * bk, bk)
        new = []
        for h in range(2):
            m, acc = carry[h]
            m_new = jnp.maximum(m, jnp.max(s_ref[h], axis=0, keepdims=True))
            pt = jnp.exp2(s_ref[h] - m_new).astype(BF16)
            acc = jnp.exp2(m - m_new) * acc + jnp.dot(vt_ref[h, :, pl.ds(off, bk)], pt,
                                                      preferred_element_type=F32)
            new.append((m_new, acc))
        return tuple(new)

    def pair(i, carry):
        c = 2 * i
        fill(sb_ref, c + 1)
        carry = update(carry, sa_ref, c)
        fill(sa_ref, c + 2)
        return update(carry, sb_ref, c + 1)

    carry = tuple((jnp.full((1, bq), -jnp.inf, F32), jnp.zeros((HEAD_PAD, bq), F32)) for _ in range(2))
    fill(sa_ref, 0)
    n_pairs = (n_trips - 1) // 2
    carry = lax.fori_loop(0, n_pairs, pair, carry)
    if n_trips - 2 * n_pairs == 2:
        fill(sb_ref, n_trips - 1)
        carry = update(carry, sa_ref, n_trips - 2)
        carry = update(carry, sb_ref, n_trips - 1)
    else:
        carry = update(carry, sa_ref, n_trips - 1)
    outs = [acc[:MLA_V_DIM] / acc[MLA_V_DIM:MLA_V_DIM + 1] for _, acc in carry]
    o_ref[...] = jnp.concatenate(outs, axis=0).astype(o_ref.dtype)


def _attn(q, k, vt, *, n_lat, bq, bk):
    n_all = k.shape[1]
    return pl.pallas_call(
        functools.partial(_attn_kernel, bk=bk),
        grid=(N_MLA_HEADS // 2, n_lat // bq),
        in_specs=[pl.BlockSpec((2, bq, HEAD_PAD), lambda hp, i: (hp, i, 0)),
                  pl.BlockSpec((2, n_all, HEAD_PAD), lambda hp, i: (hp, 0, 0)),
                  pl.BlockSpec((2, HEAD_PAD, n_all), lambda hp, i: (hp, 0, 0))],
        out_specs=pl.BlockSpec((2 * MLA_V_DIM, bq), lambda hp, i: (hp, i)),
        out_shape=jax.ShapeDtypeStruct((D_MLA_OUT, n_lat), BF16),
        scratch_shapes=[pltpu.VMEM((2, bk, bq), F32), pltpu.VMEM((2, bk, bq), F32)],
        compiler_params=_params("arbitrary", "arbitrary"),
        name="mla_attention",
    )(q, k, vt)


def _gla_level_matrices(fwd):
    c = GLA_CHUNK
    ri = lax.broadcasted_iota(jnp.int32, (c, c), 0)
    ci = lax.broadcasted_iota(jnp.int32, (c, c), 1)
    q_mats, k_mats = [], []
    for b in GLA_LEVELS:
        bs = ri & (~(b - 1))
        be = bs + (b - 1)
        tq_f = (ci >= bs) & (ci <= ri)
        tk_f = (ci > ri) & (ci <= be)
        tq_b = (ci >= ri) & (ci <= be)
        tk_b = (ci < ri) & (ci >= bs)
        q_mats.append(jnp.where(fwd, tq_f.astype(F32), tq_b.astype(F32)))
        k_mats.append(jnp.where(fwd, tk_f.astype(F32), tk_b.astype(F32)))
    return jnp.concatenate(q_mats + k_mats, axis=0)


def _gla_kernel(q_ref, k_ref, v_ref, la_ref, o_ref, st_ref):
    d = pl.program_id(0)
    s = pl.program_id(1)
    fwd = d == 0
    c = GLA_CHUNK
    tm = q_ref.shape[0]
    n_chunks = tm // c
    n_lv = len(GLA_LEVELS)
    nt = (((1,), (1,)), ((), ()))
    tn = (((0,), (0,)), ((), ()))

    @pl.when(s == 0)
    def _():
        st_ref[...] = jnp.zeros_like(st_ref)

    w_lv = _gla_level_matrices(fwd)
    ri = lax.broadcasted_iota(jnp.int32, (c, c), 0)
    ci = lax.broadcasted_iota(jnp.int32, (c, c), 1)
    hi = jnp.where(fwd, ri, ci)
    xr = ri ^ ci
    lane = lax.broadcasted_iota(jnp.int32, (c, LANES), 1)
    head_mask = (lane < GLA_DK, lane >= GLA_DK)
    lane_v = lax.broadcasted_iota(jnp.int32, (GLA_DV, LANES), 1)

    def chunk(t, carry):
        cidx = jnp.where(fwd, t, n_chunks - 1 - t)
        off = pl.multiple_of(cidx * c, c)
        g = la_ref[0, pl.ds(off, c), :]
        e_all = jnp.exp(jnp.dot(w_lv, g, preferred_element_type=F32,
                                precision=lax.Precision.HIGHEST))
        e_tot = jnp.exp(jnp.sum(g, axis=0, keepdims=True))
        q_all = q_ref[pl.ds(off, c), :]
        k_all = k_ref[pl.ds(off, c), :]
        for p in range(N_GLA_HEADS // 2):
            ls = slice(p * LANES, (p + 1) * LANES)
            qp, kp = q_all[:, ls], k_all[:, ls]
            st = st_ref[p]
            k_inter = (kp * e_all[n_lv * c:(n_lv + 1) * c, ls]).astype(BF16)
            upd = []
            for hh in range(2):
                h = 2 * p + hh
                v = v_ref[pl.ds(off, c), h * GLA_DV:(h + 1) * GLA_DV].astype(BF16)
                qm = jnp.where(head_mask[hh], qp, 0.0)
                att = jnp.where(ri == ci,
                                lax.dot_general(qm.astype(BF16), kp.astype(BF16), nt,
                                                preferred_element_type=F32), 0.0)
                for li, b in enumerate(GLA_LEVELS):
                    if b == c:
                        continue
                    ql = (qm * e_all[li * c:(li + 1) * c, ls]).astype(BF16)
                    kl = (kp * e_all[(n_lv + li) * c:(n_lv + li + 1) * c, ls]).astype(BF16)
                    sel = (xr >= b) & (xr < 2 * b) & ((hi & b) != 0)
                    att = att + jnp.where(sel, lax.dot_general(ql, kl, nt, preferred_element_type=F32), 0.0)
                q_inter = (qm * e_all[0:c, ls]).astype(BF16)
                o = lax.dot_general(q_inter, st.astype(BF16), nt, preferred_element_type=F32)
                o = o + jnp.dot(att.astype(BF16), v, preferred_element_type=F32)
                o_ref[0, pl.ds(off, c), h * GLA_DV:(h + 1) * GLA_DV] = o
                upd.append(lax.dot_general(v, k_inter, tn, preferred_element_type=F32))
            st_ref[p] = e_tot[:, ls] * st + jnp.where(lane_v < GLA_DK, upd[0], upd[1])
        return carry

    lax.fori_loop(0, n_chunks, chunk, 0)


def _gla(gq, gk, gv, la, *, n_lat):
    n_all = gq.shape[0]
    tm = ROW_TILE
    nl = n_lat // tm
    nb = n_all // tm
    nc = nb - nl

    def blk(d, s):
        ctx = jnp.where(d == 0, nl + s, nb - 1 - s)
        lat = jnp.where(d == 0, s - nc, nb - 1 - s)
        return jnp.where(s < nc, ctx, lat)

    row = lambda d, s: (blk(d, s), 0)
    return pl.pallas_call(
        _gla_kernel,
        grid=(2, nb),
        in_specs=[pl.BlockSpec((tm, D_GLA_QK), row),
                  pl.BlockSpec((tm, D_GLA_QK), row),
                  pl.BlockSpec((tm, D_GLA_OUT), row),
                  pl.BlockSpec((1, tm, D_GLA_QK), lambda d, s: (d, blk(d, s), 0))],
        out_specs=pl.BlockSpec((1, tm, D_GLA_OUT), lambda d, s: (d, blk(d, s), 0)),
        out_shape=jax.ShapeDtypeStruct((2, n_all, D_GLA_OUT), F32),
        scratch_shapes=[pltpu.VMEM((N_GLA_HEADS // 2, GLA_DV, 2 * GLA_DK), F32)],
        compiler_params=_params("arbitrary", "arbitrary"),
        name="gla_scan",
    )(gq, gk, gv, la)


def _mixout_kernel(h_ref, mlat_ref, og_ref, gg_ref, gn_ref, wout_ref, mod_ref, gpost_ref, o_ref):
    o = og_ref[0] + og_ref[1]
    gate = _silu(gg_ref[...])
    parts = []
    for h in range(N_GLA_HEADS):
        sl = slice(h * GLA_DV, (h + 1) * GLA_DV)
        parts.append((_rms(o[:, sl], gn_ref[...]) * gate[:, sl]).astype(BF16))
    tn = (((0,), (0,)), ((), ()))
    y = lax.dot_general(mlat_ref[...], wout_ref[:D_MLA_OUT, :], tn, preferred_element_type=F32)
    y = y + jnp.dot(jnp.concatenate(parts, axis=-1), wout_ref[D_MLA_OUT:, :], preferred_element_type=F32)
    o_ref[...] = h_ref[...] + mod_ref[0][5:6] * _rms(y, gpost_ref[...])


def _mixout(h_all, mla, og, gg, g_norm, w_out, mod, g_post, *, n_lat):
    tm = ROW_TILE
    row = lambda i: (i, 0)
    const = lambda i: (0, 0)
    return pl.pallas_call(
        _mixout_kernel,
        grid=(n_lat // tm,),
        in_specs=[pl.BlockSpec((tm, D_MODEL), row),
                  pl.BlockSpec((D_MLA_OUT, tm), lambda i: (0, i)),
                  pl.BlockSpec((2, tm, D_GLA_OUT), lambda i: (0, i, 0)),
                  pl.BlockSpec((tm, D_GLA_OUT), row),
                  pl.BlockSpec((1, GLA_DV), const),
                  pl.BlockSpec(w_out.shape, const),
                  _mod_spec(n_lat // tm),
                  pl.BlockSpec((1, D_MODEL), const)],
        out_specs=pl.BlockSpec((tm, D_MODEL), row),
        out_shape=jax.ShapeDtypeStruct((n_lat, D_MODEL), F32),
        compiler_params=_params("arbitrary"),
        name="mixer_out",
    )(h_all, mla, og, gg, g_norm, w_out, mod, g_post)


def _rope_half_swap(w):
    half = MLA_ROPE_DIM // 4
    g = w.reshape(w.shape[:-1] + (2, 2, half))
    return g[..., ::-1, :].reshape(w.shape)


def _layout_w_in(w):
    d = w.shape[0]
    o_kv = MLA_Q_RANK
    o_kr = o_kv + MLA_KV_RANK
    o_gq = o_kr + MLA_ROPE_DIM
    o_gk = o_gq + D_GLA_QK
    o_gv = o_gk + D_GLA_QK
    o_gg = o_gv + D_GLA_OUT
    o_ga = o_gg + D_GLA_OUT
    k_rope = w[:, o_kr:o_gq]
    z = lambda n: jnp.zeros((d, n), w.dtype)
    pad = LANES - ROPE_LANE0 - MLA_ROPE_DIM
    return jnp.concatenate([
        w[:, :o_kr], w[:, o_gq:o_ga],
        w[:, o_ga:o_ga + 2 * GLA_GATE_RANK], z(ROPE_LANE0 - 2 * GLA_GATE_RANK), k_rope, z(pad),
        z(ROPE_LANE0), _rope_half_swap(k_rope), z(pad)], axis=1)


def _layout_w_q(w):
    r = w.shape[0]
    w = w.reshape(r, N_MLA_HEADS, MLA_QK_DIM)
    nope, rope = w[..., :MLA_NOPE_DIM], w[..., MLA_NOPE_DIM:]
    zp = jnp.zeros((r, N_MLA_HEADS, HEAD_PAD - MLA_QK_DIM), w.dtype)
    zn = jnp.zeros_like(nope)
    plain = jnp.concatenate([nope, rope, zp], axis=-1).reshape(r, -1)
    swapped = jnp.concatenate([zn, _rope_half_swap(rope), zp], axis=-1).reshape(r, -1)
    return jnp.concatenate([plain, swapped], axis=1)


def _layout_w_kv(w):
    r = w.shape[0]
    w = w.reshape(r, N_MLA_HEADS, MLA_NOPE_DIM + MLA_V_DIM)
    k_nope, v = w[..., :MLA_NOPE_DIM], w[..., MLA_NOPE_DIM:]
    zk = jnp.zeros((r, N_MLA_HEADS, HEAD_PAD - MLA_NOPE_DIM), w.dtype)
    zv = jnp.zeros((r, N_MLA_HEADS, HEAD_PAD - MLA_V_DIM), w.dtype)
    return jnp.concatenate([jnp.concatenate([k_nope, zk], axis=-1).reshape(r, -1),
                            jnp.concatenate([v, zv], axis=-1).reshape(r, -1)], axis=1)


def _layout_w_a(w_f, w_b):
    z = jnp.zeros((LANES, 2 * D_GLA_QK), w_f.dtype)
    z = z.at[:GLA_GATE_RANK, :D_GLA_QK].set(w_f)
    return z.at[GLA_GATE_RANK:2 * GLA_GATE_RANK, D_GLA_QK:].set(w_b)


def _attn_key_block(n_all):
    for bk in (640, 512, 256):
        if n_all % bk == 0:
            return bk
    raise ValueError(f"unsupported key count {n_all}")


def kernel(x, c, ctx, c_ctx, w_ada, b_ada, norm_pre, norm_post, ffn1_w_in, ffn1_w_out, ffn2_w_in, ffn2_w_out, w_in, mla_q_norm, mla_w_qb, mla_kv_norm, mla_w_kvb, gla_w_a_fwd, gla_b_a_fwd, gla_w_a_bwd, gla_b_a_bwd, gla_norm, w_out):
    assert x.shape[0] == 1 and ctx.shape[0] == 1 and w_ada.shape[0] == 1
    n_lat, n_ctx = x.shape[1], ctx.shape[1]
    assert n_lat % ROW_TILE == 0 and n_ctx % ROW_TILE == 0 and n_lat % GRID_W == 0
    n_all = n_lat + n_ctx
    row2 = lambda a: a.reshape(1, -1)

    x_all = jnp.concatenate([x[0], ctx[0]], axis=0)
    ct = jnp.stack([c[0], c_ctx], axis=1)
    mod = _ada(ct, w_ada[0], row2(b_ada[0])).reshape(2, N_MOD, D_MODEL)

    h_all = _ffn(x_all, mod, row2(norm_pre[0, 0]), row2(norm_post[0, 0]),
                 ffn1_w_in[0].astype(BF16), ffn1_w_out[0].astype(BF16), mod0=0, n_rows=n_all, n_lat=n_lat)

    q, k, v, gq, gk, gv, gg, la = _mixin(
        h_all, mod, row2(norm_pre[0, 1]), _layout_w_in(w_in[0]).astype(BF16),
        row2(mla_q_norm[0]), _layout_w_q(mla_w_qb[0]).astype(BF16),
        row2(mla_kv_norm[0]), _layout_w_kv(mla_w_kvb[0]).astype(BF16),
        _layout_w_a(gla_w_a_fwd[0], gla_w_a_bwd[0]),
        jnp.concatenate([gla_b_a_fwd[0], gla_b_a_bwd[0]]).reshape(1, -1), n_lat=n_lat)

    mla = _attn(q, k, v, n_lat=n_lat, bq=ATTN_Q_TILE, bk=_attn_key_block(n_all))
    og = _gla(gq, gk, gv, la, n_lat=n_lat)

    h_lat = _mixout(h_all, mla, og, gg, row2(gla_norm[0]), w_out[0].astype(BF16), mod,
                    row2(norm_post[0, 1]), n_lat=n_lat)
    out = _ffn(h_lat, mod, row2(norm_pre[0, 2]), row2(norm_post[0, 2]),
               ffn2_w_in[0].astype(BF16), ffn2_w_out[0].astype(BF16), mod0=6, n_rows=n_lat, n_lat=n_lat)
    return out[None]
```

```python
import functools
import math

import jax
import jax.numpy as jnp
from jax import lax
from jax.experimental import pallas as pl
from jax.experimental.pallas import tpu as pltpu

F32 = jnp.float32
BF16 = jnp.bfloat16

D_MODEL = 1024
GRID_W = 64
N_MLA_HEADS = 8
MLA_Q_RANK = 256
MLA_KV_RANK = 128
MLA_NOPE_DIM = 64
MLA_ROPE_DIM = 32
MLA_QK_DIM = MLA_NOPE_DIM + MLA_ROPE_DIM
MLA_V_DIM = 64
N_GLA_HEADS = 4
GLA_DK = 64
GLA_DV = 128
GLA_GATE_RANK = 16
GLA_GATE_NORM = 16.0
GLA_CHUNK = 64
D_FF = 2816
MACARON_WEIGHT = 0.5
ROPE_BASE = 10000.0
EPS = 1e-6
N_MOD = 9

D_MLA_OUT = N_MLA_HEADS * MLA_V_DIM
D_GLA_OUT = N_GLA_HEADS * GLA_DV
D_GLA_QK = N_GLA_HEADS * GLA_DK

LANES = 128
HEAD_PAD = 128
VT_ROWS = 80
ROW_TILE = 256
ATTN_Q_TILE = 512
VMEM_LIMIT = 56 * 1024 * 1024
C_QA = 0
C_KV = C_QA + MLA_Q_RANK
C_GQ = C_KV + MLA_KV_RANK
C_GK = C_GQ + D_GLA_QK
C_GV = C_GK + D_GLA_QK
C_GG = C_GV + D_GLA_OUT
C_MISC = C_GG + D_GLA_OUT
C_KSW = C_MISC + LANES
C_TOTAL = C_KSW + LANES
ROPE_LANE0 = MLA_NOPE_DIM
GLA_SUB_LEVELS = (32, 16, 8, 4, 2, 1)
GLA_DIAG = 64


def _rms(x, g):
    return x * lax.rsqrt(jnp.mean(x * x, axis=-1, keepdims=True) + EPS) * g


def _silu(x):
    return x * jax.nn.sigmoid(x)


def _params(*sem):
    return pltpu.CompilerParams(dimension_semantics=sem, vmem_limit_bytes=VMEM_LIMIT)


def _ada_kernel(ct_ref, w_ref, b_ref, o_ref):
    s = _silu(ct_ref[...])
    w = w_ref[...]
    r0 = jnp.sum(s[:, 0:1] * w, axis=0, keepdims=True)
    r1 = jnp.sum(s[:, 1:2] * w, axis=0, keepdims=True)
    o_ref[...] = jnp.concatenate([r0, r1], axis=0) + b_ref[...]


def _ada(ct, w, b):
    d, n = w.shape
    tn = n // 8
    return pl.pallas_call(
        _ada_kernel,
        grid=(n // tn,),
        in_specs=[pl.BlockSpec((d, 2), lambda j: (0, 0)),
                  pl.BlockSpec((d, tn), lambda j: (0, j)),
                  pl.BlockSpec((1, tn), lambda j: (0, j))],
        out_specs=pl.BlockSpec((2, tn), lambda j: (0, j)),
        out_shape=jax.ShapeDtypeStruct((2, n), F32),
        compiler_params=_params("arbitrary"),
        name="adaln",
    )(ct, w, b)


def _ffn_kernel(x_ref, mod_ref, gpre_ref, gpost_ref, win_ref, wout_ref, o_ref, *, mod0):
    x = x_ref[...]
    m = mod_ref[0]
    shift, scale, gate = m[mod0:mod0 + 1], m[mod0 + 1:mod0 + 2], m[mod0 + 2:mod0 + 3]
    h = _rms(x, gpre_ref[...]) * (1.0 + scale) + shift
    ab = jnp.dot(h.astype(BF16), win_ref[...], preferred_element_type=F32)
    a, b = ab[:, :D_FF], ab[:, D_FF:]
    g = (_silu(a) * b).astype(BF16)
    y = jnp.dot(g, wout_ref[...], preferred_element_type=F32)
    o_ref[...] = x + (MACARON_WEIGHT * gate) * _rms(y, gpost_ref[...])


def _mod_spec(n_lat_blocks):
    return pl.BlockSpec((1, N_MOD, D_MODEL), lambda i: (jnp.where(i < n_lat_blocks, 0, 1), 0, 0))


def _ffn(x, mod, g_pre, g_post, w_in, w_out, *, mod0, n_rows, n_lat):
    tm = ROW_TILE
    row = lambda i: (i, 0)
    const = lambda i: (0, 0)
    return pl.pallas_call(
        functools.partial(_ffn_kernel, mod0=mod0),
        grid=(n_rows // tm,),
        in_specs=[pl.BlockSpec((tm, D_MODEL), row),
                  _mod_spec(n_lat // tm),
                  pl.BlockSpec((1, D_MODEL), const),
                  pl.BlockSpec((1, D_MODEL), const),
                  pl.BlockSpec((D_MODEL, 2 * D_FF), const),
                  pl.BlockSpec((D_FF, D_MODEL), const)],
        out_specs=pl.BlockSpec((tm, D_MODEL), row),
        out_shape=jax.ShapeDtypeStruct((n_rows, D_MODEL), F32),
        compiler_params=_params("arbitrary"),
        name="ffn",
    )(x, mod, g_pre, g_post, w_in, w_out)


def _rope_tables(i, tm, is_lat):
    lane = lax.broadcasted_iota(jnp.int32, (tm, LANES), 1)
    rowi = lax.broadcasted_iota(jnp.int32, (tm, LANES), 0)
    e = lane - ROPE_LANE0
    in_rope = (e >= 0) & (e < MLA_ROPE_DIM)
    axis_dim = MLA_ROPE_DIM // 2
    grp = e >> 4
    w = e & (axis_dim - 1)
    j = w & (axis_dim // 2 - 1)
    first_half = w < axis_dim // 2
    inv_freq = jnp.exp(j.astype(F32) * (-math.log(ROPE_BASE) * 2.0 / axis_dim))
    pos = i * tm + rowi
    grid_row = (pos // GRID_W).astype(F32)
    grid_col = (pos % GRID_W).astype(F32)
    ang = jnp.where(grp == 0, grid_row, grid_col) * inv_freq
    cosv = jnp.where(is_lat, jnp.cos(ang), 1.0)
    sinv = jnp.where(is_lat, jnp.sin(ang), 0.0)
    cos_k = jnp.where(in_rope, cosv, 0.0)
    sin_s = jnp.where(in_rope, jnp.where(first_half, -sinv, sinv), 0.0)
    cos_q = jnp.where(lane < ROPE_LANE0, 1.0, cos_k)
    return cos_q, cos_k, sin_s


def _mixin_kernel(h_ref, mod_ref, gpre_ref, win_ref, qn_ref, wq_ref, kvn_ref, wkv_ref, wa_ref, ba_ref,
                  q_ref, k_ref, v_ref, gq_ref, gk_ref, gv_ref, gg_ref, la_ref, *, n_lat_blocks):
    i = pl.program_id(0)
    tm = h_ref.shape[0]
    m = mod_ref[0]
    shift, scale = m[3:4], m[4:5]
    a = _rms(h_ref[...], gpre_ref[...]) * (1.0 + scale) + shift
    z = jnp.dot(a.astype(BF16), win_ref[...], preferred_element_type=F32)

    cos_q, cos_k, sin_s = _rope_tables(i, tm, i < n_lat_blocks)

    qn = _rms(z[:, C_QA:C_QA + MLA_Q_RANK], qn_ref[...]).astype(BF16)
    qq = jnp.dot(qn, wq_ref[...], preferred_element_type=F32)
    hw = N_MLA_HEADS * HEAD_PAD
    sm_scale = MLA_QK_DIM ** -0.5 * math.log2(math.e)
    for h in range(N_MLA_HEADS):
        lo = h * HEAD_PAD
        qh = qq[:, lo:lo + HEAD_PAD] * cos_q + qq[:, hw + lo:hw + lo + HEAD_PAD] * sin_s
        q_ref[h] = (qh * sm_scale).astype(BF16)

    kvn = _rms(z[:, C_KV:C_KV + MLA_KV_RANK], kvn_ref[...]).astype(BF16)
    kv = jnp.dot(kvn, wkv_ref[...], preferred_element_type=F32)
    misc = z[:, C_MISC:C_MISC + LANES]
    k_rope = misc * cos_k + z[:, C_KSW:C_KSW + LANES] * sin_s
    lane = lax.broadcasted_iota(jnp.int32, (tm, HEAD_PAD), 1)
    ones_col = (lane == MLA_V_DIM).astype(F32)
    for h in range(N_MLA_HEADS):
        lo = h * HEAD_PAD
        k_ref[h] = (kv[:, lo:lo + HEAD_PAD] + k_rope).astype(BF16)
        v_ref[h] = (kv[:, hw + lo:hw + lo + HEAD_PAD] + ones_col).T[:VT_ROWS].astype(BF16)

    gq_ref[...] = z[:, C_GQ:C_GQ + D_GLA_QK] * (GLA_DK ** -0.5)
    gk_ref[...] = z[:, C_GK:C_GK + D_GLA_QK]
    gv_ref[...] = z[:, C_GV:C_GV + D_GLA_OUT]
    gg_ref[...] = z[:, C_GG:C_GG + D_GLA_OUT]
    xg = jnp.dot(misc, wa_ref[...], preferred_element_type=F32,
                 precision=lax.Precision.HIGHEST) + ba_ref[...]
    la = (jnp.minimum(xg, 0.0) - jnp.log1p(jnp.exp(-jnp.abs(xg)))) * (1.0 / GLA_GATE_NORM)
    la_ref[0] = la[:, :D_GLA_QK]
    la_ref[1] = la[:, D_GLA_QK:]


def _mixin(h_all, mod, g_pre, w_in, q_norm, w_q, kv_norm, w_kv, w_a, b_a, *, n_lat):
    n_all = h_all.shape[0]
    tm = ROW_TILE
    row = lambda i: (i, 0)
    const = lambda i: (0, 0)
    hrow = lambda i: (0, i, 0)
    full = lambda arr: pl.BlockSpec(arr.shape, const)
    out_shape = (
        jax.ShapeDtypeStruct((N_MLA_HEADS, n_all, HEAD_PAD), BF16),
        jax.ShapeDtypeStruct((N_MLA_HEADS, n_all, HEAD_PAD), BF16),
        jax.ShapeDtypeStruct((N_MLA_HEADS, VT_ROWS, n_all), BF16),
        jax.ShapeDtypeStruct((n_all, D_GLA_QK), F32),
        jax.ShapeDtypeStruct((n_all, D_GLA_QK), F32),
        jax.ShapeDtypeStruct((n_all, D_GLA_OUT), F32),
        jax.ShapeDtypeStruct((n_all, D_GLA_OUT), F32),
        jax.ShapeDtypeStruct((2, n_all, D_GLA_QK), F32),
    )
    out_specs = (
        pl.BlockSpec((N_MLA_HEADS, tm, HEAD_PAD), hrow),
        pl.BlockSpec((N_MLA_HEADS, tm, HEAD_PAD), hrow),
        pl.BlockSpec((N_MLA_HEADS, VT_ROWS, tm), lambda i: (0, 0, i)),
        pl.BlockSpec((tm, D_GLA_QK), row),
        pl.BlockSpec((tm, D_GLA_QK), row),
        pl.BlockSpec((tm, D_GLA_OUT), row),
        pl.BlockSpec((tm, D_GLA_OUT), row),
        pl.BlockSpec((2, tm, D_GLA_QK), hrow),
    )
    return pl.pallas_call(
        functools.partial(_mixin_kernel, n_lat_blocks=n_lat // tm),
        grid=(n_all // tm,),
        in_specs=[pl.BlockSpec((tm, D_MODEL), row), _mod_spec(n_lat // tm),
                  full(g_pre), full(w_in), full(q_norm), full(w_q), full(kv_norm), full(w_kv),
                  full(w_a), full(b_a)],
        out_specs=out_specs,
        out_shape=out_shape,
        compiler_params=_params("arbitrary"),
        name="mixer_in",
    )(h_all, mod, g_pre, w_in, q_norm, w_q, kv_norm, w_kv, w_a, b_a)


def _attn_kernel(q_ref, k_ref, vt_ref, o_ref, sa_ref, sb_ref, *, bk):
    n_keys = k_ref.shape[1]
    bq = q_ref.shape[1]
    nt = (((1,), (1,)), ((), ()))
    qs = [q_ref[h] for h in range(2)]

    n_trips = n_keys // bk

    def scores(h, c):
        off = pl.multiple_of(c * bk, bk)
        return lax.dot_general(k_ref[h, pl.ds(off, bk), :], qs[h], nt,
                               preferred_element_type=F32)

    def fill(s_ref, c):
        for h in range(2):
            s_ref[h] = scores(h, c)

    def update(carry, s_ref, c):
        off = pl.multiple_of(c * bk, bk)
        new = []
        for h in range(2):
            m, acc = carry[h]
            m_new = jnp.maximum(m, jnp.max(s_ref[h], axis=0, keepdims=True))
            pt = jnp.exp2(s_ref[h] - m_new).astype(BF16)
            acc = jnp.exp2(m - m_new) * acc + jnp.dot(vt_ref[h, :, pl.ds(off, bk)], pt,
                                                      preferred_element_type=F32)
            new.append((m_new, acc))
        return tuple(new)

    def pair(i, carry):
        c = 2 * i
        fill(sb_ref, c + 1)
        carry = update(carry, sa_ref, c)
        fill(sa_ref, c + 2)
        return update(carry, sb_ref, c + 1)

    carry = tuple((jnp.full((1, bq), -jnp.inf, F32), jnp.zeros((VT_ROWS, bq), F32)) for _ in range(2))
    fill(sa_ref, 0)
    n_pairs = (n_trips - 1) // 2
    carry = lax.fori_loop(0, n_pairs, pair, carry)
    if n_trips - 2 * n_pairs == 2:
        fill(sb_ref, n_trips - 1)
        carry = update(carry, sa_ref, n_trips - 2)
        carry = update(carry, sb_ref, n_trips - 1)
    else:
        carry = update(carry, sa_ref, n_trips - 1)
    outs = [acc[:MLA_V_DIM] / acc[MLA_V_DIM:MLA_V_DIM + 1] for _, acc in carry]
    o_ref[...] = jnp.concatenate(outs, axis=0).astype(o_ref.dtype)


def _attn(q, k, vt, *, n_lat, bq, bk):
    n_all = k.shape[1]
    return pl.pallas_call(
        functools.partial(_attn_kernel, bk=bk),
        grid=(N_MLA_HEADS // 2, n_lat // bq),
        in_specs=[pl.BlockSpec((2, bq, HEAD_PAD), lambda hp, i: (hp, i, 0)),
                  pl.BlockSpec((2, n_all, HEAD_PAD), lambda hp, i: (hp, 0, 0)),
                  pl.BlockSpec((2, VT_ROWS, n_all), lambda hp, i: (hp, 0, 0))],
        out_specs=pl.BlockSpec((2 * MLA_V_DIM, bq), lambda hp, i: (hp, i)),
        out_shape=jax.ShapeDtypeStruct((D_MLA_OUT, n_lat), BF16),
        scratch_shapes=[pltpu.VMEM((2, bk, bq), F32), pltpu.VMEM((2, bk, bq), F32)],
        compiler_params=_params("arbitrary", "arbitrary"),
        name="mla_attention",
    )(q, k, vt)


def _gla_level_matrix(fwd):
    c = GLA_CHUNK
    ri = lax.broadcasted_iota(jnp.int32, (c, c), 0)
    ci = lax.broadcasted_iota(jnp.int32, (c, c), 1)

    def sides(b):
        bs = ri & (~(b - 1))
        be = bs + (b - 1)
        q_f = ((ci >= bs) & (ci <= ri)).astype(F32)
        k_f = ((ci > ri) & (ci <= be)).astype(F32)
        q_b = ((ci >= ri) & (ci <= be)).astype(F32)
        k_b = ((ci < ri) & (ci >= bs)).astype(F32)
        return q_f, k_f, q_b, k_b

    q_f, k_f, q_b, k_b = sides(c)
    blocks = [jnp.where(fwd, q_f, q_b), jnp.where(fwd, k_f, k_b)]
    for b in GLA_SUB_LEVELS:
        q_f, k_f, q_b, k_b = sides(b)
        bit = (ri & b) != 0
        blocks.append(jnp.where(fwd, jnp.where(bit, q_f, k_f), jnp.where(bit, k_b, q_b)))
    return jnp.concatenate(blocks, axis=0)


def _gla_level_ids(fwd, n):
    ri = lax.broadcasted_iota(jnp.int32, (n, n), 0)
    ci = lax.broadcasted_iota(jnp.int32, (n, n), 1)
    xr = ri ^ ci
    later = jnp.where(fwd, ri, ci)
    ids = jnp.where(xr == 0, GLA_DIAG, 0)
    for b in GLA_SUB_LEVELS:
        ids = jnp.where(((xr & (-b)) == b) & ((later & b) != 0), b, ids)
    return ids


def _gla_kernel(q_ref, k_ref, v_ref, la_ref, o_ref, st_ref, w3_ref, ids_ref):
    d = pl.program_id(0)
    s = pl.program_id(1)
    fwd = d == 0
    c = GLA_CHUNK
    tm = q_ref.shape[0]
    n_chunks = tm // c
    n_pairs = N_GLA_HEADS // 2
    nt = (((1,), (1,)), ((), ()))
    tn = (((0,), (0,)), ((), ()))

    @pl.when(s == 0)
    def _():
        st_ref[...] = jnp.zeros_like(st_ref)
        w = _gla_level_matrix(fwd).astype(BF16)
        w3_ref[...] = jnp.concatenate([w, w, w], axis=1)
        ids_ref[...] = _gla_level_ids(fwd, tm)

    g = la_ref[0]
    g = jnp.concatenate([g[t * c:(t + 1) * c] for t in range(n_chunks)], axis=1)
    g_hi = g.astype(BF16)
    r1 = g - g_hi.astype(F32)
    g_mid = r1.astype(BF16)
    g_lo = (r1 - g_mid.astype(F32)).astype(BF16)
    e_all = jnp.exp(jnp.dot(w3_ref[...], jnp.concatenate([g_hi, g_mid, g_lo], axis=0),
                            preferred_element_type=F32))

    def e_block(blk, p):
        return jnp.concatenate(
            [e_all[blk * c:(blk + 1) * c, t * D_GLA_QK + p * LANES:t * D_GLA_QK + (p + 1) * LANES]
             for t in range(n_chunks)], axis=0)

    lane = lax.broadcasted_iota(jnp.int32, (tm, LANES), 1)
    lane_v = lax.broadcasted_iota(jnp.int32, (GLA_DV, LANES), 1)
    ids = ids_ref[...]
    ids2 = jnp.concatenate([ids, ids], axis=0)
    vb = v_ref[...].astype(BF16)

    for p in range(n_pairs):
        ls = slice(p * LANES, (p + 1) * LANES)
        qp, kp = q_ref[:, ls], k_ref[:, ls]
        qm = (jnp.where(lane < GLA_DK, qp, 0.0), jnp.where(lane >= GLA_DK, qp, 0.0))

        att = jnp.where(ids2 == GLA_DIAG,
                        lax.dot_general(jnp.concatenate(qm, axis=0).astype(BF16), kp.astype(BF16), nt,
                                        preferred_element_type=F32), 0.0)
        for li, b in enumerate(GLA_SUB_LEVELS):
            eb = e_block(2 + li, p)
            lhs = jnp.concatenate([qm[0] * eb, qm[1] * eb], axis=0).astype(BF16)
            part = lax.dot_general(lhs, (kp * eb).astype(BF16), nt, preferred_element_type=F32)
            att = jnp.where(ids2 == b, part, att)
        att = att.astype(BF16)

        eq, ek = e_block(0, p), e_block(1, p)
        k_inter = (kp * ek).astype(BF16)
        decay, upd = [], []
        for t in range(n_chunks):
            rows = slice(t * c, (t + 1) * c)
            e_q = e_all[0:c, t * D_GLA_QK + p * LANES:t * D_GLA_QK + (p + 1) * LANES]
            decay.append(jnp.where(fwd, e_q[c - 1:c], e_q[0:1]))
            u = lax.dot_general(vb[rows, 2 * p * GLA_DV:(2 * p + 2) * GLA_DV], k_inter[rows], tn,
                                preferred_element_type=F32)
            upd.append(jnp.where(lane_v < GLA_DK, u[:GLA_DV], u[GLA_DV:]))
        st = st_ref[p]
        before = []
        for t in range(n_chunks):
            r = n_chunks - 1 - t
            before.append(st)
            st = jnp.where(fwd, decay[t], decay[r]) * st + jnp.where(fwd, upd[t], upd[r])
        st_ref[p] = st

        for t in range(n_chunks):
            rows = slice(t * c, (t + 1) * c)
            s_t = jnp.where(fwd, before[t], before[n_chunks - 1 - t]).astype(BF16)
            lhs = jnp.concatenate([qm[0][rows] * eq[rows], qm[1][rows] * eq[rows]], axis=0).astype(BF16)
            o_inter = lax.dot_general(lhs, s_t, nt, preferred_element_type=F32)
            for hh in range(2):
                h = 2 * p + hh
                o_intra = jnp.dot(att[hh * tm + t * c:hh * tm + (t + 1) * c],
                                  vb[:, h * GLA_DV:(h + 1) * GLA_DV], preferred_element_type=F32)
                o_ref[0, rows, h * GLA_DV:(h + 1) * GLA_DV] = o_inter[hh * c:(hh + 1) * c] + o_intra


def _gla(gq, gk, gv, la, *, n_lat):
    n_all = gq.shape[0]
    tm = ROW_TILE
    nl = n_lat // tm
    nb = n_all // tm
    nc = nb - nl

    def blk(d, s):
        ctx = jnp.where(d == 0, nl + s, nb - 1 - s)
        lat = jnp.where(d == 0, s - nc, nb - 1 - s)
        return jnp.where(s < nc, ctx, lat)

    row = lambda d, s: (blk(d, s), 0)
    return pl.pallas_call(
        _gla_kernel,
        grid=(2, nb),
        in_specs=[pl.BlockSpec((tm, D_GLA_QK), row),
                  pl.BlockSpec((tm, D_GLA_QK), row),
                  pl.BlockSpec((tm, D_GLA_OUT), row),
                  pl.BlockSpec((1, tm, D_GLA_QK), lambda d, s: (d, blk(d, s), 0))],
        out_specs=pl.BlockSpec((1, tm, D_GLA_OUT), lambda d, s: (d, blk(d, s), 0)),
        out_shape=jax.ShapeDtypeStruct((2, n_all, D_GLA_OUT), F32),
        scratch_shapes=[pltpu.VMEM((N_GLA_HEADS // 2, GLA_DV, 2 * GLA_DK), F32),
                        pltpu.VMEM(((2 + len(GLA_SUB_LEVELS)) * GLA_CHUNK, 3 * GLA_CHUNK), BF16),
                        pltpu.VMEM((tm, tm), jnp.int32)],
        compiler_params=_params("arbitrary", "arbitrary"),
        name="gla_scan",
    )(gq, gk, gv, la)


def _mixout_kernel(h_ref, mlat_ref, og_ref, gg_ref, gn_ref, wout_ref, mod_ref, gpost_ref, o_ref):
    o = og_ref[0] + og_ref[1]
    gate = _silu(gg_ref[...])
    parts = []
    for h in range(N_GLA_HEADS):
        sl = slice(h * GLA_DV, (h + 1) * GLA_DV)
        parts.append((_rms(o[:, sl], gn_ref[...]) * gate[:, sl]).astype(BF16))
    tn = (((0,), (0,)), ((), ()))
    y = lax.dot_general(mlat_ref[...], wout_ref[:D_MLA_OUT, :], tn, preferred_element_type=F32)
    y = y + jnp.dot(jnp.concatenate(parts, axis=-1), wout_ref[D_MLA_OUT:, :], preferred_element_type=F32)
    o_ref[...] = h_ref[...] + mod_ref[0][5:6] * _rms(y, gpost_ref[...])


def _mixout(h_all, mla, og, gg, g_norm, w_out, mod, g_post, *, n_lat):
    tm = ROW_TILE
    row = lambda i: (i, 0)
    const = lambda i: (0, 0)
    return pl.pallas_call(
        _mixout_kernel,
        grid=(n_lat // tm,),
        in_specs=[pl.BlockSpec((tm, D_MODEL), row),
                  pl.BlockSpec((D_MLA_OUT, tm), lambda i: (0, i)),
                  pl.BlockSpec((2, tm, D_GLA_OUT), lambda i: (0, i, 0)),
                  pl.BlockSpec((tm, D_GLA_OUT), row),
                  pl.BlockSpec((1, GLA_DV), const),
                  pl.BlockSpec(w_out.shape, const),
                  _mod_spec(n_lat // tm),
                  pl.BlockSpec((1, D_MODEL), const)],
        out_specs=pl.BlockSpec((tm, D_MODEL), row),
        out_shape=jax.ShapeDtypeStruct((n_lat, D_MODEL), F32),
        compiler_params=_params("arbitrary"),
        name="mixer_out",
    )(h_all, mla, og, gg, g_norm, w_out, mod, g_post)


def _rope_half_swap(w):
    half = MLA_ROPE_DIM // 4
    g = w.reshape(w.shape[:-1] + (2, 2, half))
    return g[..., ::-1, :].reshape(w.shape)


def _layout_w_in(w):
    d = w.shape[0]
    o_kv = MLA_Q_RANK
    o_kr = o_kv + MLA_KV_RANK
    o_gq = o_kr + MLA_ROPE_DIM
    o_gk = o_gq + D_GLA_QK
    o_gv = o_gk + D_GLA_QK
    o_gg = o_gv + D_GLA_OUT
    o_ga = o_gg + D_GLA_OUT
    k_rope = w[:, o_kr:o_gq]
    z = lambda n: jnp.zeros((d, n), w.dtype)
    pad = LANES - ROPE_LANE0 - MLA_ROPE_DIM
    return jnp.concatenate([
        w[:, :o_kr], w[:, o_gq:o_ga],
        w[:, o_ga:o_ga + 2 * GLA_GATE_RANK], z(ROPE_LANE0 - 2 * GLA_GATE_RANK), k_rope, z(pad),
        z(ROPE_LANE0), _rope_half_swap(k_rope), z(pad)], axis=1)


def _layout_w_q(w):
    r = w.shape[0]
    w = w.reshape(r, N_MLA_HEADS, MLA_QK_DIM)
    nope, rope = w[..., :MLA_NOPE_DIM], w[..., MLA_NOPE_DIM:]
    zp = jnp.zeros((r, N_MLA_HEADS, HEAD_PAD - MLA_QK_DIM), w.dtype)
    zn = jnp.zeros_like(nope)
    plain = jnp.concatenate([nope, rope, zp], axis=-1).reshape(r, -1)
    swapped = jnp.concatenate([zn, _rope_half_swap(rope), zp], axis=-1).reshape(r, -1)
    return jnp.concatenate([plain, swapped], axis=1)


def _layout_w_kv(w):
    r = w.shape[0]
    w = w.reshape(r, N_MLA_HEADS, MLA_NOPE_DIM + MLA_V_DIM)
    k_nope, v = w[..., :MLA_NOPE_DIM], w[..., MLA_NOPE_DIM:]
    zk = jnp.zeros((r, N_MLA_HEADS, HEAD_PAD - MLA_NOPE_DIM), w.dtype)
    zv = jnp.zeros((r, N_MLA_HEADS, HEAD_PAD - MLA_V_DIM), w.dtype)
    return jnp.concatenate([jnp.concatenate([k_nope, zk], axis=-1).reshape(r, -1),
                            jnp.concatenate([v, zv], axis=-1).reshape(r, -1)], axis=1)


def _layout_w_a(w_f, w_b):
    z = jnp.zeros((LANES, 2 * D_GLA_QK), w_f.dtype)
    z = z.at[:GLA_GATE_RANK, :D_GLA_QK].set(w_f)
    return z.at[GLA_GATE_RANK:2 * GLA_GATE_RANK, D_GLA_QK:].set(w_b)


def _attn_key_block(n_all):
    for bk in (640, 512, 256):
        if n_all % bk == 0:
            return bk
    raise ValueError(f"unsupported key count {n_all}")


def kernel(x, c, ctx, c_ctx, w_ada, b_ada, norm_pre, norm_post, ffn1_w_in, ffn1_w_out, ffn2_w_in, ffn2_w_out, w_in, mla_q_norm, mla_w_qb, mla_kv_norm, mla_w_kvb, gla_w_a_fwd, gla_b_a_fwd, gla_w_a_bwd, gla_b_a_bwd, gla_norm, w_out):
    assert x.shape[0] == 1 and ctx.shape[0] == 1 and w_ada.shape[0] == 1
    n_lat, n_ctx = x.shape[1], ctx.shape[1]
    assert n_lat % ROW_TILE == 0 and n_ctx % ROW_TILE == 0 and n_lat % GRID_W == 0
    n_all = n_lat + n_ctx
    row2 = lambda a: a.reshape(1, -1)

    x_all = jnp.concatenate([x[0], ctx[0]], axis=0)
    ct = jnp.stack([c[0], c_ctx], axis=1)
    mod = _ada(ct, w_ada[0], row2(b_ada[0])).reshape(2, N_MOD, D_MODEL)

    h_all = _ffn(x_all, mod, row2(norm_pre[0, 0]), row2(norm_post[0, 0]),
                 ffn1_w_in[0].astype(BF16), ffn1_w_out[0].astype(BF16), mod0=0, n_rows=n_all, n_lat=n_lat)

    q, k, v, gq, gk, gv, gg, la = _mixin(
        h_all, mod, row2(norm_pre[0, 1]), _layout_w_in(w_in[0]).astype(BF16),
        row2(mla_q_norm[0]), _layout_w_q(mla_w_qb[0]).astype(BF16),
        row2(mla_kv_norm[0]), _layout_w_kv(mla_w_kvb[0]).astype(BF16),
        _layout_w_a(gla_w_a_fwd[0], gla_w_a_bwd[0]),
        jnp.concatenate([gla_b_a_fwd[0], gla_b_a_bwd[0]]).reshape(1, -1), n_lat=n_lat)

    mla = _attn(q, k, v, n_lat=n_lat, bq=ATTN_Q_TILE, bk=_attn_key_block(n_all))
    og = _gla(gq, gk, gv, la, n_lat=n_lat)

    h_lat = _mixout(h_all, mla, og, gg, row2(gla_norm[0]), w_out[0].astype(BF16), mod,
                    row2(norm_post[0, 1]), n_lat=n_lat)
    out = _ffn(h_lat, mod, row2(norm_pre[0, 2]), row2(norm_post[0, 2]),
               ffn2_w_in[0].astype(BF16), ffn2_w_out[0].astype(BF16), mod0=6, n_rows=n_lat, n_lat=n_lat)
    return out[None]
```

```python
import functools
import math

import jax
import jax.numpy as jnp
from jax import lax
from jax.experimental import pallas as pl
from jax.experimental.pallas import tpu as pltpu

F32 = jnp.float32
BF16 = jnp.bfloat16

D_MODEL = 1024
GRID_W = 64
N_MLA_HEADS = 8
MLA_Q_RANK = 256
MLA_KV_RANK = 128
MLA_NOPE_DIM = 64
MLA_ROPE_DIM = 32
MLA_QK_DIM = MLA_NOPE_DIM + MLA_ROPE_DIM
MLA_V_DIM = 64
N_GLA_HEADS = 4
GLA_DK = 64
GLA_DV = 128
GLA_GATE_RANK = 16
GLA_GATE_NORM = 16.0
GLA_CHUNK = 64
D_FF = 2816
MACARON_WEIGHT = 0.5
ROPE_BASE = 10000.0
EPS = 1e-6
N_MOD = 9

D_MLA_OUT = N_MLA_HEADS * MLA_V_DIM
D_GLA_OUT = N_GLA_HEADS * GLA_DV
D_GLA_QK = N_GLA_HEADS * GLA_DK

LANES = 128
HEAD_PAD = 128
VT_ROWS = 128
ROW_TILE = 256
ATTN_Q_TILE = 512
ATTN_UNROLL = 4
VMEM_LIMIT = 56 * 1024 * 1024
C_QA = 0
C_KV = C_QA + MLA_Q_RANK
C_GQ = C_KV + MLA_KV_RANK
C_GK = C_GQ + D_GLA_QK
C_GV = C_GK + D_GLA_QK
C_GG = C_GV + D_GLA_OUT
C_MISC = C_GG + D_GLA_OUT
C_KSW = C_MISC + LANES
C_TOTAL = C_KSW + LANES
ROPE_LANE0 = MLA_NOPE_DIM
GLA_SUB_LEVELS = (32, 16, 8, 4, 2, 1)
GLA_DIAG = 64


def _rms(x, g):
    return x * lax.rsqrt(jnp.mean(x * x, axis=-1, keepdims=True) + EPS) * g


def _silu(x):
    return x * jax.nn.sigmoid(x)


def _params(*sem):
    return pltpu.CompilerParams(dimension_semantics=sem, vmem_limit_bytes=VMEM_LIMIT)


def _ada_kernel(ct_ref, w_ref, b_ref, o_ref):
    s = _silu(ct_ref[...])
    w = w_ref[...]
    r0 = jnp.sum(s[:, 0:1] * w, axis=0, keepdims=True)
    r1 = jnp.sum(s[:, 1:2] * w, axis=0, keepdims=True)
    o_ref[...] = jnp.concatenate([r0, r1], axis=0) + b_ref[...]


def _ada(ct, w, b):
    d, n = w.shape
    tn = n // 8
    return pl.pallas_call(
        _ada_kernel,
        grid=(n // tn,),
        in_specs=[pl.BlockSpec((d, 2), lambda j: (0, 0)),
                  pl.BlockSpec((d, tn), lambda j: (0, j)),
                  pl.BlockSpec((1, tn), lambda j: (0, j))],
        out_specs=pl.BlockSpec((2, tn), lambda j: (0, j)),
        out_shape=jax.ShapeDtypeStruct((2, n), F32),
        compiler_params=_params("arbitrary"),
        name="adaln",
    )(ct, w, b)


def _ffn_kernel(*refs, mod0, n_lat_blocks, with_ctx):
    if with_ctx:
        x_ref, ctx_ref, mod_ref, gpre_ref, gpost_ref, win_ref, wout_ref, o_ref = refs
        x = jnp.where(pl.program_id(0) < n_lat_blocks, x_ref[...], ctx_ref[...])
    else:
        x_ref, mod_ref, gpre_ref, gpost_ref, win_ref, wout_ref, o_ref = refs
        x = x_ref[...]
    m = mod_ref[0]
    shift, scale, gate = m[mod0:mod0 + 1], m[mod0 + 1:mod0 + 2], m[mod0 + 2:mod0 + 3]
    h = _rms(x, gpre_ref[...]) * (1.0 + scale) + shift
    ab = jnp.dot(h.astype(BF16), win_ref[...], preferred_element_type=F32)
    a, b = ab[:, :D_FF], ab[:, D_FF:]
    g = (_silu(a) * b).astype(BF16)
    y = jnp.dot(g, wout_ref[...], preferred_element_type=F32)
    o_ref[...] = x + (MACARON_WEIGHT * gate) * _rms(y, gpost_ref[...])


def _mod_spec(n_lat_blocks):
    return pl.BlockSpec((1, N_MOD, D_MODEL), lambda i: (jnp.where(i < n_lat_blocks, 0, 1), 0, 0))


def _ffn(x, ctx, mod, g_pre, g_post, w_in, w_out, *, mod0):
    tm = ROW_TILE
    nl = x.shape[0] // tm
    nc = 0 if ctx is None else ctx.shape[0] // tm
    const = lambda i: (0, 0)
    if ctx is None:
        rows, row_specs = (x,), [pl.BlockSpec((tm, D_MODEL), lambda i: (i, 0))]
    else:
        rows = (x, ctx)
        row_specs = [pl.BlockSpec((tm, D_MODEL), lambda i: (jnp.minimum(i, nl - 1), 0)),
                     pl.BlockSpec((tm, D_MODEL), lambda i: (jnp.maximum(i - nl, 0), 0))]
    return pl.pallas_call(
        functools.partial(_ffn_kernel, mod0=mod0, n_lat_blocks=nl, with_ctx=ctx is not None),
        grid=(nl + nc,),
        in_specs=row_specs + [_mod_spec(nl),
                              pl.BlockSpec((1, D_MODEL), const),
                              pl.BlockSpec((1, D_MODEL), const),
                              pl.BlockSpec((D_MODEL, 2 * D_FF), const),
                              pl.BlockSpec((D_FF, D_MODEL), const)],
        out_specs=pl.BlockSpec((tm, D_MODEL), lambda i: (i, 0)),
        out_shape=jax.ShapeDtypeStruct(((nl + nc) * tm, D_MODEL), F32),
        compiler_params=_params("arbitrary"),
        name="ffn",
    )(*rows, mod, g_pre, g_post, w_in, w_out)


def _rope_tables(i, tm, is_lat):
    n_rows = tm // GRID_W
    assert tm % GRID_W == 0 and n_rows <= 8
    lane = lax.broadcasted_iota(jnp.int32, (GRID_W, LANES), 1)
    sub = lax.broadcasted_iota(jnp.int32, (GRID_W, LANES), 0)
    e = lane - ROPE_LANE0
    in_rope = (e >= 0) & (e < MLA_ROPE_DIM)
    axis_dim = MLA_ROPE_DIM // 2
    row_axis = (e >> 4) == 0
    w = e & (axis_dim - 1)
    j = w & (axis_dim // 2 - 1)
    first_half = w < axis_dim // 2
    inv_freq = jnp.exp(j.astype(F32) * (-math.log(ROPE_BASE) * 2.0 / axis_dim))
    col_ang = sub.astype(F32) * inv_freq
    row_ang = (i * n_rows + sub[:8]).astype(F32) * inv_freq[:8]
    sign = jnp.where(first_half, -1.0, 1.0)

    def table(fn, off_value, scale):
        col_t, row_t = fn(col_ang), fn(row_ang)
        blocks = [jnp.where(row_axis, jnp.broadcast_to(row_t[r:r + 1], (GRID_W, LANES)), col_t)
                  for r in range(n_rows)]
        blocks = [jnp.where(in_rope, jnp.where(is_lat, b, off_value) * scale, 0.0) for b in blocks]
        return jnp.concatenate(blocks, axis=0)

    cos_k = table(jnp.cos, 1.0, 1.0)
    sin_s = table(jnp.sin, 0.0, sign)
    lane_t = lax.broadcasted_iota(jnp.int32, (tm, LANES), 1)
    cos_q = jnp.where(lane_t < ROPE_LANE0, 1.0, cos_k)
    return cos_q, cos_k, sin_s


def _mixin_kernel(h_ref, mod_ref, gpre_ref, win_ref, qn_ref, wq_ref, kvn_ref, wkv_ref, wa_ref, ba_ref,
                  q_ref, k_ref, v_ref, gq_ref, gk_ref, gv_ref, gg_ref, la_ref, *, n_lat_blocks):
    i = pl.program_id(0)
    tm = h_ref.shape[0]
    m = mod_ref[0]
    shift, scale = m[3:4], m[4:5]
    a = _rms(h_ref[...], gpre_ref[...]) * (1.0 + scale) + shift
    z = jnp.dot(a.astype(BF16), win_ref[...], preferred_element_type=F32)

    cos_q, cos_k, sin_s = _rope_tables(i, tm, i < n_lat_blocks)

    qn = _rms(z[:, C_QA:C_QA + MLA_Q_RANK], qn_ref[...]).astype(BF16)
    qq = jnp.dot(qn, wq_ref[...], preferred_element_type=F32)
    hw = N_MLA_HEADS * HEAD_PAD
    sm_scale = MLA_QK_DIM ** -0.5 * math.log2(math.e)
    for h in range(N_MLA_HEADS):
        lo = h * HEAD_PAD
        qh = qq[:, lo:lo + HEAD_PAD] * cos_q + qq[:, hw + lo:hw + lo + HEAD_PAD] * sin_s
        q_ref[h] = (qh * sm_scale).astype(BF16)

    kvn = _rms(z[:, C_KV:C_KV + MLA_KV_RANK], kvn_ref[...]).astype(BF16)
    kv = jnp.dot(kvn, wkv_ref[...], preferred_element_type=F32)
    misc = z[:, C_MISC:C_MISC + LANES]
    k_rope = misc * cos_k + z[:, C_KSW:C_KSW + LANES] * sin_s
    lane = lax.broadcasted_iota(jnp.int32, (tm, HEAD_PAD), 1)
    ones_col = (lane == MLA_V_DIM).astype(F32)
    for h in range(N_MLA_HEADS):
        lo = h * HEAD_PAD
        k_ref[h] = (kv[:, lo:lo + HEAD_PAD] + k_rope).astype(BF16)
        v_ref[h] = (kv[:, hw + lo:hw + lo + HEAD_PAD] + ones_col).T[:VT_ROWS].astype(BF16)

    gq_ref[...] = z[:, C_GQ:C_GQ + D_GLA_QK] * (GLA_DK ** -0.5)
    gk_ref[...] = z[:, C_GK:C_GK + D_GLA_QK]
    gv_ref[...] = z[:, C_GV:C_GV + D_GLA_OUT]
    gg_ref[...] = z[:, C_GG:C_GG + D_GLA_OUT]
    xg = jnp.dot(misc, wa_ref[...], preferred_element_type=F32,
                 precision=lax.Precision.HIGHEST) + ba_ref[...]
    la = (jnp.minimum(xg, 0.0) - jnp.log1p(jnp.exp(-jnp.abs(xg)))) * (1.0 / GLA_GATE_NORM)
    la_ref[0] = la[:, :D_GLA_QK]
    la_ref[1] = la[:, D_GLA_QK:]


def _mixin(h_all, mod, g_pre, w_in, q_norm, w_q, kv_norm, w_kv, w_a, b_a, *, n_lat):
    n_all = h_all.shape[0]
    tm = ROW_TILE
    row = lambda i: (i, 0)
    const = lambda i: (0, 0)
    hrow = lambda i: (0, i, 0)
    full = lambda arr: pl.BlockSpec(arr.shape, const)
    out_shape = (
        jax.ShapeDtypeStruct((N_MLA_HEADS, n_all, HEAD_PAD), BF16),
        jax.ShapeDtypeStruct((N_MLA_HEADS, n_all, HEAD_PAD), BF16),
        jax.ShapeDtypeStruct((N_MLA_HEADS, VT_ROWS, n_all), BF16),
        jax.ShapeDtypeStruct((n_all, D_GLA_QK), F32),
        jax.ShapeDtypeStruct((n_all, D_GLA_QK), F32),
        jax.ShapeDtypeStruct((n_all, D_GLA_OUT), F32),
        jax.ShapeDtypeStruct((n_all, D_GLA_OUT), F32),
        jax.ShapeDtypeStruct((2, n_all, D_GLA_QK), F32),
    )
    out_specs = (
        pl.BlockSpec((N_MLA_HEADS, tm, HEAD_PAD), hrow),
        pl.BlockSpec((N_MLA_HEADS, tm, HEAD_PAD), hrow),
        pl.BlockSpec((N_MLA_HEADS, VT_ROWS, tm), lambda i: (0, 0, i)),
        pl.BlockSpec((tm, D_GLA_QK), row),
        pl.BlockSpec((tm, D_GLA_QK), row),
        pl.BlockSpec((tm, D_GLA_OUT), row),
        pl.BlockSpec((tm, D_GLA_OUT), row),
        pl.BlockSpec((2, tm, D_GLA_QK), hrow),
    )
    return pl.pallas_call(
        functools.partial(_mixin_kernel, n_lat_blocks=n_lat // tm),
        grid=(n_all // tm,),
        in_specs=[pl.BlockSpec((tm, D_MODEL), row), _mod_spec(n_lat // tm),
                  full(g_pre), full(w_in), full(q_norm), full(w_q), full(kv_norm), full(w_kv),
                  full(w_a), full(b_a)],
        out_specs=out_specs,
        out_shape=out_shape,
        compiler_params=_params("arbitrary"),
        name="mixer_in",
    )(h_all, mod, g_pre, w_in, q_norm, w_q, kv_norm, w_kv, w_a, b_a)


def _attn_kernel(q_ref, k_ref, vt_ref, o_ref, sa_ref, sb_ref, *, bk):
    n_keys = k_ref.shape[1]
    bq = q_ref.shape[1]
    nt = (((1,), (1,)), ((), ()))
    qs = [q_ref[h] for h in range(2)]

    n_trips = n_keys // bk

    def scores(h, c):
        off = pl.multiple_of(c * bk, bk)
        return lax.dot_general(k_ref[h, pl.ds(off, bk), :], qs[h], nt,
                               preferred_element_type=F32)

    def fill(s_ref, c):
        mx = []
        for h in range(2):
            st = scores(h, c)
            s_ref[h] = st
            mx.append(jnp.max(st, axis=0, keepdims=True))
        return tuple(mx)

    def update(carry, s_ref, mx, c):
        off = pl.multiple_of(c * bk, bk)
        new = []
        for h in range(2):
            m, acc = carry[h]
            m_new = jnp.maximum(m, mx[h])
            pt = jnp.exp2(s_ref[h] - m_new).astype(BF16)
            acc = jnp.exp2(m - m_new) * acc + jnp.dot(vt_ref[h, :, pl.ds(off, bk)], pt,
                                                      preferred_element_type=F32)
            new.append((m_new, acc))
        return tuple(new)

    bufs = (sa_ref, sb_ref)

    def run(carry, mx, first, count, fill_after_last):
        for u in range(count):
            nxt = None
            if u < count - 1 or fill_after_last:
                nxt = fill(bufs[(u + 1) % 2], first + u + 1)
            carry = update(carry, bufs[u % 2], mx, first + u)
            mx = nxt
        return carry, mx

    def group(i, state):
        return run(*state, ATTN_UNROLL * i, ATTN_UNROLL, True)

    carry = tuple((jnp.full((1, bq), -jnp.inf, F32), jnp.zeros((VT_ROWS, bq), F32)) for _ in range(2))
    n_groups = (n_trips - 1) // ATTN_UNROLL
    state = lax.fori_loop(0, n_groups, group, (carry, fill(sa_ref, 0)))
    carry, _ = run(*state, ATTN_UNROLL * n_groups, n_trips - ATTN_UNROLL * n_groups, False)
    outs = [acc[:MLA_V_DIM] / acc[MLA_V_DIM:MLA_V_DIM + 1] for _, acc in carry]
    o_ref[...] = jnp.concatenate(outs, axis=0).astype(o_ref.dtype)


def _attn(q, k, vt, *, n_lat, bq, bk):
    n_all = k.shape[1]
    return pl.pallas_call(
        functools.partial(_attn_kernel, bk=bk),
        grid=(N_MLA_HEADS // 2, n_lat // bq),
        in_specs=[pl.BlockSpec((2, bq, HEAD_PAD), lambda hp, i: (hp, i, 0)),
                  pl.BlockSpec((2, n_all, HEAD_PAD), lambda hp, i: (hp, 0, 0)),
                  pl.BlockSpec((2, VT_ROWS, n_all), lambda hp, i: (hp, 0, 0))],
        out_specs=pl.BlockSpec((2 * MLA_V_DIM, bq), lambda hp, i: (hp, i)),
        out_shape=jax.ShapeDtypeStruct((D_MLA_OUT, n_lat), BF16),
        scratch_shapes=[pltpu.VMEM((2, bk, bq), F32), pltpu.VMEM((2, bk, bq), F32)],
        compiler_params=_params("arbitrary", "arbitrary"),
        name="mla_attention",
    )(q, k, vt)


def _gla_level_matrix(fwd):
    c = GLA_CHUNK
    ri = lax.broadcasted_iota(jnp.int32, (c, c), 0)
    ci = lax.broadcasted_iota(jnp.int32, (c, c), 1)

    def sides(b):
        bs = ri & (~(b - 1))
        be = bs + (b - 1)
        q_f = ((ci >= bs) & (ci <= ri)).astype(F32)
        k_f = ((ci > ri) & (ci <= be)).astype(F32)
        q_b = ((ci >= ri) & (ci <= be)).astype(F32)
        k_b = ((ci < ri) & (ci >= bs)).astype(F32)
        return q_f, k_f, q_b, k_b

    q_f, k_f, q_b, k_b = sides(c)
    blocks = [jnp.where(fwd, q_f, q_b), jnp.where(fwd, k_f, k_b)]
    for b in GLA_SUB_LEVELS:
        q_f, k_f, q_b, k_b = sides(b)
        bit = (ri & b) != 0
        blocks.append(jnp.where(fwd, jnp.where(bit, q_f, k_f), jnp.where(bit, k_b, q_b)))
    return jnp.concatenate(blocks, axis=0)


def _gla_level_ids(fwd, n):
    ri = lax.broadcasted_iota(jnp.int32, (n, n), 0)
    ci = lax.broadcasted_iota(jnp.int32, (n, n), 1)
    xr = ri ^ ci
    later = jnp.where(fwd, ri, ci)
    ids = jnp.where(xr == 0, GLA_DIAG, 0)
    for b in GLA_SUB_LEVELS:
        ids = jnp.where(((xr & (-b)) == b) & ((later & b) != 0), b, ids)
    return ids


def _gla_kernel(q_ref, k_ref, v_ref, la_ref, o_ref, st_ref, w3_ref, ids_ref):
    d = pl.program_id(0)
    s = pl.program_id(1)
    fwd = d == 0
    c = GLA_CHUNK
    tm = q_ref.shape[0]
    n_chunks = tm // c
    n_pairs = N_GLA_HEADS // 2
    nt = (((1,), (1,)), ((), ()))
    tn = (((0,), (0,)), ((), ()))

    @pl.when(s == 0)
    def _():
        st_ref[...] = jnp.zeros_like(st_ref)
        w = _gla_level_matrix(fwd).astype(BF16)
        w3_ref[...] = jnp.concatenate([w, w, w], axis=1)
        ids_ref[...] = _gla_level_ids(fwd, tm)

    g = la_ref[0]
    g = jnp.concatenate([g[t * c:(t + 1) * c] for t in range(n_chunks)], axis=1)
    g_hi = g.astype(BF16)
    r1 = g - g_hi.astype(F32)
    g_mid = r1.astype(BF16)
    g_lo = (r1 - g_mid.astype(F32)).astype(BF16)
    e_all = jnp.exp(jnp.dot(w3_ref[...], jnp.concatenate([g_hi, g_mid, g_lo], axis=0),
                            preferred_element_type=F32))

    def e_block(blk, p):
        return jnp.concatenate(
            [e_all[blk * c:(blk + 1) * c, t * D_GLA_QK + p * LANES:t * D_GLA_QK + (p + 1) * LANES]
             for t in range(n_chunks)], axis=0)

    lane = lax.broadcasted_iota(jnp.int32, (tm, LANES), 1)
    lane_v = lax.broadcasted_iota(jnp.int32, (GLA_DV, LANES), 1)
    ids = ids_ref[...]
    ids2 = jnp.concatenate([ids, ids], axis=0)
    vb = v_ref[...].astype(BF16)

    for p in range(n_pairs):
        ls = slice(p * LANES, (p + 1) * LANES)
        qp, kp = q_ref[:, ls], k_ref[:, ls]
        qm = (jnp.where(lane < GLA_DK, qp, 0.0), jnp.where(lane >= GLA_DK, qp, 0.0))

        att = jnp.where(ids2 == GLA_DIAG,
                        lax.dot_general(jnp.concatenate(qm, axis=0).astype(BF16), kp.astype(BF16), nt,
                                        preferred_element_type=F32), 0.0)
        for li, b in enumerate(GLA_SUB_LEVELS):
            eb = e_block(2 + li, p)
            lhs = jnp.concatenate([qm[0] * eb, qm[1] * eb], axis=0).astype(BF16)
            part = lax.dot_general(lhs, (kp * eb).astype(BF16), nt, preferred_element_type=F32)
            att = jnp.where(ids2 == b, part, att)
        att = att.astype(BF16)

        eq, ek = e_block(0, p), e_block(1, p)
        k_inter = (kp * ek).astype(BF16)
        decay, upd = [], []
        for t in range(n_chunks):
            rows = slice(t * c, (t + 1) * c)
            e_q = e_all[0:c, t * D_GLA_QK + p * LANES:t * D_GLA_QK + (p + 1) * LANES]
            decay.append(jnp.where(fwd, e_q[c - 1:c], e_q[0:1]))
            u = lax.dot_general(vb[rows, 2 * p * GLA_DV:(2 * p + 2) * GLA_DV], k_inter[rows], tn,
                                preferred_element_type=F32)
            upd.append(jnp.where(lane_v < GLA_DK, u[:GLA_DV], u[GLA_DV:]))
        st = st_ref[p]
        before = []
        for t in range(n_chunks):
            r = n_chunks - 1 - t
            before.append(st)
            st = jnp.where(fwd, decay[t], decay[r]) * st + jnp.where(fwd, upd[t], upd[r])
        st_ref[p] = st

        for t in range(n_chunks):
            rows = slice(t * c, (t + 1) * c)
            s_t = jnp.where(fwd, before[t], before[n_chunks - 1 - t]).astype(BF16)
            lhs = jnp.concatenate([qm[0][rows] * eq[rows], qm[1][rows] * eq[rows]], axis=0).astype(BF16)
            o_inter = lax.dot_general(lhs, s_t, nt, preferred_element_type=F32)
            for hh in range(2):
                h = 2 * p + hh
                o_intra = jnp.dot(att[hh * tm + t * c:hh * tm + (t + 1) * c],
                                  vb[:, h * GLA_DV:(h + 1) * GLA_DV], preferred_element_type=F32)
                o_ref[0, rows, h * GLA_DV:(h + 1) * GLA_DV] = o_inter[hh * c:(hh + 1) * c] + o_intra


def _gla(gq, gk, gv, la, *, n_lat):
    n_all = gq.shape[0]
    tm = ROW_TILE
    nl = n_lat // tm
    nb = n_all // tm
    nc = nb - nl

    def blk(d, s):
        ctx = jnp.where(d == 0, nl + s, nb - 1 - s)
        lat = jnp.where(d == 0, s - nc, nb - 1 - s)
        return jnp.where(s < nc, ctx, lat)

    row = lambda d, s: (blk(d, s), 0)
    return pl.pallas_call(
        _gla_kernel,
        grid=(2, nb),
        in_specs=[pl.BlockSpec((tm, D_GLA_QK), row),
                  pl.BlockSpec((tm, D_GLA_QK), row),
                  pl.BlockSpec((tm, D_GLA_OUT), row),
                  pl.BlockSpec((1, tm, D_GLA_QK), lambda d, s: (d, blk(d, s), 0))],
        out_specs=pl.BlockSpec((1, tm, D_GLA_OUT), lambda d, s: (d, blk(d, s), 0)),
        out_shape=jax.ShapeDtypeStruct((2, n_all, D_GLA_OUT), F32),
        scratch_shapes=[pltpu.VMEM((N_GLA_HEADS // 2, GLA_DV, 2 * GLA_DK), F32),
                        pltpu.VMEM(((2 + len(GLA_SUB_LEVELS)) * GLA_CHUNK, 3 * GLA_CHUNK), BF16),
                        pltpu.VMEM((tm, tm), jnp.int32)],
        compiler_params=_params("arbitrary", "arbitrary"),
        name="gla_scan",
    )(gq, gk, gv, la)


def _mixout_kernel(h_ref, mlat_ref, og_ref, gg_ref, gn_ref, wout_ref, mod_ref, gpost_ref, o_ref):
    o = og_ref[0] + og_ref[1]
    gate = _silu(gg_ref[...])
    parts = []
    for h in range(N_GLA_HEADS):
        sl = slice(h * GLA_DV, (h + 1) * GLA_DV)
        parts.append((_rms(o[:, sl], gn_ref[...]) * gate[:, sl]).astype(BF16))
    tn = (((0,), (0,)), ((), ()))
    y = lax.dot_general(mlat_ref[...], wout_ref[:D_MLA_OUT, :], tn, preferred_element_type=F32)
    y = y + jnp.dot(jnp.concatenate(parts, axis=-1), wout_ref[D_MLA_OUT:, :], preferred_element_type=F32)
    o_ref[...] = h_ref[...] + mod_ref[0][5:6] * _rms(y, gpost_ref[...])


def _mixout(h_all, mla, og, gg, g_norm, w_out, mod, g_post, *, n_lat):
    tm = ROW_TILE
    row = lambda i: (i, 0)
    const = lambda i: (0, 0)
    return pl.pallas_call(
        _mixout_kernel,
        grid=(n_lat // tm,),
        in_specs=[pl.BlockSpec((tm, D_MODEL), row),
                  pl.BlockSpec((D_MLA_OUT, tm), lambda i: (0, i)),
                  pl.BlockSpec((2, tm, D_GLA_OUT), lambda i: (0, i, 0)),
                  pl.BlockSpec((tm, D_GLA_OUT), row),
                  pl.BlockSpec((1, GLA_DV), const),
                  pl.BlockSpec(w_out.shape, const),
                  _mod_spec(n_lat // tm),
                  pl.BlockSpec((1, D_MODEL), const)],
        out_specs=pl.BlockSpec((tm, D_MODEL), row),
        out_shape=jax.ShapeDtypeStruct((n_lat, D_MODEL), F32),
        compiler_params=_params("arbitrary"),
        name="mixer_out",
    )(h_all, mla, og, gg, g_norm, w_out, mod, g_post)


def _rope_half_swap(w):
    half = MLA_ROPE_DIM // 4
    g = w.reshape(w.shape[:-1] + (2, 2, half))
    return g[..., ::-1, :].reshape(w.shape)


def _layout_w_in(w):
    d = w.shape[0]
    o_kv = MLA_Q_RANK
    o_kr = o_kv + MLA_KV_RANK
    o_gq = o_kr + MLA_ROPE_DIM
    o_gk = o_gq + D_GLA_QK
    o_gv = o_gk + D_GLA_QK
    o_gg = o_gv + D_GLA_OUT
    o_ga = o_gg + D_GLA_OUT
    k_rope = w[:, o_kr:o_gq]
    z = lambda n: jnp.zeros((d, n), w.dtype)
    pad = LANES - ROPE_LANE0 - MLA_ROPE_DIM
    return jnp.concatenate([
        w[:, :o_kr], w[:, o_gq:o_ga],
        w[:, o_ga:o_ga + 2 * GLA_GATE_RANK], z(ROPE_LANE0 - 2 * GLA_GATE_RANK), k_rope, z(pad),
        z(ROPE_LANE0), _rope_half_swap(k_rope), z(pad)], axis=1)


def _layout_w_q(w):
    r = w.shape[0]
    w = w.reshape(r, N_MLA_HEADS, MLA_QK_DIM)
    nope, rope = w[..., :MLA_NOPE_DIM], w[..., MLA_NOPE_DIM:]
    zp = jnp.zeros((r, N_MLA_HEADS, HEAD_PAD - MLA_QK_DIM), w.dtype)
    zn = jnp.zeros_like(nope)
    plain = jnp.concatenate([nope, rope, zp], axis=-1).reshape(r, -1)
    swapped = jnp.concatenate([zn, _rope_half_swap(rope), zp], axis=-1).reshape(r, -1)
    return jnp.concatenate([plain, swapped], axis=1)


def _layout_w_kv(w):
    r = w.shape[0]
    w = w.reshape(r, N_MLA_HEADS, MLA_NOPE_DIM + MLA_V_DIM)
    k_nope, v = w[..., :MLA_NOPE_DIM], w[..., MLA_NOPE_DIM:]
    zk = jnp.zeros((r, N_MLA_HEADS, HEAD_PAD - MLA_NOPE_DIM), w.dtype)
    zv = jnp.zeros((r, N_MLA_HEADS, HEAD_PAD - MLA_V_DIM), w.dtype)
    return jnp.concatenate([jnp.concatenate([k_nope, zk], axis=-1).reshape(r, -1),
                            jnp.concatenate([v, zv], axis=-1).reshape(r, -1)], axis=1)


def _layout_w_a(w_f, w_b):
    z = jnp.zeros((LANES, 2 * D_GLA_QK), w_f.dtype)
    z = z.at[:GLA_GATE_RANK, :D_GLA_QK].set(w_f)
    return z.at[GLA_GATE_RANK:2 * GLA_GATE_RANK, D_GLA_QK:].set(w_b)


def _attn_key_block(n_all):
    for bk in (640, 512, 256):
        if n_all % bk == 0:
            return bk
    raise ValueError(f"unsupported key count {n_all}")


def kernel(x, c, ctx, c_ctx, w_ada, b_ada, norm_pre, norm_post, ffn1_w_in, ffn1_w_out, ffn2_w_in, ffn2_w_out, w_in, mla_q_norm, mla_w_qb, mla_kv_norm, mla_w_kvb, gla_w_a_fwd, gla_b_a_fwd, gla_w_a_bwd, gla_b_a_bwd, gla_norm, w_out):
    assert x.shape[0] == 1 and ctx.shape[0] == 1 and w_ada.shape[0] == 1
    n_lat, n_ctx = x.shape[1], ctx.shape[1]
    assert n_lat % ROW_TILE == 0 and n_ctx % ROW_TILE == 0 and n_lat % GRID_W == 0
    n_all = n_lat + n_ctx
    row2 = lambda a: a.reshape(1, -1)

    ct = jnp.stack([c[0], c_ctx], axis=1)
    mod = _ada(ct, w_ada[0], row2(b_ada[0])).reshape(2, N_MOD, D_MODEL)

    h_all = _ffn(x[0], ctx[0], mod, row2(norm_pre[0, 0]), row2(norm_post[0, 0]),
                 ffn1_w_in[0].astype(BF16), ffn1_w_out[0].astype(BF16), mod0=0)

    q, k, v, gq, gk, gv, gg, la = _mixin(
        h_all, mod, row2(norm_pre[0, 1]), _layout_w_in(w_in[0]).astype(BF16),
        row2(mla_q_norm[0]), _layout_w_q(mla_w_qb[0]).astype(BF16),
        row2(mla_kv_norm[0]), _layout_w_kv(mla_w_kvb[0]).astype(BF16),
        _layout_w_a(gla_w_a_fwd[0], gla_w_a_bwd[0]),
        jnp.concatenate([gla_b_a_fwd[0], gla_b_a_bwd[0]]).reshape(1, -1), n_lat=n_lat)

    mla = _attn(q, k, v, n_lat=n_lat, bq=ATTN_Q_TILE, bk=_attn_key_block(n_all))
    og = _gla(gq, gk, gv, la, n_lat=n_lat)

    h_lat = _mixout(h_all, mla, og, gg, row2(gla_norm[0]), w_out[0].astype(BF16), mod,
                    row2(norm_post[0, 1]), n_lat=n_lat)
    out = _ffn(h_lat, None, mod, row2(norm_pre[0, 2]), row2(norm_post[0, 2]),
               ffn2_w_in[0].astype(BF16), ffn2_w_out[0].astype(BF16), mod0=6)
    return out[None]
```

```python
import functools
import math

import jax
import jax.numpy as jnp
from jax import lax
from jax.experimental import pallas as pl
from jax.experimental.pallas import tpu as pltpu

F32 = jnp.float32
BF16 = jnp.bfloat16

D_MODEL = 1024
GRID_W = 64
N_MLA_HEADS = 8
MLA_Q_RANK = 256
MLA_KV_RANK = 128
MLA_NOPE_DIM = 64
MLA_ROPE_DIM = 32
MLA_QK_DIM = MLA_NOPE_DIM + MLA_ROPE_DIM
MLA_V_DIM = 64
N_GLA_HEADS = 4
GLA_DK = 64
GLA_DV = 128
GLA_GATE_RANK = 16
GLA_GATE_NORM = 16.0
GLA_CHUNK = 64
D_FF = 2816
MACARON_WEIGHT = 0.5
ROPE_BASE = 10000.0
EPS = 1e-6
N_MOD = 9

D_MLA_OUT = N_MLA_HEADS * MLA_V_DIM
D_GLA_OUT = N_GLA_HEADS * GLA_DV
D_GLA_QK = N_GLA_HEADS * GLA_DK

LANES = 128
HEAD_PAD = 128
VT_ROWS = 128
ROW_TILE = 256
FUSED_TILE = 512
ATTN_Q_TILE = 512
ATTN_UNROLL = 4
VMEM_LIMIT = 56 * 1024 * 1024
C_QA = 0
C_KV = C_QA + MLA_Q_RANK
C_GQ = C_KV + MLA_KV_RANK
C_GK = C_GQ + D_GLA_QK
C_GV = C_GK + D_GLA_QK
C_GG = C_GV + D_GLA_OUT
C_MISC = C_GG + D_GLA_OUT
C_KSW = C_MISC + LANES
C_TOTAL = C_KSW + LANES
ROPE_LANE0 = MLA_NOPE_DIM
GLA_SUB_LEVELS = (32, 16, 8, 4, 2, 1)
GLA_DIAG = 64


def _rms(x, g):
    return x * lax.rsqrt(jnp.mean(x * x, axis=-1, keepdims=True) + EPS) * g


def _silu(x):
    return x * jax.nn.sigmoid(x)


def _params(*sem):
    return pltpu.CompilerParams(dimension_semantics=sem, vmem_limit_bytes=VMEM_LIMIT)


def _ada_kernel(ct_ref, w_ref, b_ref, o_ref):
    s = _silu(ct_ref[...])
    w = w_ref[...]
    r0 = jnp.sum(s[:, 0:1] * w, axis=0, keepdims=True)
    r1 = jnp.sum(s[:, 1:2] * w, axis=0, keepdims=True)
    o_ref[...] = jnp.concatenate([r0, r1], axis=0) + b_ref[...]


def _ada(ct, w, b):
    d, n = w.shape
    tn = n // 8
    return pl.pallas_call(
        _ada_kernel,
        grid=(n // tn,),
        in_specs=[pl.BlockSpec((d, 2), lambda j: (0, 0)),
                  pl.BlockSpec((d, tn), lambda j: (0, j)),
                  pl.BlockSpec((1, tn), lambda j: (0, j))],
        out_specs=pl.BlockSpec((2, tn), lambda j: (0, j)),
        out_shape=jax.ShapeDtypeStruct((2, n), F32),
        compiler_params=_params("arbitrary"),
        name="adaln",
    )(ct, w, b)


def _ffn_kernel(x_ref, ctx_ref, mod_ref, gpre_ref, gpost_ref, win_ref, wout_ref, o_ref, *, n_lat_blocks):
    x = jnp.where(pl.program_id(0) < n_lat_blocks, x_ref[...], ctx_ref[...])
    o_ref[...] = _ffn_tile(x, mod_ref[0], 0, gpre_ref[...], gpost_ref[...], win_ref, wout_ref)


def _ffn_tile(x, m, mod0, g_pre, g_post, win_ref, wout_ref):
    shift, scale, gate = m[mod0:mod0 + 1], m[mod0 + 1:mod0 + 2], m[mod0 + 2:mod0 + 3]
    h = _rms(x, g_pre) * (1.0 + scale) + shift
    ab = jnp.dot(h.astype(BF16), win_ref[...], preferred_element_type=F32)
    a, b = ab[:, :D_FF], ab[:, D_FF:]
    g = (_silu(a) * b).astype(BF16)
    y = jnp.dot(g, wout_ref[...], preferred_element_type=F32)
    return x + (MACARON_WEIGHT * gate) * _rms(y, g_post)


def _mod_spec(n_lat_blocks):
    return pl.BlockSpec((1, N_MOD, D_MODEL), lambda i: (jnp.where(i < n_lat_blocks, 0, 1), 0, 0))


def _ffn1(x, ctx, mod, g_pre, g_post, w_in, w_out):
    tm = ROW_TILE
    nl, nc = x.shape[0] // tm, ctx.shape[0] // tm
    const = lambda i: (0, 0)
    return pl.pallas_call(
        functools.partial(_ffn_kernel, n_lat_blocks=nl),
        grid=(nl + nc,),
        in_specs=[pl.BlockSpec((tm, D_MODEL), lambda i: (jnp.minimum(i, nl - 1), 0)),
                  pl.BlockSpec((tm, D_MODEL), lambda i: (jnp.maximum(i - nl, 0), 0)),
                  _mod_spec(nl),
                  pl.BlockSpec((1, D_MODEL), const),
                  pl.BlockSpec((1, D_MODEL), const),
                  pl.BlockSpec((D_MODEL, 2 * D_FF), const),
                  pl.BlockSpec((D_FF, D_MODEL), const)],
        out_specs=pl.BlockSpec((tm, D_MODEL), lambda i: (i, 0)),
        out_shape=jax.ShapeDtypeStruct(((nl + nc) * tm, D_MODEL), F32),
        compiler_params=_params("arbitrary"),
        name="ffn",
    )(x, ctx, mod, g_pre, g_post, w_in, w_out)


def _rope_tables(i, tm, is_lat):
    n_rows = tm // GRID_W
    assert tm % GRID_W == 0 and n_rows <= 8
    lane = lax.broadcasted_iota(jnp.int32, (GRID_W, LANES), 1)
    sub = lax.broadcasted_iota(jnp.int32, (GRID_W, LANES), 0)
    e = lane - ROPE_LANE0
    in_rope = (e >= 0) & (e < MLA_ROPE_DIM)
    axis_dim = MLA_ROPE_DIM // 2
    row_axis = (e >> 4) == 0
    w = e & (axis_dim - 1)
    j = w & (axis_dim // 2 - 1)
    first_half = w < axis_dim // 2
    inv_freq = jnp.exp(j.astype(F32) * (-math.log(ROPE_BASE) * 2.0 / axis_dim))
    col_ang = sub.astype(F32) * inv_freq
    row_ang = (i * n_rows + sub[:8]).astype(F32) * inv_freq[:8]
    sign = jnp.where(first_half, -1.0, 1.0)

    def table(fn, off_value, scale):
        col_t, row_t = fn(col_ang), fn(row_ang)
        blocks = [jnp.where(row_axis, jnp.broadcast_to(row_t[r:r + 1], (GRID_W, LANES)), col_t)
                  for r in range(n_rows)]
        blocks = [jnp.where(in_rope, jnp.where(is_lat, b, off_value) * scale, 0.0) for b in blocks]
        return jnp.concatenate(blocks, axis=0)

    cos_k = table(jnp.cos, 1.0, 1.0)
    sin_s = table(jnp.sin, 0.0, sign)
    lane_t = lax.broadcasted_iota(jnp.int32, (tm, LANES), 1)
    cos_q = jnp.where(lane_t < ROPE_LANE0, 1.0, cos_k)
    return cos_q, cos_k, sin_s


def _mixin_kernel(h_ref, mod_ref, gpre_ref, win_ref, qn_ref, wq_ref, kvn_ref, wkv_ref, wa_ref, ba_ref,
                  q_ref, k_ref, v_ref, gq_ref, gk_ref, gv_ref, gg_ref, la_ref, *, n_lat_blocks):
    i = pl.program_id(0)
    tm = h_ref.shape[0]
    m = mod_ref[0]
    shift, scale = m[3:4], m[4:5]
    a = _rms(h_ref[...], gpre_ref[...]) * (1.0 + scale) + shift
    z = jnp.dot(a.astype(BF16), win_ref[...], preferred_element_type=F32)

    cos_q, cos_k, sin_s = _rope_tables(i, tm, i < n_lat_blocks)

    qn = _rms(z[:, C_QA:C_QA + MLA_Q_RANK], qn_ref[...]).astype(BF16)
    qq = jnp.dot(qn, wq_ref[...], preferred_element_type=F32)
    hw = N_MLA_HEADS * HEAD_PAD
    sm_scale = MLA_QK_DIM ** -0.5 * math.log2(math.e)
    for h in range(N_MLA_HEADS):
        lo = h * HEAD_PAD
        qh = qq[:, lo:lo + HEAD_PAD] * cos_q + qq[:, hw + lo:hw + lo + HEAD_PAD] * sin_s
        q_ref[h] = (qh * sm_scale).astype(BF16)

    kvn = _rms(z[:, C_KV:C_KV + MLA_KV_RANK], kvn_ref[...]).astype(BF16)
    kv = jnp.dot(kvn, wkv_ref[...], preferred_element_type=F32)
    misc = z[:, C_MISC:C_MISC + LANES]
    k_rope = misc * cos_k + z[:, C_KSW:C_KSW + LANES] * sin_s
    lane = lax.broadcasted_iota(jnp.int32, (tm, HEAD_PAD), 1)
    ones_col = (lane == MLA_V_DIM).astype(F32)
    for h in range(N_MLA_HEADS):
        lo = h * HEAD_PAD
        k_ref[h] = (kv[:, lo:lo + HEAD_PAD] + k_rope).astype(BF16)
        v_ref[h] = (kv[:, hw + lo:hw + lo + HEAD_PAD] + ones_col).T[:VT_ROWS].astype(BF16)

    gq_ref[...] = z[:, C_GQ:C_GQ + D_GLA_QK] * (GLA_DK ** -0.5)
    gk_ref[...] = z[:, C_GK:C_GK + D_GLA_QK]
    gv_ref[...] = z[:, C_GV:C_GV + D_GLA_OUT]
    gg_ref[...] = z[:, C_GG:C_GG + D_GLA_OUT]
    xg = jnp.dot(misc.astype(BF16), wa_ref[...], preferred_element_type=F32) + ba_ref[...]
    la = (jnp.minimum(xg, 0.0) - jnp.log1p(jnp.exp(-jnp.abs(xg)))) * (1.0 / GLA_GATE_NORM)
    la_ref[0] = la[:, :D_GLA_QK]
    la_ref[1] = la[:, D_GLA_QK:]


def _mixin(h_all, mod, g_pre, w_in, q_norm, w_q, kv_norm, w_kv, w_a, b_a, *, n_lat):
    n_all = h_all.shape[0]
    tm = ROW_TILE
    row = lambda i: (i, 0)
    const = lambda i: (0, 0)
    hrow = lambda i: (0, i, 0)
    full = lambda arr: pl.BlockSpec(arr.shape, const)
    out_shape = (
        jax.ShapeDtypeStruct((N_MLA_HEADS, n_all, HEAD_PAD), BF16),
        jax.ShapeDtypeStruct((N_MLA_HEADS, n_all, HEAD_PAD), BF16),
        jax.ShapeDtypeStruct((N_MLA_HEADS, VT_ROWS, n_all), BF16),
        jax.ShapeDtypeStruct((n_all, D_GLA_QK), F32),
        jax.ShapeDtypeStruct((n_all, D_GLA_QK), F32),
        jax.ShapeDtypeStruct((n_all, D_GLA_OUT), F32),
        jax.ShapeDtypeStruct((n_all, D_GLA_OUT), F32),
        jax.ShapeDtypeStruct((2, n_all, D_GLA_QK), F32),
    )
    out_specs = (
        pl.BlockSpec((N_MLA_HEADS, tm, HEAD_PAD), hrow),
        pl.BlockSpec((N_MLA_HEADS, tm, HEAD_PAD), hrow),
        pl.BlockSpec((N_MLA_HEADS, VT_ROWS, tm), lambda i: (0, 0, i)),
        pl.BlockSpec((tm, D_GLA_QK), row),
        pl.BlockSpec((tm, D_GLA_QK), row),
        pl.BlockSpec((tm, D_GLA_OUT), row),
        pl.BlockSpec((tm, D_GLA_OUT), row),
        pl.BlockSpec((2, tm, D_GLA_QK), hrow),
    )
    return pl.pallas_call(
        functools.partial(_mixin_kernel, n_lat_blocks=n_lat // tm),
        grid=(n_all // tm,),
        in_specs=[pl.BlockSpec((tm, D_MODEL), row), _mod_spec(n_lat // tm),
                  full(g_pre), full(w_in), full(q_norm), full(w_q), full(kv_norm), full(w_kv),
                  full(w_a), full(b_a)],
        out_specs=out_specs,
        out_shape=out_shape,
        compiler_params=_params("arbitrary"),
        name="mixer_in",
    )(h_all, mod, g_pre, w_in, q_norm, w_q, kv_norm, w_kv, w_a, b_a)


def _attn_kernel(q_ref, k_ref, vt_ref, o_ref, sa_ref, sb_ref, *, bk):
    n_keys = k_ref.shape[1]
    bq = q_ref.shape[1]
    nt = (((1,), (1,)), ((), ()))
    qs = [q_ref[h] for h in range(2)]

    n_trips = n_keys // bk

    def scores(h, c):
        off = pl.multiple_of(c * bk, bk)
        return lax.dot_general(k_ref[h, pl.ds(off, bk), :], qs[h], nt,
                               preferred_element_type=F32)

    def fill(s_ref, c):
        mx = []
        for h in range(2):
            st = scores(h, c)
            s_ref[h] = st
            mx.append(jnp.max(st, axis=0, keepdims=True))
        return tuple(mx)

    def update(carry, s_ref, mx, c):
        off = pl.multiple_of(c * bk, bk)
        new = []
        for h in range(2):
            m, acc = carry[h]
            m_new = jnp.maximum(m, mx[h])
            pt = jnp.exp2(s_ref[h] - m_new).astype(BF16)
            acc = jnp.exp2(m - m_new) * acc + jnp.dot(vt_ref[h, :, pl.ds(off, bk)], pt,
                                                      preferred_element_type=F32)
            new.append((m_new, acc))
        return tuple(new)

    bufs = (sa_ref, sb_ref)

    def run(carry, mx, first, count, fill_after_last):
        for u in range(count):
            nxt = None
            if u < count - 1 or fill_after_last:
                nxt = fill(bufs[(u + 1) % 2], first + u + 1)
            carry = update(carry, bufs[u % 2], mx, first + u)
            mx = nxt
        return carry, mx

    def group(i, state):
        return run(*state, ATTN_UNROLL * i, ATTN_UNROLL, True)

    carry = tuple((jnp.full((1, bq), -jnp.inf, F32), jnp.zeros((VT_ROWS, bq), F32)) for _ in range(2))
    n_groups = (n_trips - 1) // ATTN_UNROLL
    state = lax.fori_loop(0, n_groups, group, (carry, fill(sa_ref, 0)))
    carry, _ = run(*state, ATTN_UNROLL * n_groups, n_trips - ATTN_UNROLL * n_groups, False)
    outs = [acc[:MLA_V_DIM] / acc[MLA_V_DIM:MLA_V_DIM + 1] for _, acc in carry]
    o_ref[...] = jnp.concatenate(outs, axis=0).astype(o_ref.dtype)


def _attn(q, k, vt, *, n_lat, bq, bk):
    n_all = k.shape[1]
    return pl.pallas_call(
        functools.partial(_attn_kernel, bk=bk),
        grid=(N_MLA_HEADS // 2, n_lat // bq),
        in_specs=[pl.BlockSpec((2, bq, HEAD_PAD), lambda hp, i: (hp, i, 0)),
                  pl.BlockSpec((2, n_all, HEAD_PAD), lambda hp, i: (hp, 0, 0)),
                  pl.BlockSpec((2, VT_ROWS, n_all), lambda hp, i: (hp, 0, 0))],
        out_specs=pl.BlockSpec((2 * MLA_V_DIM, bq), lambda hp, i: (hp, i)),
        out_shape=jax.ShapeDtypeStruct((D_MLA_OUT, n_lat), BF16),
        scratch_shapes=[pltpu.VMEM((2, bk, bq), F32), pltpu.VMEM((2, bk, bq), F32)],
        compiler_params=_params("arbitrary", "arbitrary"),
        name="mla_attention",
    )(q, k, vt)


def _gla_level_matrix(fwd):
    c = GLA_CHUNK
    ri = lax.broadcasted_iota(jnp.int32, (c, c), 0)
    ci = lax.broadcasted_iota(jnp.int32, (c, c), 1)

    def sides(b):
        bs = ri & (~(b - 1))
        be = bs + (b - 1)
        q_f = ((ci >= bs) & (ci <= ri)).astype(F32)
        k_f = ((ci > ri) & (ci <= be)).astype(F32)
        q_b = ((ci >= ri) & (ci <= be)).astype(F32)
        k_b = ((ci < ri) & (ci >= bs)).astype(F32)
        return q_f, k_f, q_b, k_b

    q_f, k_f, q_b, k_b = sides(c)
    blocks = [jnp.where(fwd, q_f, q_b), jnp.where(fwd, k_f, k_b)]
    for b in GLA_SUB_LEVELS:
        q_f, k_f, q_b, k_b = sides(b)
        bit = (ri & b) != 0
        blocks.append(jnp.where(fwd, jnp.where(bit, q_f, k_f), jnp.where(bit, k_b, q_b)))
    return jnp.concatenate(blocks, axis=0)


def _gla_level_ids(fwd, n):
    ri = lax.broadcasted_iota(jnp.int32, (n, n), 0)
    ci = lax.broadcasted_iota(jnp.int32, (n, n), 1)
    xr = ri ^ ci
    later = jnp.where(fwd, ri, ci)
    ids = jnp.where(xr == 0, GLA_DIAG, 0)
    for b in GLA_SUB_LEVELS:
        ids = jnp.where(((xr & (-b)) == b) & ((later & b) != 0), b, ids)
    return ids


def _gla_kernel(q_ref, k_ref, v_ref, la_ref, o_ref, st_ref, w3_ref, ids_ref):
    d = pl.program_id(0)
    s = pl.program_id(1)
    fwd = d == 0
    c = GLA_CHUNK
    tm = q_ref.shape[0]
    n_chunks = tm // c
    n_pairs = N_GLA_HEADS // 2
    nt = (((1,), (1,)), ((), ()))
    tn = (((0,), (0,)), ((), ()))

    @pl.when(s == 0)
    def _():
        st_ref[...] = jnp.zeros_like(st_ref)
        w = _gla_level_matrix(fwd).astype(BF16)
        w3_ref[...] = jnp.concatenate([w, w, w], axis=1)
        ids_ref[...] = _gla_level_ids(fwd, tm)

    g = la_ref[0]
    g = jnp.concatenate([g[t * c:(t + 1) * c] for t in range(n_chunks)], axis=1)
    g_hi = g.astype(BF16)
    r1 = g - g_hi.astype(F32)
    g_mid = r1.astype(BF16)
    g_lo = (r1 - g_mid.astype(F32)).astype(BF16)
    e_all = jnp.exp(jnp.dot(w3_ref[...], jnp.concatenate([g_hi, g_mid, g_lo], axis=0),
                            preferred_element_type=F32))

    def e_block(blk, p):
        return jnp.concatenate(
            [e_all[blk * c:(blk + 1) * c, t * D_GLA_QK + p * LANES:t * D_GLA_QK + (p + 1) * LANES]
             for t in range(n_chunks)], axis=0)

    lane = lax.broadcasted_iota(jnp.int32, (tm, LANES), 1)
    lane_v = lax.broadcasted_iota(jnp.int32, (GLA_DV, LANES), 1)
    ids = ids_ref[...]
    ids2 = jnp.concatenate([ids, ids], axis=0)
    vb = v_ref[...].astype(BF16)

    for p in range(n_pairs):
        ls = slice(p * LANES, (p + 1) * LANES)
        qp, kp = q_ref[:, ls], k_ref[:, ls]
        qm = (jnp.where(lane < GLA_DK, qp, 0.0), jnp.where(lane >= GLA_DK, qp, 0.0))

        att = jnp.where(ids2 == GLA_DIAG,
                        lax.dot_general(jnp.concatenate(qm, axis=0).astype(BF16), kp.astype(BF16), nt,
                                        preferred_element_type=F32), 0.0)
        for li, b in enumerate(GLA_SUB_LEVELS):
            eb = e_block(2 + li, p)
            lhs = jnp.concatenate([qm[0] * eb, qm[1] * eb], axis=0).astype(BF16)
            part = lax.dot_general(lhs, (kp * eb).astype(BF16), nt, preferred_element_type=F32)
            att = jnp.where(ids2 == b, part, att)
        att = att.astype(BF16)

        eq, ek = e_block(0, p), e_block(1, p)
        k_inter = (kp * ek).astype(BF16)
        decay, upd = [], []
        for t in range(n_chunks):
            rows = slice(t * c, (t + 1) * c)
            e_q = e_all[0:c, t * D_GLA_QK + p * LANES:t * D_GLA_QK + (p + 1) * LANES]
            decay.append(jnp.where(fwd, e_q[c - 1:c], e_q[0:1]))
            u = lax.dot_general(vb[rows, 2 * p * GLA_DV:(2 * p + 2) * GLA_DV], k_inter[rows], tn,
                                preferred_element_type=F32)
            upd.append(jnp.where(lane_v < GLA_DK, u[:GLA_DV], u[GLA_DV:]))
        st = st_ref[p]
        before = []
        for t in range(n_chunks):
            r = n_chunks - 1 - t
            before.append(st)
            st = jnp.where(fwd, decay[t], decay[r]) * st + jnp.where(fwd, upd[t], upd[r])
        st_ref[p] = st

        for t in range(n_chunks):
            rows = slice(t * c, (t + 1) * c)
            s_t = jnp.where(fwd, before[t], before[n_chunks - 1 - t]).astype(BF16)
            lhs = jnp.concatenate([qm[0][rows] * eq[rows], qm[1][rows] * eq[rows]], axis=0).astype(BF16)
            o_inter = lax.dot_general(lhs, s_t, nt, preferred_element_type=F32)
            for hh in range(2):
                h = 2 * p + hh
                o_intra = jnp.dot(att[hh * tm + t * c:hh * tm + (t + 1) * c],
                                  vb[:, h * GLA_DV:(h + 1) * GLA_DV], preferred_element_type=F32)
                o_ref[0, rows, h * GLA_DV:(h + 1) * GLA_DV] = o_inter[hh * c:(hh + 1) * c] + o_intra


def _gla(gq, gk, gv, la, *, n_lat):
    n_all = gq.shape[0]
    tm = ROW_TILE
    nl = n_lat // tm
    nb = n_all // tm
    nc = nb - nl

    def blk(d, s):
        ctx = jnp.where(d == 0, nl + s, nb - 1 - s)
        lat = jnp.where(d == 0, s - nc, nb - 1 - s)
        return jnp.where(s < nc, ctx, lat)

    row = lambda d, s: (blk(d, s), 0)
    return pl.pallas_call(
        _gla_kernel,
        grid=(2, nb),
        in_specs=[pl.BlockSpec((tm, D_GLA_QK), row),
                  pl.BlockSpec((tm, D_GLA_QK), row),
                  pl.BlockSpec((tm, D_GLA_OUT), row),
                  pl.BlockSpec((1, tm, D_GLA_QK), lambda d, s: (d, blk(d, s), 0))],
        out_specs=pl.BlockSpec((1, tm, D_GLA_OUT), lambda d, s: (d, blk(d, s), 0)),
        out_shape=jax.ShapeDtypeStruct((2, n_all, D_GLA_OUT), F32),
        scratch_shapes=[pltpu.VMEM((N_GLA_HEADS // 2, GLA_DV, 2 * GLA_DK), F32),
                        pltpu.VMEM(((2 + len(GLA_SUB_LEVELS)) * GLA_CHUNK, 3 * GLA_CHUNK), BF16),
                        pltpu.VMEM((tm, tm), jnp.int32)],
        compiler_params=_params("arbitrary", "arbitrary"),
        name="gla_scan",
    )(gq, gk, gv, la)


def _mixout_ffn_kernel(h_ref, mlat_ref, og_ref, gg_ref, gn_ref, wmix_ref, mod_ref, gpost1_ref,
                       gpre2_ref, gpost2_ref, win_ref, wout_ref, o_ref):
    o = og_ref[0] + og_ref[1]
    gate = _silu(gg_ref[...])
    parts = []
    for h in range(N_GLA_HEADS):
        sl = slice(h * GLA_DV, (h + 1) * GLA_DV)
        parts.append((_rms(o[:, sl], gn_ref[...]) * gate[:, sl]).astype(BF16))
    tn = (((0,), (0,)), ((), ()))
    y = lax.dot_general(mlat_ref[...], wmix_ref[:D_MLA_OUT, :], tn, preferred_element_type=F32)
    y = y + jnp.dot(jnp.concatenate(parts, axis=-1), wmix_ref[D_MLA_OUT:, :], preferred_element_type=F32)
    m = mod_ref[0]
    x = h_ref[...] + m[5:6] * _rms(y, gpost1_ref[...])
    o_ref[...] = _ffn_tile(x, m, 6, gpre2_ref[...], gpost2_ref[...], win_ref, wout_ref)


def _mixout_ffn(h_all, mla_t, og, gg, g_norm, w_mix, mod, g_post1, g_pre2, g_post2, w_in, w_out, *, n_lat):
    tm = FUSED_TILE
    row = lambda i: (i, 0)
    const = lambda i: (0, 0)
    vec = pl.BlockSpec((1, D_MODEL), const)
    return pl.pallas_call(
        _mixout_ffn_kernel,
        grid=(n_lat // tm,),
        in_specs=[pl.BlockSpec((tm, D_MODEL), row),
                  pl.BlockSpec((D_MLA_OUT, tm), lambda i: (0, i)),
                  pl.BlockSpec((2, tm, D_GLA_OUT), lambda i: (0, i, 0)),
                  pl.BlockSpec((tm, D_GLA_OUT), row),
                  pl.BlockSpec((1, GLA_DV), const),
                  pl.BlockSpec(w_mix.shape, const),
                  pl.BlockSpec((1, N_MOD, D_MODEL), lambda i: (0, 0, 0)),
                  vec, vec, vec,
                  pl.BlockSpec((D_MODEL, 2 * D_FF), const),
                  pl.BlockSpec((D_FF, D_MODEL), const)],
        out_specs=pl.BlockSpec((tm, D_MODEL), row),
        out_shape=jax.ShapeDtypeStruct((n_lat, D_MODEL), F32),
        compiler_params=_params("arbitrary"),
        name="mixer_out_ffn",
    )(h_all, mla_t, og, gg, g_norm, w_mix, mod, g_post1, g_pre2, g_post2, w_in, w_out)


def _rope_half_swap(w):
    half = MLA_ROPE_DIM // 4
    g = w.reshape(w.shape[:-1] + (2, 2, half))
    return g[..., ::-1, :].reshape(w.shape)


def _layout_w_in(w):
    d = w.shape[0]
    o_kv = MLA_Q_RANK
    o_kr = o_kv + MLA_KV_RANK
    o_gq = o_kr + MLA_ROPE_DIM
    o_gk = o_gq + D_GLA_QK
    o_gv = o_gk + D_GLA_QK
    o_gg = o_gv + D_GLA_OUT
    o_ga = o_gg + D_GLA_OUT
    k_rope = w[:, o_kr:o_gq]
    z = lambda n: jnp.zeros((d, n), w.dtype)
    pad = LANES - ROPE_LANE0 - MLA_ROPE_DIM
    return jnp.concatenate([
        w[:, :o_kr], w[:, o_gq:o_ga],
        w[:, o_ga:o_ga + 2 * GLA_GATE_RANK], z(ROPE_LANE0 - 2 * GLA_GATE_RANK), k_rope, z(pad),
        z(ROPE_LANE0), _rope_half_swap(k_rope), z(pad)], axis=1)


def _layout_w_q(w):
    r = w.shape[0]
    w = w.reshape(r, N_MLA_HEADS, MLA_QK_DIM)
    nope, rope = w[..., :MLA_NOPE_DIM], w[..., MLA_NOPE_DIM:]
    zp = jnp.zeros((r, N_MLA_HEADS, HEAD_PAD - MLA_QK_DIM), w.dtype)
    zn = jnp.zeros_like(nope)
    plain = jnp.concatenate([nope, rope, zp], axis=-1).reshape(r, -1)
    swapped = jnp.concatenate([zn, _rope_half_swap(rope), zp], axis=-1).reshape(r, -1)
    return jnp.concatenate([plain, swapped], axis=1)


def _layout_w_kv(w):
    r = w.shape[0]
    w = w.reshape(r, N_MLA_HEADS, MLA_NOPE_DIM + MLA_V_DIM)
    k_nope, v = w[..., :MLA_NOPE_DIM], w[..., MLA_NOPE_DIM:]
    zk = jnp.zeros((r, N_MLA_HEADS, HEAD_PAD - MLA_NOPE_DIM), w.dtype)
    zv = jnp.zeros((r, N_MLA_HEADS, HEAD_PAD - MLA_V_DIM), w.dtype)
    return jnp.concatenate([jnp.concatenate([k_nope, zk], axis=-1).reshape(r, -1),
                            jnp.concatenate([v, zv], axis=-1).reshape(r, -1)], axis=1)


def _layout_w_a(w_f, w_b):
    z = jnp.zeros((LANES, 2 * D_GLA_QK), w_f.dtype)
    z = z.at[:GLA_GATE_RANK, :D_GLA_QK].set(w_f)
    return z.at[GLA_GATE_RANK:2 * GLA_GATE_RANK, D_GLA_QK:].set(w_b)


def _attn_key_block(n_all):
    for bk in (640, 512, 256):
        if n_all % bk == 0:
            return bk
    raise ValueError(f"unsupported key count {n_all}")


def kernel(x, c, ctx, c_ctx, w_ada, b_ada, norm_pre, norm_post, ffn1_w_in, ffn1_w_out, ffn2_w_in, ffn2_w_out, w_in, mla_q_norm, mla_w_qb, mla_kv_norm, mla_w_kvb, gla_w_a_fwd, gla_b_a_fwd, gla_w_a_bwd, gla_b_a_bwd, gla_norm, w_out):
    assert x.shape[0] == 1 and ctx.shape[0] == 1 and w_ada.shape[0] == 1
    n_lat, n_ctx = x.shape[1], ctx.shape[1]
    assert n_lat % ROW_TILE == 0 and n_ctx % ROW_TILE == 0 and n_lat % GRID_W == 0
    n_all = n_lat + n_ctx
    row2 = lambda a: a.reshape(1, -1)

    ct = jnp.stack([c[0], c_ctx], axis=1)
    mod = _ada(ct, w_ada[0], row2(b_ada[0])).reshape(2, N_MOD, D_MODEL)

    h_all = _ffn1(x[0], ctx[0], mod, row2(norm_pre[0, 0]), row2(norm_post[0, 0]),
                  ffn1_w_in[0].astype(BF16), ffn1_w_out[0].astype(BF16))

    q, k, v, gq, gk, gv, gg, la = _mixin(
        h_all, mod, row2(norm_pre[0, 1]), _layout_w_in(w_in[0]).astype(BF16),
        row2(mla_q_norm[0]), _layout_w_q(mla_w_qb[0]).astype(BF16),
        row2(mla_kv_norm[0]), _layout_w_kv(mla_w_kvb[0]).astype(BF16),
        _layout_w_a(gla_w_a_fwd[0], gla_w_a_bwd[0]).astype(BF16),
        jnp.concatenate([gla_b_a_fwd[0], gla_b_a_bwd[0]]).reshape(1, -1), n_lat=n_lat)

    mla = _attn(q, k, v, n_lat=n_lat, bq=ATTN_Q_TILE, bk=_attn_key_block(n_all))
    og = _gla(gq, gk, gv, la, n_lat=n_lat)

    out = _mixout_ffn(h_all, mla, og, gg, row2(gla_norm[0]), w_out[0].astype(BF16), mod,
                      row2(norm_post[0, 1]), row2(norm_pre[0, 2]), row2(norm_post[0, 2]),
                      ffn2_w_in[0].astype(BF16), ffn2_w_out[0].astype(BF16), n_lat=n_lat)
    return out[None]
```

```python
import functools
import math

import jax
import jax.numpy as jnp
from jax import lax
from jax.experimental import pallas as pl
from jax.experimental.pallas import tpu as pltpu

F32 = jnp.float32
BF16 = jnp.bfloat16

D_MODEL = 1024
GRID_W = 64
N_MLA_HEADS = 8
MLA_Q_RANK = 256
MLA_KV_RANK = 128
MLA_NOPE_DIM = 64
MLA_ROPE_DIM = 32
MLA_QK_DIM = MLA_NOPE_DIM + MLA_ROPE_DIM
MLA_V_DIM = 64
N_GLA_HEADS = 4
GLA_DK = 64
GLA_DV = 128
GLA_GATE_RANK = 16
GLA_GATE_NORM = 16.0
GLA_CHUNK = 64
D_FF = 2816
MACARON_WEIGHT = 0.5
ROPE_BASE = 10000.0
EPS = 1e-6
N_MOD = 9

D_MLA_OUT = N_MLA_HEADS * MLA_V_DIM
D_GLA_OUT = N_GLA_HEADS * GLA_DV
D_GLA_QK = N_GLA_HEADS * GLA_DK

LANES = 128
HEAD_PAD = 128
VT_ROWS = 80
ROW_TILE = 256
FUSED_TILE = 512
ATTN_Q_TILE = 512
ATTN_UNROLL = 4
VMEM_LIMIT = 56 * 1024 * 1024
C_QA = 0
C_KV = C_QA + MLA_Q_RANK
C_GQ = C_KV + MLA_KV_RANK
C_GK = C_GQ + D_GLA_QK
C_GV = C_GK + D_GLA_QK
C_GG = C_GV + D_GLA_OUT
C_MISC = C_GG + D_GLA_OUT
C_KSW = C_MISC + LANES
C_TOTAL = C_KSW + LANES
ROPE_LANE0 = MLA_NOPE_DIM
GLA_SUB_LEVELS = (32, 16, 8, 4, 2, 1)
GLA_DIAG = 64


def _rms(x, g):
    return x * lax.rsqrt(jnp.mean(x * x, axis=-1, keepdims=True) + EPS) * g


def _silu(x):
    return x * jax.nn.sigmoid(x)


def _params(*sem):
    return pltpu.CompilerParams(dimension_semantics=sem, vmem_limit_bytes=VMEM_LIMIT)


def _ada_kernel(ct_ref, w_ref, b_ref, o_ref):
    s = _silu(ct_ref[...])
    w = w_ref[...]
    r0 = jnp.sum(s[:, 0:1] * w, axis=0, keepdims=True)
    r1 = jnp.sum(s[:, 1:2] * w, axis=0, keepdims=True)
    o_ref[...] = jnp.concatenate([r0, r1], axis=0) + b_ref[...]


def _ada(ct, w, b):
    d, n = w.shape
    tn = n // 8
    return pl.pallas_call(
        _ada_kernel,
        grid=(n // tn,),
        in_specs=[pl.BlockSpec((d, 2), lambda j: (0, 0)),
                  pl.BlockSpec((d, tn), lambda j: (0, j)),
                  pl.BlockSpec((1, tn), lambda j: (0, j))],
        out_specs=pl.BlockSpec((2, tn), lambda j: (0, j)),
        out_shape=jax.ShapeDtypeStruct((2, n), F32),
        compiler_params=_params("arbitrary"),
        name="adaln",
    )(ct, w, b)


def _ffn_kernel(x_ref, ctx_ref, mod_ref, gpre_ref, gpost_ref, win_ref, wout_ref, o_ref, octx_ref):
    args = (gpre_ref[...], gpost_ref[...], win_ref, wout_ref)
    o_ref[...] = _ffn_tile(x_ref[...], mod_ref[0], 0, *args)

    @pl.when(pl.program_id(0) == pl.num_programs(0) - 1)
    def _():
        octx_ref[...] = _ffn_tile(ctx_ref[...], mod_ref[1], 0, *args)


def _ffn_tile(x, m, mod0, g_pre, g_post, win_ref, wout_ref):
    shift, scale, gate = m[mod0:mod0 + 1], m[mod0 + 1:mod0 + 2], m[mod0 + 2:mod0 + 3]
    h = _rms(x, g_pre) * (1.0 + scale) + shift
    ab = jnp.dot(h.astype(BF16), win_ref[...], preferred_element_type=F32)
    a, b = ab[:, :D_FF], ab[:, D_FF:]
    g = (_silu(a) * b).astype(BF16)
    y = jnp.dot(g, wout_ref[...], preferred_element_type=F32)
    return x + (MACARON_WEIGHT * gate) * _rms(y, g_post)


def _mod_spec(n_lat_blocks):
    return pl.BlockSpec((1, N_MOD, D_MODEL), lambda i: (jnp.where(i < n_lat_blocks, 0, 1), 0, 0))


def _ffn1(x, ctx, mod, g_pre, g_post, w_in, w_out):
    tm = FUSED_TILE
    const = lambda i: (0, 0)
    return pl.pallas_call(
        _ffn_kernel,
        grid=(x.shape[0] // tm,),
        in_specs=[pl.BlockSpec((tm, D_MODEL), lambda i: (i, 0)),
                  pl.BlockSpec(ctx.shape, const),
                  pl.BlockSpec(mod.shape, lambda i: (0, 0, 0)),
                  pl.BlockSpec((1, D_MODEL), const),
                  pl.BlockSpec((1, D_MODEL), const),
                  pl.BlockSpec((D_MODEL, 2 * D_FF), const),
                  pl.BlockSpec((D_FF, D_MODEL), const)],
        out_specs=(pl.BlockSpec((tm, D_MODEL), lambda i: (i, 0)), pl.BlockSpec(ctx.shape, const)),
        out_shape=(jax.ShapeDtypeStruct(x.shape, F32), jax.ShapeDtypeStruct(ctx.shape, F32)),
        compiler_params=_params("arbitrary"),
        name="ffn",
    )(x, ctx, mod, g_pre, g_post, w_in, w_out)


def _rope_tables(i, tm, is_lat):
    n_rows = tm // GRID_W
    assert tm % GRID_W == 0 and n_rows <= 8
    lane = lax.broadcasted_iota(jnp.int32, (GRID_W, LANES), 1)
    sub = lax.broadcasted_iota(jnp.int32, (GRID_W, LANES), 0)
    e = lane - ROPE_LANE0
    in_rope = (e >= 0) & (e < MLA_ROPE_DIM)
    axis_dim = MLA_ROPE_DIM // 2
    row_axis = (e >> 4) == 0
    w = e & (axis_dim - 1)
    j = w & (axis_dim // 2 - 1)
    first_half = w < axis_dim // 2
    inv_freq = jnp.exp(j.astype(F32) * (-math.log(ROPE_BASE) * 2.0 / axis_dim))
    col_ang = sub.astype(F32) * inv_freq
    row_ang = (i * n_rows + sub[:8]).astype(F32) * inv_freq[:8]
    sign = jnp.where(first_half, -1.0, 1.0)

    def table(fn, off_value, scale):
        col_t, row_t = fn(col_ang), fn(row_ang)
        blocks = [jnp.where(row_axis, jnp.broadcast_to(row_t[r:r + 1], (GRID_W, LANES)), col_t)
                  for r in range(n_rows)]
        blocks = [jnp.where(in_rope, jnp.where(is_lat, b, off_value) * scale, 0.0) for b in blocks]
        return jnp.concatenate(blocks, axis=0)

    cos_k = table(jnp.cos, 1.0, 1.0)
    sin_s = table(jnp.sin, 0.0, sign)
    lane_t = lax.broadcasted_iota(jnp.int32, (tm, LANES), 1)
    cos_q = jnp.where(lane_t < ROPE_LANE0, 1.0, cos_k)
    return cos_q, cos_k, sin_s


def _mixin_kernel(h_ref, hctx_ref, mod_ref, gpre_ref, win_ref, qn_ref, wq_ref, kvn_ref, wkv_ref, wa_ref, ba_ref,
                  q_ref, k_ref, v_ref, gq_ref, gk_ref, gv_ref, gg_ref, la_ref, *, n_lat_blocks):
    i = pl.program_id(0)
    tm = h_ref.shape[0]
    m = mod_ref[0]
    shift, scale = m[3:4], m[4:5]
    h = jnp.where(i < n_lat_blocks, h_ref[...], hctx_ref[...])
    a = _rms(h, gpre_ref[...]) * (1.0 + scale) + shift
    z = jnp.dot(a.astype(BF16), win_ref[...], preferred_element_type=F32)

    cos_q, cos_k, sin_s = _rope_tables(i, tm, i < n_lat_blocks)

    qn = _rms(z[:, C_QA:C_QA + MLA_Q_RANK], qn_ref[...]).astype(BF16)
    qq = jnp.dot(qn, wq_ref[...], preferred_element_type=F32)
    hw = N_MLA_HEADS * HEAD_PAD
    sm_scale = MLA_QK_DIM ** -0.5 * math.log2(math.e)
    for h in range(N_MLA_HEADS):
        lo = h * HEAD_PAD
        qh = qq[:, lo:lo + HEAD_PAD] * cos_q + qq[:, hw + lo:hw + lo + HEAD_PAD] * sin_s
        q_ref[h] = (qh * sm_scale).astype(BF16)

    kvn = _rms(z[:, C_KV:C_KV + MLA_KV_RANK], kvn_ref[...]).astype(BF16)
    kv = jnp.dot(kvn, wkv_ref[...], preferred_element_type=F32)
    misc = z[:, C_MISC:C_MISC + LANES]
    k_rope = misc * cos_k + z[:, C_KSW:C_KSW + LANES] * sin_s
    lane = lax.broadcasted_iota(jnp.int32, (tm, HEAD_PAD), 1)
    ones_col = (lane == MLA_V_DIM).astype(F32)
    for h in range(N_MLA_HEADS):
        lo = h * HEAD_PAD
        k_ref[h] = (kv[:, lo:lo + HEAD_PAD] + k_rope).astype(BF16)
        v_ref[h] = (kv[:, hw + lo:hw + lo + HEAD_PAD] + ones_col).T[:VT_ROWS].astype(BF16)

    gq_ref[...] = z[:, C_GQ:C_GQ + D_GLA_QK] * (GLA_DK ** -0.5)
    gk_ref[...] = z[:, C_GK:C_GK + D_GLA_QK]
    gv_ref[...] = z[:, C_GV:C_GV + D_GLA_OUT]
    gg_ref[...] = z[:, C_GG:C_GG + D_GLA_OUT]
    xg = jnp.dot(misc.astype(BF16), wa_ref[...], preferred_element_type=F32) + ba_ref[...]
    la = (jnp.minimum(xg, 0.0) - jnp.log1p(jnp.exp(-jnp.abs(xg)))) * (1.0 / GLA_GATE_NORM)
    la_ref[0] = la[:, :D_GLA_QK]
    la_ref[1] = la[:, D_GLA_QK:]


def _mixin(h_lat, h_ctx, mod, g_pre, w_in, q_norm, w_q, kv_norm, w_kv, w_a, b_a, *, n_lat):
    n_all = h_lat.shape[0] + h_ctx.shape[0]
    tm = ROW_TILE
    nl = n_lat // tm
    row = lambda i: (i, 0)
    const = lambda i: (0, 0)
    hrow = lambda i: (0, i, 0)
    full = lambda arr: pl.BlockSpec(arr.shape, const)
    out_shape = (
        jax.ShapeDtypeStruct((N_MLA_HEADS, n_all, HEAD_PAD), BF16),
        jax.ShapeDtypeStruct((N_MLA_HEADS, n_all, HEAD_PAD), BF16),
        jax.ShapeDtypeStruct((N_MLA_HEADS, VT_ROWS, n_all), BF16),
        jax.ShapeDtypeStruct((n_all, D_GLA_QK), F32),
        jax.ShapeDtypeStruct((n_all, D_GLA_QK), F32),
        jax.ShapeDtypeStruct((n_all, D_GLA_OUT), F32),
        jax.ShapeDtypeStruct((n_all, D_GLA_OUT), F32),
        jax.ShapeDtypeStruct((2, n_all, D_GLA_QK), F32),
    )
    out_specs = (
        pl.BlockSpec((N_MLA_HEADS, tm, HEAD_PAD), hrow),
        pl.BlockSpec((N_MLA_HEADS, tm, HEAD_PAD), hrow),
        pl.BlockSpec((N_MLA_HEADS, VT_ROWS, tm), lambda i: (0, 0, i)),
        pl.BlockSpec((tm, D_GLA_QK), row),
        pl.BlockSpec((tm, D_GLA_QK), row),
        pl.BlockSpec((tm, D_GLA_OUT), row),
        pl.BlockSpec((tm, D_GLA_OUT), row),
        pl.BlockSpec((2, tm, D_GLA_QK), hrow),
    )
    return pl.pallas_call(
        functools.partial(_mixin_kernel, n_lat_blocks=n_lat // tm),
        grid=(n_all // tm,),
        in_specs=[pl.BlockSpec((tm, D_MODEL), lambda i: (jnp.minimum(i, nl - 1), 0)),
                  pl.BlockSpec((tm, D_MODEL), lambda i: (jnp.maximum(i - nl, 0), 0)),
                  _mod_spec(nl), full(g_pre), full(w_in), full(q_norm), full(w_q), full(kv_norm), full(w_kv),
                  full(w_a), full(b_a)],
        out_specs=out_specs,
        out_shape=out_shape,
        compiler_params=_params("arbitrary"),
        name="mixer_in",
    )(h_lat, h_ctx, mod, g_pre, w_in, q_norm, w_q, kv_norm, w_kv, w_a, b_a)


def _attn_kernel(q_ref, k_ref, vt_ref, o_ref, sa_ref, sb_ref, *, bk):
    n_keys = k_ref.shape[1]
    bq = q_ref.shape[1]
    nt = (((1,), (1,)), ((), ()))
    qs = [q_ref[h] for h in range(2)]

    n_trips = n_keys // bk

    def scores(h, c):
        off = pl.multiple_of(c * bk, bk)
        return lax.dot_general(k_ref[h, pl.ds(off, bk), :], qs[h], nt,
                               preferred_element_type=F32)

    def fill(s_ref, c):
        mx = []
        for h in range(2):
            st = scores(h, c)
            s_ref[h] = st
            mx.append(jnp.max(st, axis=0, keepdims=True))
        return tuple(mx)

    def update(carry, s_ref, mx, c):
        off = pl.multiple_of(c * bk, bk)
        new = []
        for h in range(2):
            m, acc = carry[h]
            m_new = jnp.maximum(m, mx[h])
            pt = jnp.exp2(s_ref[h] - m_new).astype(BF16)
            acc = jnp.exp2(m - m_new) * acc + jnp.dot(vt_ref[h, :, pl.ds(off, bk)], pt,
                                                      preferred_element_type=F32)
            new.append((m_new, acc))
        return tuple(new)

    bufs = (sa_ref, sb_ref)

    def run(carry, mx, first, count, fill_after_last):
        for u in range(count):
            nxt = None
            if u < count - 1 or fill_after_last:
                nxt = fill(bufs[(u + 1) % 2], first + u + 1)
            carry = update(carry, bufs[u % 2], mx, first + u)
            mx = nxt
        return carry, mx

    def group(i, state):
        return run(*state, ATTN_UNROLL * i, ATTN_UNROLL, True)

    carry = tuple((jnp.full((1, bq), -jnp.inf, F32), jnp.zeros((VT_ROWS, bq), F32)) for _ in range(2))
    n_groups = (n_trips - 1) // ATTN_UNROLL
    state = lax.fori_loop(0, n_groups, group, (carry, fill(sa_ref, 0)))
    carry, _ = run(*state, ATTN_UNROLL * n_groups, n_trips - ATTN_UNROLL * n_groups, False)
    outs = [acc[:MLA_V_DIM] / acc[MLA_V_DIM:MLA_V_DIM + 1] for _, acc in carry]
    o_ref[...] = jnp.concatenate(outs, axis=0).astype(o_ref.dtype)


def _attn(q, k, vt, *, n_lat, bq, bk):
    n_all = k.shape[1]
    return pl.pallas_call(
        functools.partial(_attn_kernel, bk=bk),
        grid=(N_MLA_HEADS // 2, n_lat // bq),
        in_specs=[pl.BlockSpec((2, bq, HEAD_PAD), lambda hp, i: (hp, i, 0)),
                  pl.BlockSpec((2, n_all, HEAD_PAD), lambda hp, i: (hp, 0, 0)),
                  pl.BlockSpec((2, VT_ROWS, n_all), lambda hp, i: (hp, 0, 0))],
        out_specs=pl.BlockSpec((2 * MLA_V_DIM, bq), lambda hp, i: (hp, i)),
        out_shape=jax.ShapeDtypeStruct((D_MLA_OUT, n_lat), BF16),
        scratch_shapes=[pltpu.VMEM((2, bk, bq), F32), pltpu.VMEM((2, bk, bq), F32)],
        compiler_params=_params("arbitrary", "arbitrary"),
        name="mla_attention",
    )(q, k, vt)


def _gla_level_matrix(fwd):
    c = GLA_CHUNK
    ri = lax.broadcasted_iota(jnp.int32, (c, c), 0)
    ci = lax.broadcasted_iota(jnp.int32, (c, c), 1)

    def sides(b):
        bs = ri & (~(b - 1))
        be = bs + (b - 1)
        q_f = ((ci >= bs) & (ci <= ri)).astype(F32)
        k_f = ((ci > ri) & (ci <= be)).astype(F32)
        q_b = ((ci >= ri) & (ci <= be)).astype(F32)
        k_b = ((ci < ri) & (ci >= bs)).astype(F32)
        return q_f, k_f, q_b, k_b

    q_f, k_f, q_b, k_b = sides(c)
    blocks = [jnp.where(fwd, q_f, q_b), jnp.where(fwd, k_f, k_b)]
    for b in GLA_SUB_LEVELS:
        q_f, k_f, q_b, k_b = sides(b)
        bit = (ri & b) != 0
        blocks.append(jnp.where(fwd, jnp.where(bit, q_f, k_f), jnp.where(bit, k_b, q_b)))
    return jnp.concatenate(blocks, axis=0)


def _gla_level_ids(fwd, n):
    ri = lax.broadcasted_iota(jnp.int32, (n, n), 0)
    ci = lax.broadcasted_iota(jnp.int32, (n, n), 1)
    xr = ri ^ ci
    later = jnp.where(fwd, ri, ci)
    ids = jnp.where(xr == 0, GLA_DIAG, 0)
    for b in GLA_SUB_LEVELS:
        ids = jnp.where(((xr & (-b)) == b) & ((later & b) != 0), b, ids)
    return ids


def _gla_kernel(q_ref, k_ref, v_ref, la_ref, o_ref, st_ref, w3_ref, ids_ref):
    d = pl.program_id(0)
    s = pl.program_id(1)
    fwd = d == 0
    c = GLA_CHUNK
    tm = q_ref.shape[0]
    n_chunks = tm // c
    n_pairs = N_GLA_HEADS // 2
    nt = (((1,), (1,)), ((), ()))
    tn = (((0,), (0,)), ((), ()))

    @pl.when(s == 0)
    def _():
        st_ref[...] = jnp.zeros_like(st_ref)
        w = _gla_level_matrix(fwd).astype(BF16)
        w3_ref[...] = jnp.concatenate([w, w, w], axis=1)
        ids_ref[...] = _gla_level_ids(fwd, tm)

    g = la_ref[0]
    g = jnp.concatenate([g[t * c:(t + 1) * c] for t in range(n_chunks)], axis=1)
    g_hi = g.astype(BF16)
    r1 = g - g_hi.astype(F32)
    g_mid = r1.astype(BF16)
    g_lo = (r1 - g_mid.astype(F32)).astype(BF16)
    e_all = jnp.exp(jnp.dot(w3_ref[...], jnp.concatenate([g_hi, g_mid, g_lo], axis=0),
                            preferred_element_type=F32))

    def e_block(blk, p):
        return jnp.concatenate(
            [e_all[blk * c:(blk + 1) * c, t * D_GLA_QK + p * LANES:t * D_GLA_QK + (p + 1) * LANES]
             for t in range(n_chunks)], axis=0)

    lane = lax.broadcasted_iota(jnp.int32, (tm, LANES), 1)
    lane_v = lax.broadcasted_iota(jnp.int32, (GLA_DV, LANES), 1)
    ids = ids_ref[...]
    ids2 = jnp.concatenate([ids, ids], axis=0)
    vb = v_ref[...].astype(BF16)

    for p in range(n_pairs):
        ls = slice(p * LANES, (p + 1) * LANES)
        qp, kp = q_ref[:, ls], k_ref[:, ls]
        qm = (jnp.where(lane < GLA_DK, qp, 0.0), jnp.where(lane >= GLA_DK, qp, 0.0))

        att = jnp.where(ids2 == GLA_DIAG,
                        lax.dot_general(jnp.concatenate(qm, axis=0).astype(BF16), kp.astype(BF16), nt,
                                        preferred_element_type=F32), 0.0)
        for li, b in enumerate(GLA_SUB_LEVELS):
            eb = e_block(2 + li, p)
            lhs = jnp.concatenate([qm[0] * eb, qm[1] * eb], axis=0).astype(BF16)
            part = lax.dot_general(lhs, (kp * eb).astype(BF16), nt, preferred_element_type=F32)
            att = jnp.where(ids2 == b, part, att)
        att = att.astype(BF16)

        eq, ek = e_block(0, p), e_block(1, p)
        k_inter = (kp * ek).astype(BF16)
        decay, upd = [], []
        for t in range(n_chunks):
            rows = slice(t * c, (t + 1) * c)
            e_q = e_all[0:c, t * D_GLA_QK + p * LANES:t * D_GLA_QK + (p + 1) * LANES]
            decay.append(jnp.where(fwd, e_q[c - 1:c], e_q[0:1]))
            u = lax.dot_general(vb[rows, 2 * p * GLA_DV:(2 * p + 2) * GLA_DV], k_inter[rows], tn,
                                preferred_element_type=F32)
            upd.append(jnp.where(lane_v < GLA_DK, u[:GLA_DV], u[GLA_DV:]))
        st = st_ref[p]
        before = []
        for t in range(n_chunks):
            r = n_chunks - 1 - t
            before.append(st)
            st = jnp.where(fwd, decay[t], decay[r]) * st + jnp.where(fwd, upd[t], upd[r])
        st_ref[p] = st

        for t in range(n_chunks):
            rows = slice(t * c, (t + 1) * c)
            s_t = jnp.where(fwd, before[t], before[n_chunks - 1 - t]).astype(BF16)
            lhs = jnp.concatenate([qm[0][rows] * eq[rows], qm[1][rows] * eq[rows]], axis=0).astype(BF16)
            o_inter = lax.dot_general(lhs, s_t, nt, preferred_element_type=F32)
            for hh in range(2):
                h = 2 * p + hh
                o_intra = jnp.dot(att[hh * tm + t * c:hh * tm + (t + 1) * c],
                                  vb[:, h * GLA_DV:(h + 1) * GLA_DV], preferred_element_type=F32)
                o_ref[0, rows, h * GLA_DV:(h + 1) * GLA_DV] = o_inter[hh * c:(hh + 1) * c] + o_intra


def _gla(gq, gk, gv, la, *, n_lat):
    n_all = gq.shape[0]
    tm = ROW_TILE
    nl = n_lat // tm
    nb = n_all // tm
    nc = nb - nl

    def blk(d, s):
        ctx = jnp.where(d == 0, nl + s, nb - 1 - s)
        lat = jnp.where(d == 0, s - nc, nb - 1 - s)
        return jnp.where(s < nc, ctx, lat)

    row = lambda d, s: (blk(d, s), 0)
    return pl.pallas_call(
        _gla_kernel,
        grid=(2, nb),
        in_specs=[pl.BlockSpec((tm, D_GLA_QK), row),
                  pl.BlockSpec((tm, D_GLA_QK), row),
                  pl.BlockSpec((tm, D_GLA_OUT), row),
                  pl.BlockSpec((1, tm, D_GLA_QK), lambda d, s: (d, blk(d, s), 0))],
        out_specs=pl.BlockSpec((1, tm, D_GLA_OUT), lambda d, s: (d, blk(d, s), 0)),
        out_shape=jax.ShapeDtypeStruct((2, n_all, D_GLA_OUT), F32),
        scratch_shapes=[pltpu.VMEM((N_GLA_HEADS // 2, GLA_DV, 2 * GLA_DK), F32),
                        pltpu.VMEM(((2 + len(GLA_SUB_LEVELS)) * GLA_CHUNK, 3 * GLA_CHUNK), BF16),
                        pltpu.VMEM((tm, tm), jnp.int32)],
        compiler_params=_params("arbitrary", "arbitrary"),
        name="gla_scan",
    )(gq, gk, gv, la)


def _mixout_ffn_kernel(h_ref, mlat_ref, og_ref, gg_ref, gn_ref, wmix_ref, mod_ref, gpost1_ref,
                       gpre2_ref, gpost2_ref, win_ref, wout_ref, o_ref):
    o = og_ref[0] + og_ref[1]
    gate = _silu(gg_ref[...])
    parts = []
    for h in range(N_GLA_HEADS):
        sl = slice(h * GLA_DV, (h + 1) * GLA_DV)
        parts.append((_rms(o[:, sl], gn_ref[...]) * gate[:, sl]).astype(BF16))
    tn = (((0,), (0,)), ((), ()))
    y = lax.dot_general(mlat_ref[...], wmix_ref[:D_MLA_OUT, :], tn, preferred_element_type=F32)
    y = y + jnp.dot(jnp.concatenate(parts, axis=-1), wmix_ref[D_MLA_OUT:, :], preferred_element_type=F32)
    m = mod_ref[0]
    x = h_ref[...] + m[5:6] * _rms(y, gpost1_ref[...])
    o_ref[...] = _ffn_tile(x, m, 6, gpre2_ref[...], gpost2_ref[...], win_ref, wout_ref)


def _mixout_ffn(h_lat, mla_t, og, gg, g_norm, w_mix, mod, g_post1, g_pre2, g_post2, w_in, w_out, *, n_lat):
    tm = FUSED_TILE
    row = lambda i: (i, 0)
    const = lambda i: (0, 0)
    vec = pl.BlockSpec((1, D_MODEL), const)
    return pl.pallas_call(
        _mixout_ffn_kernel,
        grid=(n_lat // tm,),
        in_specs=[pl.BlockSpec((tm, D_MODEL), row),
                  pl.BlockSpec((D_MLA_OUT, tm), lambda i: (0, i)),
                  pl.BlockSpec((2, tm, D_GLA_OUT), lambda i: (0, i, 0)),
                  pl.BlockSpec((tm, D_GLA_OUT), row),
                  pl.BlockSpec((1, GLA_DV), const),
                  pl.BlockSpec(w_mix.shape, const),
                  pl.BlockSpec((1, N_MOD, D_MODEL), lambda i: (0, 0, 0)),
                  vec, vec, vec,
                  pl.BlockSpec((D_MODEL, 2 * D_FF), const),
                  pl.BlockSpec((D_FF, D_MODEL), const)],
        out_specs=pl.BlockSpec((tm, D_MODEL), row),
        out_shape=jax.ShapeDtypeStruct((n_lat, D_MODEL), F32),
        compiler_params=_params("arbitrary"),
        name="mixer_out_ffn",
    )(h_lat, mla_t, og, gg, g_norm, w_mix, mod, g_post1, g_pre2, g_post2, w_in, w_out)


def _rope_half_swap(w):
    half = MLA_ROPE_DIM // 4
    g = w.reshape(w.shape[:-1] + (2, 2, half))
    return g[..., ::-1, :].reshape(w.shape)


def _layout_w_in(w):
    d = w.shape[0]
    o_kv = MLA_Q_RANK
    o_kr = o_kv + MLA_KV_RANK
    o_gq = o_kr + MLA_ROPE_DIM
    o_gk = o_gq + D_GLA_QK
    o_gv = o_gk + D_GLA_QK
    o_gg = o_gv + D_GLA_OUT
    o_ga = o_gg + D_GLA_OUT
    k_rope = w[:, o_kr:o_gq]
    z = lambda n: jnp.zeros((d, n), w.dtype)
    pad = LANES - ROPE_LANE0 - MLA_ROPE_DIM
    return jnp.concatenate([
        w[:, :o_kr], w[:, o_gq:o_ga],
        w[:, o_ga:o_ga + 2 * GLA_GATE_RANK], z(ROPE_LANE0 - 2 * GLA_GATE_RANK), k_rope, z(pad),
        z(ROPE_LANE0), _rope_half_swap(k_rope), z(pad)], axis=1)


def _layout_w_q(w):
    r = w.shape[0]
    w = w.reshape(r, N_MLA_HEADS, MLA_QK_DIM)
    nope, rope = w[..., :MLA_NOPE_DIM], w[..., MLA_NOPE_DIM:]
    zp = jnp.zeros((r, N_MLA_HEADS, HEAD_PAD - MLA_QK_DIM), w.dtype)
    zn = jnp.zeros_like(nope)
    plain = jnp.concatenate([nope, rope, zp], axis=-1).reshape(r, -1)
    swapped = jnp.concatenate([zn, _rope_half_swap(rope), zp], axis=-1).reshape(r, -1)
    return jnp.concatenate([plain, swapped], axis=1)


def _layout_w_kv(w):
    r = w.shape[0]
    w = w.reshape(r, N_MLA_HEADS, MLA_NOPE_DIM + MLA_V_DIM)
    k_nope, v = w[..., :MLA_NOPE_DIM], w[..., MLA_NOPE_DIM:]
    zk = jnp.zeros((r, N_MLA_HEADS, HEAD_PAD - MLA_NOPE_DIM), w.dtype)
    zv = jnp.zeros((r, N_MLA_HEADS, HEAD_PAD - MLA_V_DIM), w.dtype)
    return jnp.concatenate([jnp.concatenate([k_nope, zk], axis=-1).reshape(r, -1),
                            jnp.concatenate([v, zv], axis=-1).reshape(r, -1)], axis=1)


def _layout_w_a(w_f, w_b):
    z = jnp.zeros((LANES, 2 * D_GLA_QK), w_f.dtype)
    z = z.at[:GLA_GATE_RANK, :D_GLA_QK].set(w_f)
    return z.at[GLA_GATE_RANK:2 * GLA_GATE_RANK, D_GLA_QK:].set(w_b)


def _attn_key_block(n_all):
    for bk in (640, 512, 256):
        if n_all % bk == 0:
            return bk
    raise ValueError(f"unsupported key count {n_all}")


def kernel(x, c, ctx, c_ctx, w_ada, b_ada, norm_pre, norm_post, ffn1_w_in, ffn1_w_out, ffn2_w_in, ffn2_w_out, w_in, mla_q_norm, mla_w_qb, mla_kv_norm, mla_w_kvb, gla_w_a_fwd, gla_b_a_fwd, gla_w_a_bwd, gla_b_a_bwd, gla_norm, w_out):
    assert x.shape[0] == 1 and ctx.shape[0] == 1 and w_ada.shape[0] == 1
    n_lat, n_ctx = x.shape[1], ctx.shape[1]
    assert n_lat % ROW_TILE == 0 and n_ctx % ROW_TILE == 0 and n_lat % GRID_W == 0
    n_all = n_lat + n_ctx
    row2 = lambda a: a.reshape(1, -1)

    ct = jnp.stack([c[0], c_ctx], axis=1)
    mod = _ada(ct, w_ada[0], row2(b_ada[0])).reshape(2, N_MOD, D_MODEL)

    h_lat, h_ctx = _ffn1(x[0], ctx[0], mod, row2(norm_pre[0, 0]), row2(norm_post[0, 0]),
                  ffn1_w_in[0].astype(BF16), ffn1_w_out[0].astype(BF16))

    q, k, v, gq, gk, gv, gg, la = _mixin(
        h_lat, h_ctx, mod, row2(norm_pre[0, 1]), _layout_w_in(w_in[0]).astype(BF16),
        row2(mla_q_norm[0]), _layout_w_q(mla_w_qb[0]).astype(BF16),
        row2(mla_kv_norm[0]), _layout_w_kv(mla_w_kvb[0]).astype(BF16),
        _layout_w_a(gla_w_a_fwd[0], gla_w_a_bwd[0]).astype(BF16),
        jnp.concatenate([gla_b_a_fwd[0], gla_b_a_bwd[0]]).reshape(1, -1), n_lat=n_lat)

    mla = _attn(q, k, v, n_lat=n_lat, bq=ATTN_Q_TILE, bk=_attn_key_block(n_all))
    og = _gla(gq, gk, gv, la, n_lat=n_lat)

    out = _mixout_ffn(h_lat, mla, og, gg, row2(gla_norm[0]), w_out[0].astype(BF16), mod,
                      row2(norm_post[0, 1]), row2(norm_pre[0, 2]), row2(norm_post[0, 2]),
                      ffn2_w_in[0].astype(BF16), ffn2_w_out[0].astype(BF16), n_lat=n_lat)
    return out[None]
```

```python
import functools
import math

import jax
import jax.numpy as jnp
from jax import lax
from jax.experimental import pallas as pl
from jax.experimental.pallas import tpu as pltpu

F32 = jnp.float32
BF16 = jnp.bfloat16

D_MODEL = 1024
GRID_W = 64
N_MLA_HEADS = 8
MLA_Q_RANK = 256
MLA_KV_RANK = 128
MLA_NOPE_DIM = 64
MLA_ROPE_DIM = 32
MLA_QK_DIM = MLA_NOPE_DIM + MLA_ROPE_DIM
MLA_V_DIM = 64
N_GLA_HEADS = 4
GLA_DK = 64
GLA_DV = 128
GLA_GATE_RANK = 16
GLA_GATE_NORM = 16.0
GLA_CHUNK = 64
D_FF = 2816
MACARON_WEIGHT = 0.5
ROPE_BASE = 10000.0
EPS = 1e-6
N_MOD = 9

D_MLA_OUT = N_MLA_HEADS * MLA_V_DIM
D_GLA_OUT = N_GLA_HEADS * GLA_DV
D_GLA_QK = N_GLA_HEADS * GLA_DK

LANES = 128
HEAD_PAD = 128
VT_ROWS = 128
ROW_TILE = 256
FUSED_TILE = 512
ATTN_Q_TILE = 512
ATTN_UNROLL = 8
VMEM_LIMIT = 56 * 1024 * 1024
C_QA = 0
C_KV = C_QA + MLA_Q_RANK
C_GQ = C_KV + MLA_KV_RANK
C_GK = C_GQ + D_GLA_QK
C_GV = C_GK + D_GLA_QK
C_GG = C_GV + D_GLA_OUT
C_MISC = C_GG + D_GLA_OUT
C_KSW = C_MISC + LANES
C_TOTAL = C_KSW + LANES
ROPE_LANE0 = MLA_NOPE_DIM
GLA_SUB_LEVELS = (32, 16, 8, 4, 2, 1)
GLA_DIAG = 64


def _rms(x, g):
    return x * lax.rsqrt(jnp.mean(x * x, axis=-1, keepdims=True) + EPS) * g


def _silu(x):
    return x * jax.nn.sigmoid(x)


def _params(*sem):
    return pltpu.CompilerParams(dimension_semantics=sem, vmem_limit_bytes=VMEM_LIMIT)


def _ada_kernel(ct_ref, w_ref, b_ref, o_ref):
    s = _silu(ct_ref[...])
    w = w_ref[...]
    r0 = jnp.sum(s[:, 0:1] * w, axis=0, keepdims=True)
    r1 = jnp.sum(s[:, 1:2] * w, axis=0, keepdims=True)
    o_ref[...] = jnp.concatenate([r0, r1], axis=0) + b_ref[...]


def _ada(ct, w, b):
    d, n = w.shape
    tn = n // 8
    return pl.pallas_call(
        _ada_kernel,
        grid=(n // tn,),
        in_specs=[pl.BlockSpec((d, 2), lambda j: (0, 0)),
                  pl.BlockSpec((d, tn), lambda j: (0, j)),
                  pl.BlockSpec((1, tn), lambda j: (0, j))],
        out_specs=pl.BlockSpec((2, tn), lambda j: (0, j)),
        out_shape=jax.ShapeDtypeStruct((2, n), F32),
        compiler_params=_params("arbitrary"),
        name="adaln",
    )(ct, w, b)


def _ffn_kernel(x_ref, ctx_ref, mod_ref, gpre_ref, gpost_ref, win_ref, wout_ref, o_ref, octx_ref):
    args = (gpre_ref[...], gpost_ref[...], win_ref, wout_ref)
    o_ref[...] = _ffn_tile(x_ref[...], mod_ref[0], 0, *args)

    @pl.when(pl.program_id(0) == pl.num_programs(0) - 1)
    def _():
        octx_ref[...] = _ffn_tile(ctx_ref[...], mod_ref[1], 0, *args)


def _ffn_tile(x, m, mod0, g_pre, g_post, win_ref, wout_ref):
    shift, scale, gate = m[mod0:mod0 + 1], m[mod0 + 1:mod0 + 2], m[mod0 + 2:mod0 + 3]
    h = _rms(x, g_pre) * (1.0 + scale) + shift
    ab = jnp.dot(h.astype(BF16), win_ref[...], preferred_element_type=F32)
    a, b = ab[:, :D_FF], ab[:, D_FF:]
    g = (_silu(a) * b).astype(BF16)
    y = jnp.dot(g, wout_ref[...], preferred_element_type=F32)
    return x + (MACARON_WEIGHT * gate) * _rms(y, g_post)


def _mod_spec(n_lat_blocks):
    return pl.BlockSpec((1, N_MOD, D_MODEL), lambda i: (jnp.where(i < n_lat_blocks, 0, 1), 0, 0))


def _ffn1(x, ctx, mod, g_pre, g_post, w_in, w_out):
    tm = FUSED_TILE
    const = lambda i: (0, 0)
    return pl.pallas_call(
        _ffn_kernel,
        grid=(x.shape[0] // tm,),
        in_specs=[pl.BlockSpec((tm, D_MODEL), lambda i: (i, 0)),
                  pl.BlockSpec(ctx.shape, const),
                  pl.BlockSpec(mod.shape, lambda i: (0, 0, 0)),
                  pl.BlockSpec((1, D_MODEL), const),
                  pl.BlockSpec((1, D_MODEL), const),
                  pl.BlockSpec((D_MODEL, 2 * D_FF), const),
                  pl.BlockSpec((D_FF, D_MODEL), const)],
        out_specs=(pl.BlockSpec((tm, D_MODEL), lambda i: (i, 0)), pl.BlockSpec(ctx.shape, const)),
        out_shape=(jax.ShapeDtypeStruct(x.shape, F32), jax.ShapeDtypeStruct(ctx.shape, F32)),
        compiler_params=_params("arbitrary"),
        name="ffn",
    )(x, ctx, mod, g_pre, g_post, w_in, w_out)


def _rope_tables(i, tm, is_lat):
    n_rows = tm // GRID_W
    assert tm % GRID_W == 0 and n_rows <= 8
    lane = lax.broadcasted_iota(jnp.int32, (GRID_W, LANES), 1)
    sub = lax.broadcasted_iota(jnp.int32, (GRID_W, LANES), 0)
    e = lane - ROPE_LANE0
    in_rope = (e >= 0) & (e < MLA_ROPE_DIM)
    axis_dim = MLA_ROPE_DIM // 2
    row_axis = (e >> 4) == 0
    w = e & (axis_dim - 1)
    j = w & (axis_dim // 2 - 1)
    first_half = w < axis_dim // 2
    inv_freq = jnp.exp(j.astype(F32) * (-math.log(ROPE_BASE) * 2.0 / axis_dim))
    col_ang = sub.astype(F32) * inv_freq
    row_ang = (i * n_rows + sub[:8]).astype(F32) * inv_freq[:8]
    sign = jnp.where(first_half, -1.0, 1.0)

    def table(fn, off_value, scale):
        col_t, row_t = fn(col_ang), fn(row_ang)
        blocks = [jnp.where(row_axis, jnp.broadcast_to(row_t[r:r + 1], (GRID_W, LANES)), col_t)
                  for r in range(n_rows)]
        blocks = [jnp.where(in_rope, jnp.where(is_lat, b, off_value) * scale, 0.0) for b in blocks]
        return jnp.concatenate(blocks, axis=0)

    cos_k = table(jnp.cos, 1.0, 1.0)
    sin_s = table(jnp.sin, 0.0, sign)
    lane_t = lax.broadcasted_iota(jnp.int32, (tm, LANES), 1)
    cos_q = jnp.where(lane_t < ROPE_LANE0, 1.0, cos_k)
    return cos_q, cos_k, sin_s


def _mixin_kernel(h_ref, hctx_ref, mod_ref, gpre_ref, win_ref, qn_ref, wq_ref, kvn_ref, wkv_ref, wa_ref, ba_ref,
                  q_ref, k_ref, v_ref, gq_ref, gk_ref, gv_ref, gg_ref, la_ref, *, n_lat_blocks):
    i = pl.program_id(0)
    tm = h_ref.shape[0]
    m = mod_ref[0]
    shift, scale = m[3:4], m[4:5]
    h = jnp.where(i < n_lat_blocks, h_ref[...], hctx_ref[...])
    a = _rms(h, gpre_ref[...]) * (1.0 + scale) + shift
    z = jnp.dot(a.astype(BF16), win_ref[...], preferred_element_type=F32)

    cos_q, cos_k, sin_s = _rope_tables(i, tm, i < n_lat_blocks)

    qn = _rms(z[:, C_QA:C_QA + MLA_Q_RANK], qn_ref[...]).astype(BF16)
    qq = jnp.dot(qn, wq_ref[...], preferred_element_type=F32)
    hw = N_MLA_HEADS * HEAD_PAD
    sm_scale = MLA_QK_DIM ** -0.5 * math.log2(math.e)
    for h in range(N_MLA_HEADS):
        lo = h * HEAD_PAD
        qh = qq[:, lo:lo + HEAD_PAD] * cos_q + qq[:, hw + lo:hw + lo + HEAD_PAD] * sin_s
        q_ref[h] = (qh * sm_scale).astype(BF16)

    kvn = _rms(z[:, C_KV:C_KV + MLA_KV_RANK], kvn_ref[...]).astype(BF16)
    kv = jnp.dot(kvn, wkv_ref[...], preferred_element_type=F32)
    misc = z[:, C_MISC:C_MISC + LANES]
    k_rope = misc * cos_k + z[:, C_KSW:C_KSW + LANES] * sin_s
    lane = lax.broadcasted_iota(jnp.int32, (tm, HEAD_PAD), 1)
    ones_col = (lane == MLA_V_DIM).astype(F32)
    for h in range(N_MLA_HEADS):
        lo = h * HEAD_PAD
        k_ref[h] = (kv[:, lo:lo + HEAD_PAD] + k_rope).astype(BF16)
        v_ref[h] = (kv[:, hw + lo:hw + lo + HEAD_PAD] + ones_col).T[:VT_ROWS].astype(BF16)

    gq_ref[...] = z[:, C_GQ:C_GQ + D_GLA_QK] * (GLA_DK ** -0.5)
    gk_ref[...] = z[:, C_GK:C_GK + D_GLA_QK]
    gv_ref[...] = z[:, C_GV:C_GV + D_GLA_OUT]
    gg_ref[...] = z[:, C_GG:C_GG + D_GLA_OUT]
    xg = jnp.dot(misc.astype(BF16), wa_ref[...], preferred_element_type=F32) + ba_ref[...]
    la = (jnp.minimum(xg, 0.0) - jnp.log1p(jnp.exp(-jnp.abs(xg)))) * (1.0 / GLA_GATE_NORM)
    la_ref[0] = la[:, :D_GLA_QK]
    la_ref[1] = la[:, D_GLA_QK:]


def _mixin(h_lat, h_ctx, mod, g_pre, w_in, q_norm, w_q, kv_norm, w_kv, w_a, b_a, *, n_lat):
    n_all = h_lat.shape[0] + h_ctx.shape[0]
    tm = ROW_TILE
    nl = n_lat // tm
    row = lambda i: (i, 0)
    const = lambda i: (0, 0)
    hrow = lambda i: (0, i, 0)
    full = lambda arr: pl.BlockSpec(arr.shape, const)
    out_shape = (
        jax.ShapeDtypeStruct((N_MLA_HEADS, n_all, HEAD_PAD), BF16),
        jax.ShapeDtypeStruct((N_MLA_HEADS, n_all, HEAD_PAD), BF16),
        jax.ShapeDtypeStruct((N_MLA_HEADS, VT_ROWS, n_all), BF16),
        jax.ShapeDtypeStruct((n_all, D_GLA_QK), F32),
        jax.ShapeDtypeStruct((n_all, D_GLA_QK), F32),
        jax.ShapeDtypeStruct((n_all, D_GLA_OUT), F32),
        jax.ShapeDtypeStruct((n_all, D_GLA_OUT), F32),
        jax.ShapeDtypeStruct((2, n_all, D_GLA_QK), F32),
    )
    out_specs = (
        pl.BlockSpec((N_MLA_HEADS, tm, HEAD_PAD), hrow),
        pl.BlockSpec((N_MLA_HEADS, tm, HEAD_PAD), hrow),
        pl.BlockSpec((N_MLA_HEADS, VT_ROWS, tm), lambda i: (0, 0, i)),
        pl.BlockSpec((tm, D_GLA_QK), row),
        pl.BlockSpec((tm, D_GLA_QK), row),
        pl.BlockSpec((tm, D_GLA_OUT), row),
        pl.BlockSpec((tm, D_GLA_OUT), row),
        pl.BlockSpec((2, tm, D_GLA_QK), hrow),
    )
    return pl.pallas_call(
        functools.partial(_mixin_kernel, n_lat_blocks=n_lat // tm),
        grid=(n_all // tm,),
        in_specs=[pl.BlockSpec((tm, D_MODEL), lambda i: (jnp.minimum(i, nl - 1), 0)),
                  pl.BlockSpec((tm, D_MODEL), lambda i: (jnp.maximum(i - nl, 0), 0)),
                  _mod_spec(nl), full(g_pre), full(w_in), full(q_norm), full(w_q), full(kv_norm), full(w_kv),
                  full(w_a), full(b_a)],
        out_specs=out_specs,
        out_shape=out_shape,
        compiler_params=_params("arbitrary"),
        name="mixer_in",
    )(h_lat, h_ctx, mod, g_pre, w_in, q_norm, w_q, kv_norm, w_kv, w_a, b_a)


def _attn_kernel(q_ref, k_ref, vt_ref, o_ref, sa_ref, sb_ref, *, bk):
    n_keys = k_ref.shape[1]
    bq = q_ref.shape[1]
    nt = (((1,), (1,)), ((), ()))
    qs = [q_ref[h] for h in range(2)]

    n_trips = n_keys // bk

    def scores(h, c):
        off = pl.multiple_of(c * bk, bk)
        return lax.dot_general(k_ref[h, pl.ds(off, bk), :], qs[h], nt,
                               preferred_element_type=F32)

    def fill(s_ref, c):
        mx = []
        for h in range(2):
            st = scores(h, c)
            s_ref[h] = st
            mx.append(jnp.max(st, axis=0, keepdims=True))
        return tuple(mx)

    def update(carry, s_ref, mx, c):
        off = pl.multiple_of(c * bk, bk)
        new = []
        for h in range(2):
            m, acc = carry[h]
            m_new = jnp.maximum(m, mx[h])
            pt = jnp.exp2(s_ref[h] - m_new).astype(BF16)
            acc = jnp.exp2(m - m_new) * acc + jnp.dot(vt_ref[h, :, pl.ds(off, bk)], pt,
                                                      preferred_element_type=F32)
            new.append((m_new, acc))
        return tuple(new)

    bufs = (sa_ref, sb_ref)

    def run(carry, mx, first, count, fill_after_last):
        for u in range(count):
            nxt = None
            if u < count - 1 or fill_after_last:
                nxt = fill(bufs[(u + 1) % 2], first + u + 1)
            carry = update(carry, bufs[u % 2], mx, first + u)
            mx = nxt
        return carry, mx

    def group(i, state):
        return run(*state, ATTN_UNROLL * i, ATTN_UNROLL, True)

    carry = tuple((jnp.full((1, bq), -jnp.inf, F32), jnp.zeros((VT_ROWS, bq), F32)) for _ in range(2))
    n_groups = (n_trips - 1) // ATTN_UNROLL
    state = lax.fori_loop(0, n_groups, group, (carry, fill(sa_ref, 0)))
    carry, _ = run(*state, ATTN_UNROLL * n_groups, n_trips - ATTN_UNROLL * n_groups, False)
    outs = [acc[:MLA_V_DIM] / acc[MLA_V_DIM:MLA_V_DIM + 1] for _, acc in carry]
    o_ref[...] = jnp.concatenate(outs, axis=0).astype(o_ref.dtype)


def _attn(q, k, vt, *, n_lat, bq, bk):
    n_all = k.shape[1]
    return pl.pallas_call(
        functools.partial(_attn_kernel, bk=bk),
        grid=(N_MLA_HEADS // 2, n_lat // bq),
        in_specs=[pl.BlockSpec((2, bq, HEAD_PAD), lambda hp, i: (hp, i, 0)),
                  pl.BlockSpec((2, n_all, HEAD_PAD), lambda hp, i: (hp, 0, 0)),
                  pl.BlockSpec((2, VT_ROWS, n_all), lambda hp, i: (hp, 0, 0))],
        out_specs=pl.BlockSpec((2 * MLA_V_DIM, bq), lambda hp, i: (hp, i)),
        out_shape=jax.ShapeDtypeStruct((D_MLA_OUT, n_lat), BF16),
        scratch_shapes=[pltpu.VMEM((2, bk, bq), F32), pltpu.VMEM((2, bk, bq), F32)],
        compiler_params=_params("arbitrary", "arbitrary"),
        name="mla_attention",
    )(q, k, vt)


def _gla_level_matrix(fwd):
    c = GLA_CHUNK
    ri = lax.broadcasted_iota(jnp.int32, (c, c), 0)
    ci = lax.broadcasted_iota(jnp.int32, (c, c), 1)

    def sides(b):
        bs = ri & (~(b - 1))
        be = bs + (b - 1)
        q_f = ((ci >= bs) & (ci <= ri)).astype(F32)
        k_f = ((ci > ri) & (ci <= be)).astype(F32)
        q_b = ((ci >= ri) & (ci <= be)).astype(F32)
        k_b = ((ci < ri) & (ci >= bs)).astype(F32)
        return q_f, k_f, q_b, k_b

    q_f, k_f, q_b, k_b = sides(c)
    blocks = [jnp.where(fwd, q_f, q_b), jnp.where(fwd, k_f, k_b)]
    for b in GLA_SUB_LEVELS:
        q_f, k_f, q_b, k_b = sides(b)
        bit = (ri & b) != 0
        blocks.append(jnp.where(fwd, jnp.where(bit, q_f, k_f), jnp.where(bit, k_b, q_b)))
    return jnp.concatenate(blocks, axis=0)


def _gla_level_ids(fwd, n):
    ri = lax.broadcasted_iota(jnp.int32, (n, n), 0)
    ci = lax.broadcasted_iota(jnp.int32, (n, n), 1)
    xr = ri ^ ci
    later = jnp.where(fwd, ri, ci)
    ids = jnp.where(xr == 0, GLA_DIAG, 0)
    for b in GLA_SUB_LEVELS:
        ids = jnp.where(((xr & (-b)) == b) & ((later & b) != 0), b, ids)
    return ids


def _gla_kernel(q_ref, k_ref, v_ref, la_ref, o_ref, st_ref, w3_ref, ids_ref):
    d = pl.program_id(0)
    s = pl.program_id(1)
    fwd = d == 0
    c = GLA_CHUNK
    tm = q_ref.shape[0]
    n_chunks = tm // c
    n_pairs = N_GLA_HEADS // 2
    nt = (((1,), (1,)), ((), ()))
    tn = (((0,), (0,)), ((), ()))

    @pl.when(s == 0)
    def _():
        st_ref[...] = jnp.zeros_like(st_ref)
        w = _gla_level_matrix(fwd).astype(BF16)
        w3_ref[...] = jnp.concatenate([w, w, w], axis=1)
        ids_ref[...] = _gla_level_ids(fwd, tm)

    g = la_ref[0]
    g = jnp.concatenate([g[t * c:(t + 1) * c] for t in range(n_chunks)], axis=1)
    g_hi = g.astype(BF16)
    r1 = g - g_hi.astype(F32)
    g_mid = r1.astype(BF16)
    g_lo = (r1 - g_mid.astype(F32)).astype(BF16)
    e_all = jnp.exp(jnp.dot(w3_ref[...], jnp.concatenate([g_hi, g_mid, g_lo], axis=0),
                            preferred_element_type=F32))

    def e_block(blk, p):
        return jnp.concatenate(
            [e_all[blk * c:(blk + 1) * c, t * D_GLA_QK + p * LANES:t * D_GLA_QK + (p + 1) * LANES]
             for t in range(n_chunks)], axis=0)

    lane = lax.broadcasted_iota(jnp.int32, (tm, LANES), 1)
    lane_v = lax.broadcasted_iota(jnp.int32, (GLA_DV, LANES), 1)
    ids = ids_ref[...]
    ids2 = jnp.concatenate([ids, ids], axis=0)
    vb = v_ref[...].astype(BF16)

    for p in range(n_pairs):
        ls = slice(p * LANES, (p + 1) * LANES)
        qp, kp = q_ref[:, ls], k_ref[:, ls]
        qm = (jnp.where(lane < GLA_DK, qp, 0.0), jnp.where(lane >= GLA_DK, qp, 0.0))

        att = jnp.where(ids2 == GLA_DIAG,
                        lax.dot_general(jnp.concatenate(qm, axis=0).astype(BF16), kp.astype(BF16), nt,
                                        preferred_element_type=F32), 0.0)
        for li, b in enumerate(GLA_SUB_LEVELS):
            eb = e_block(2 + li, p)
            lhs = jnp.concatenate([qm[0] * eb, qm[1] * eb], axis=0).astype(BF16)
            part = lax.dot_general(lhs, (kp * eb).astype(BF16), nt, preferred_element_type=F32)
            att = jnp.where(ids2 == b, part, att)
        att = att.astype(BF16)

        eq, ek = e_block(0, p), e_block(1, p)
        k_inter = (kp * ek).astype(BF16)
        decay, upd = [], []
        for t in range(n_chunks):
            rows = slice(t * c, (t + 1) * c)
            e_q = e_all[0:c, t * D_GLA_QK + p * LANES:t * D_GLA_QK + (p + 1) * LANES]
            decay.append(jnp.where(fwd, e_q[c - 1:c], e_q[0:1]))
            u = lax.dot_general(vb[rows, 2 * p * GLA_DV:(2 * p + 2) * GLA_DV], k_inter[rows], tn,
                                preferred_element_type=F32)
            upd.append(jnp.where(lane_v < GLA_DK, u[:GLA_DV], u[GLA_DV:]))
        st = st_ref[p]
        before = []
        for t in range(n_chunks):
            r = n_chunks - 1 - t
            before.append(st)
            st = jnp.where(fwd, decay[t], decay[r]) * st + jnp.where(fwd, upd[t], upd[r])
        st_ref[p] = st

        for t in range(n_chunks):
            rows = slice(t * c, (t + 1) * c)
            s_t = jnp.where(fwd, before[t], before[n_chunks - 1 - t]).astype(BF16)
            lhs = jnp.concatenate([qm[0][rows] * eq[rows], qm[1][rows] * eq[rows]], axis=0).astype(BF16)
            o_inter = lax.dot_general(lhs, s_t, nt, preferred_element_type=F32)
            for hh in range(2):
                h = 2 * p + hh
                o_intra = jnp.dot(att[hh * tm + t * c:hh * tm + (t + 1) * c],
                                  vb[:, h * GLA_DV:(h + 1) * GLA_DV], preferred_element_type=F32)
                o_ref[0, rows, h * GLA_DV:(h + 1) * GLA_DV] = o_inter[hh * c:(hh + 1) * c] + o_intra


def _gla(gq, gk, gv, la, *, n_lat):
    n_all = gq.shape[0]
    tm = ROW_TILE
    nl = n_lat // tm
    nb = n_all // tm
    nc = nb - nl

    def blk(d, s):
        ctx = jnp.where(d == 0, nl + s, nb - 1 - s)
        lat = jnp.where(d == 0, s - nc, nb - 1 - s)
        return jnp.where(s < nc, ctx, lat)

    row = lambda d, s: (blk(d, s), 0)
    return pl.pallas_call(
        _gla_kernel,
        grid=(2, nb),
        in_specs=[pl.BlockSpec((tm, D_GLA_QK), row),
                  pl.BlockSpec((tm, D_GLA_QK), row),
                  pl.BlockSpec((tm, D_GLA_OUT), row),
                  pl.BlockSpec((1, tm, D_GLA_QK), lambda d, s: (d, blk(d, s), 0))],
        out_specs=pl.BlockSpec((1, tm, D_GLA_OUT), lambda d, s: (d, blk(d, s), 0)),
        out_shape=jax.ShapeDtypeStruct((2, n_all, D_GLA_OUT), F32),
        scratch_shapes=[pltpu.VMEM((N_GLA_HEADS // 2, GLA_DV, 2 * GLA_DK), F32),
                        pltpu.VMEM(((2 + len(GLA_SUB_LEVELS)) * GLA_CHUNK, 3 * GLA_CHUNK), BF16),
                        pltpu.VMEM((tm, tm), jnp.int32)],
        compiler_params=_params("arbitrary", "arbitrary"),
        name="gla_scan",
    )(gq, gk, gv, la)


def _mixout_ffn_kernel(h_ref, mlat_ref, og_ref, gg_ref, gn_ref, wmix_ref, mod_ref, gpost1_ref,
                       gpre2_ref, gpost2_ref, win_ref, wout_ref, o_ref):
    o = og_ref[0] + og_ref[1]
    gate = _silu(gg_ref[...])
    parts = []
    for h in range(N_GLA_HEADS):
        sl = slice(h * GLA_DV, (h + 1) * GLA_DV)
        parts.append((_rms(o[:, sl], gn_ref[...]) * gate[:, sl]).astype(BF16))
    tn = (((0,), (0,)), ((), ()))
    y = lax.dot_general(mlat_ref[...], wmix_ref[:D_MLA_OUT, :], tn, preferred_element_type=F32)
    y = y + jnp.dot(jnp.concatenate(parts, axis=-1), wmix_ref[D_MLA_OUT:, :], preferred_element_type=F32)
    m = mod_ref[0]
    x = h_ref[...] + m[5:6] * _rms(y, gpost1_ref[...])
    o_ref[...] = _ffn_tile(x, m, 6, gpre2_ref[...], gpost2_ref[...], win_ref, wout_ref)


def _mixout_ffn(h_lat, mla_t, og, gg, g_norm, w_mix, mod, g_post1, g_pre2, g_post2, w_in, w_out, *, n_lat):
    tm = FUSED_TILE
    row = lambda i: (i, 0)
    const = lambda i: (0, 0)
    vec = pl.BlockSpec((1, D_MODEL), const)
    return pl.pallas_call(
        _mixout_ffn_kernel,
        grid=(n_lat // tm,),
        in_specs=[pl.BlockSpec((tm, D_MODEL), row),
                  pl.BlockSpec((D_MLA_OUT, tm), lambda i: (0, i)),
                  pl.BlockSpec((2, tm, D_GLA_OUT), lambda i: (0, i, 0)),
                  pl.BlockSpec((tm, D_GLA_OUT), row),
                  pl.BlockSpec((1, GLA_DV), const),
                  pl.BlockSpec(w_mix.shape, const),
                  pl.BlockSpec((1, N_MOD, D_MODEL), lambda i: (0, 0, 0)),
                  vec, vec, vec,
                  pl.BlockSpec((D_MODEL, 2 * D_FF), const),
                  pl.BlockSpec((D_FF, D_MODEL), const)],
        out_specs=pl.BlockSpec((tm, D_MODEL), row),
        out_shape=jax.ShapeDtypeStruct((n_lat, D_MODEL), F32),
        compiler_params=_params("arbitrary"),
        name="mixer_out_ffn",
    )(h_lat, mla_t, og, gg, g_norm, w_mix, mod, g_post1, g_pre2, g_post2, w_in, w_out)


def _rope_half_swap(w):
    half = MLA_ROPE_DIM // 4
    g = w.reshape(w.shape[:-1] + (2, 2, half))
    return g[..., ::-1, :].reshape(w.shape)


def _layout_w_in(w):
    d = w.shape[0]
    o_kv = MLA_Q_RANK
    o_kr = o_kv + MLA_KV_RANK
    o_gq = o_kr + MLA_ROPE_DIM
    o_gk = o_gq + D_GLA_QK
    o_gv = o_gk + D_GLA_QK
    o_gg = o_gv + D_GLA_OUT
    o_ga = o_gg + D_GLA_OUT
    k_rope = w[:, o_kr:o_gq]
    z = lambda n: jnp.zeros((d, n), w.dtype)
    pad = LANES - ROPE_LANE0 - MLA_ROPE_DIM
    return jnp.concatenate([
        w[:, :o_kr], w[:, o_gq:o_ga],
        w[:, o_ga:o_ga + 2 * GLA_GATE_RANK], z(ROPE_LANE0 - 2 * GLA_GATE_RANK), k_rope, z(pad),
        z(ROPE_LANE0), _rope_half_swap(k_rope), z(pad)], axis=1)


def _layout_w_q(w):
    r = w.shape[0]
    w = w.reshape(r, N_MLA_HEADS, MLA_QK_DIM)
    nope, rope = w[..., :MLA_NOPE_DIM], w[..., MLA_NOPE_DIM:]
    zp = jnp.zeros((r, N_MLA_HEADS, HEAD_PAD - MLA_QK_DIM), w.dtype)
    zn = jnp.zeros_like(nope)
    plain = jnp.concatenate([nope, rope, zp], axis=-1).reshape(r, -1)
    swapped = jnp.concatenate([zn, _rope_half_swap(rope), zp], axis=-1).reshape(r, -1)
    return jnp.concatenate([plain, swapped], axis=1)


def _layout_w_kv(w):
    r = w.shape[0]
    w = w.reshape(r, N_MLA_HEADS, MLA_NOPE_DIM + MLA_V_DIM)
    k_nope, v = w[..., :MLA_NOPE_DIM], w[..., MLA_NOPE_DIM:]
    zk = jnp.zeros((r, N_MLA_HEADS, HEAD_PAD - MLA_NOPE_DIM), w.dtype)
    zv = jnp.zeros((r, N_MLA_HEADS, HEAD_PAD - MLA_V_DIM), w.dtype)
    return jnp.concatenate([jnp.concatenate([k_nope, zk], axis=-1).reshape(r, -1),
                            jnp.concatenate([v, zv], axis=-1).reshape(r, -1)], axis=1)


def _layout_w_a(w_f, w_b):
    z = jnp.zeros((LANES, 2 * D_GLA_QK), w_f.dtype)
    z = z.at[:GLA_GATE_RANK, :D_GLA_QK].set(w_f)
    return z.at[GLA_GATE_RANK:2 * GLA_GATE_RANK, D_GLA_QK:].set(w_b)


def _attn_key_block(n_all):
    for bk in (640, 512, 256):
        if n_all % bk == 0:
            return bk
    raise ValueError(f"unsupported key count {n_all}")


def kernel(x, c, ctx, c_ctx, w_ada, b_ada, norm_pre, norm_post, ffn1_w_in, ffn1_w_out, ffn2_w_in, ffn2_w_out, w_in, mla_q_norm, mla_w_qb, mla_kv_norm, mla_w_kvb, gla_w_a_fwd, gla_b_a_fwd, gla_w_a_bwd, gla_b_a_bwd, gla_norm, w_out):
    assert x.shape[0] == 1 and ctx.shape[0] == 1 and w_ada.shape[0] == 1
    n_lat, n_ctx = x.shape[1], ctx.shape[1]
    assert n_lat % ROW_TILE == 0 and n_ctx % ROW_TILE == 0 and n_lat % GRID_W == 0
    n_all = n_lat + n_ctx
    row2 = lambda a: a.reshape(1, -1)

    ct = jnp.stack([c[0], c_ctx], axis=1)
    mod = _ada(ct, w_ada[0], row2(b_ada[0])).reshape(2, N_MOD, D_MODEL)

    h_lat, h_ctx = _ffn1(x[0], ctx[0], mod, row2(norm_pre[0, 0]), row2(norm_post[0, 0]),
                  ffn1_w_in[0].astype(BF16), ffn1_w_out[0].astype(BF16))

    q, k, v, gq, gk, gv, gg, la = _mixin(
        h_lat, h_ctx, mod, row2(norm_pre[0, 1]), _layout_w_in(w_in[0]).astype(BF16),
        row2(mla_q_norm[0]), _layout_w_q(mla_w_qb[0]).astype(BF16),
        row2(mla_kv_norm[0]), _layout_w_kv(mla_w_kvb[0]).astype(BF16),
        _layout_w_a(gla_w_a_fwd[0], gla_w_a_bwd[0]).astype(BF16),
        jnp.concatenate([gla_b_a_fwd[0], gla_b_a_bwd[0]]).reshape(1, -1), n_lat=n_lat)

    mla = _attn(q, k, v, n_lat=n_lat, bq=ATTN_Q_TILE, bk=_attn_key_block(n_all))
    og = _gla(gq, gk, gv, la, n_lat=n_lat)

    out = _mixout_ffn(h_lat, mla, og, gg, row2(gla_norm[0]), w_out[0].astype(BF16), mod,
                      row2(norm_post[0, 1]), row2(norm_pre[0, 2]), row2(norm_post[0, 2]),
                      ffn2_w_in[0].astype(BF16), ffn2_w_out[0].astype(BF16), n_lat=n_lat)
    return out[None]
```

```python
import functools
import math

import jax
import jax.numpy as jnp
from jax import lax
from jax.experimental import pallas as pl
from jax.experimental.pallas import tpu as pltpu

F32 = jnp.float32
BF16 = jnp.bfloat16

D_MODEL = 1024
GRID_W = 64
N_MLA_HEADS = 8
MLA_Q_RANK = 256
MLA_KV_RANK = 128
MLA_NOPE_DIM = 64
MLA_ROPE_DIM = 32
MLA_QK_DIM = MLA_NOPE_DIM + MLA_ROPE_DIM
MLA_V_DIM = 64
N_GLA_HEADS = 4
GLA_DK = 64
GLA_DV = 128
GLA_GATE_RANK = 16
GLA_GATE_NORM = 16.0
GLA_CHUNK = 64
D_FF = 2816
MACARON_WEIGHT = 0.5
ROPE_BASE = 10000.0
EPS = 1e-6
N_MOD = 9

D_MLA_OUT = N_MLA_HEADS * MLA_V_DIM
D_GLA_OUT = N_GLA_HEADS * GLA_DV
D_GLA_QK = N_GLA_HEADS * GLA_DK

LANES = 128
HEAD_PAD = 128
VT_ROWS = 128
ROW_TILE = 256
FUSED_TILE = 512
ATTN_Q_TILE = 512
ATTN_UNROLL = 4
VMEM_LIMIT = 56 * 1024 * 1024
C_QA = 0
C_KV = C_QA + MLA_Q_RANK
C_GQ = C_KV + MLA_KV_RANK
C_GK = C_GQ + D_GLA_QK
C_GV = C_GK + D_GLA_QK
C_GG = C_GV + D_GLA_OUT
C_MISC = C_GG + D_GLA_OUT
C_KSW = C_MISC + LANES
C_TOTAL = C_KSW + LANES
ROPE_LANE0 = MLA_NOPE_DIM
GLA_SUB_LEVELS = (32, 16, 8, 4, 2, 1)
GLA_DIAG = 64


def _rms(x, g):
    return x * lax.rsqrt(jnp.mean(x * x, axis=-1, keepdims=True) + EPS) * g


def _silu(x):
    return x * jax.nn.sigmoid(x)


def _params(*sem):
    return pltpu.CompilerParams(dimension_semantics=sem, vmem_limit_bytes=VMEM_LIMIT)


def _ada_kernel(ct_ref, w_ref, b_ref, o_ref):
    s = _silu(ct_ref[...])
    w = w_ref[...]
    r0 = jnp.sum(s[:, 0:1] * w, axis=0, keepdims=True)
    r1 = jnp.sum(s[:, 1:2] * w, axis=0, keepdims=True)
    o_ref[...] = jnp.concatenate([r0, r1], axis=0) + b_ref[...]


def _ada(ct, w, b):
    d, n = w.shape
    tn = n // 8
    return pl.pallas_call(
        _ada_kernel,
        grid=(n // tn,),
        in_specs=[pl.BlockSpec((d, 2), lambda j: (0, 0)),
                  pl.BlockSpec((d, tn), lambda j: (0, j)),
                  pl.BlockSpec((1, tn), lambda j: (0, j))],
        out_specs=pl.BlockSpec((2, tn), lambda j: (0, j)),
        out_shape=jax.ShapeDtypeStruct((2, n), F32),
        compiler_params=_params("arbitrary"),
        name="adaln",
    )(ct, w, b)


def _ffn_kernel(x_ref, ctx_ref, mod_ref, gpre_ref, gpost_ref, win_ref, wout_ref, o_ref, octx_ref):
    args = (gpre_ref[...], gpost_ref[...], win_ref, wout_ref)
    o_ref[...] = _ffn_tile(x_ref[...], mod_ref[0], 0, *args)

    @pl.when(pl.program_id(0) == pl.num_programs(0) - 1)
    def _():
        octx_ref[...] = _ffn_tile(ctx_ref[...], mod_ref[1], 0, *args)


def _ffn_tile(x, m, mod0, g_pre, g_post, win_ref, wout_ref):
    shift, scale, gate = m[mod0:mod0 + 1], m[mod0 + 1:mod0 + 2], m[mod0 + 2:mod0 + 3]
    h = _rms(x, g_pre) * (1.0 + scale) + shift
    ab = jnp.dot(h.astype(BF16), win_ref[...], preferred_element_type=F32)
    a, b = ab[:, :D_FF], ab[:, D_FF:]
    g = (_silu(a) * b).astype(BF16)
    y = jnp.dot(g, wout_ref[...], preferred_element_type=F32)
    return x + (MACARON_WEIGHT * gate) * _rms(y, g_post)


def _mod_spec(n_lat_blocks):
    return pl.BlockSpec((1, N_MOD, D_MODEL), lambda i: (jnp.where(i < n_lat_blocks, 0, 1), 0, 0))


def _ffn1(x, ctx, mod, g_pre, g_post, w_in, w_out):
    tm = FUSED_TILE
    const = lambda i: (0, 0)
    return pl.pallas_call(
        _ffn_kernel,
        grid=(x.shape[0] // tm,),
        in_specs=[pl.BlockSpec((tm, D_MODEL), lambda i: (i, 0)),
                  pl.BlockSpec(ctx.shape, const),
                  pl.BlockSpec(mod.shape, lambda i: (0, 0, 0)),
                  pl.BlockSpec((1, D_MODEL), const),
                  pl.BlockSpec((1, D_MODEL), const),
                  pl.BlockSpec((D_MODEL, 2 * D_FF), const),
                  pl.BlockSpec((D_FF, D_MODEL), const)],
        out_specs=(pl.BlockSpec((tm, D_MODEL), lambda i: (i, 0)), pl.BlockSpec(ctx.shape, const)),
        out_shape=(jax.ShapeDtypeStruct(x.shape, F32), jax.ShapeDtypeStruct(ctx.shape, F32)),
        compiler_params=_params("arbitrary"),
        name="ffn",
    )(x, ctx, mod, g_pre, g_post, w_in, w_out)


def _rope_tables(i, tm, is_lat):
    n_rows = tm // GRID_W
    assert tm % GRID_W == 0 and n_rows <= 8
    lane = lax.broadcasted_iota(jnp.int32, (GRID_W, LANES), 1)
    sub = lax.broadcasted_iota(jnp.int32, (GRID_W, LANES), 0)
    e = lane - ROPE_LANE0
    in_rope = (e >= 0) & (e < MLA_ROPE_DIM)
    axis_dim = MLA_ROPE_DIM // 2
    row_axis = (e >> 4) == 0
    w = e & (axis_dim - 1)
    j = w & (axis_dim // 2 - 1)
    first_half = w < axis_dim // 2
    inv_freq = jnp.exp(j.astype(F32) * (-math.log(ROPE_BASE) * 2.0 / axis_dim))
    col_ang = sub.astype(F32) * inv_freq
    row_ang = (i * n_rows + sub[:8]).astype(F32) * inv_freq[:8]
    sign = jnp.where(first_half, -1.0, 1.0)

    def table(fn, off_value, scale):
        col_t, row_t = fn(col_ang), fn(row_ang)
        blocks = [jnp.where(row_axis, jnp.broadcast_to(row_t[r:r + 1], (GRID_W, LANES)), col_t)
                  for r in range(n_rows)]
        blocks = [jnp.where(in_rope, jnp.where(is_lat, b, off_value) * scale, 0.0) for b in blocks]
        return jnp.concatenate(blocks, axis=0)

    cos_k = table(jnp.cos, 1.0, 1.0)
    sin_s = table(jnp.sin, 0.0, sign)
    lane_t = lax.broadcasted_iota(jnp.int32, (tm, LANES), 1)
    cos_q = jnp.where(lane_t < ROPE_LANE0, 1.0, cos_k)
    return cos_q, cos_k, sin_s


def _mixin_kernel(h_ref, hctx_ref, mod_ref, gpre_ref, win_ref, qn_ref, wq_ref, kvn_ref, wkv_ref, wa_ref, ba_ref,
                  q_ref, k_ref, v_ref, gq_ref, gk_ref, gv_ref, gg_ref, la_ref, *, n_lat_blocks):
    i = pl.program_id(0)
    tm = h_ref.shape[0]
    m = mod_ref[0]
    shift, scale = m[3:4], m[4:5]
    h = jnp.where(i < n_lat_blocks, h_ref[...], hctx_ref[...])
    a = _rms(h, gpre_ref[...]) * (1.0 + scale) + shift
    z = jnp.dot(a.astype(BF16), win_ref[...], preferred_element_type=F32)

    cos_q, cos_k, sin_s = _rope_tables(i, tm, i < n_lat_blocks)

    qn = _rms(z[:, C_QA:C_QA + MLA_Q_RANK], qn_ref[...]).astype(BF16)
    qq = jnp.dot(qn, wq_ref[...], preferred_element_type=F32)
    hw = N_MLA_HEADS * HEAD_PAD
    sm_scale = MLA_QK_DIM ** -0.5 * math.log2(math.e)
    for h in range(N_MLA_HEADS):
        lo = h * HEAD_PAD
        qh = qq[:, lo:lo + HEAD_PAD] * cos_q + qq[:, hw + lo:hw + lo + HEAD_PAD] * sin_s
        q_ref[h] = (qh * sm_scale).T.astype(BF16)

    kvn = _rms(z[:, C_KV:C_KV + MLA_KV_RANK], kvn_ref[...]).astype(BF16)
    kv = jnp.dot(kvn, wkv_ref[...], preferred_element_type=F32)
    misc = z[:, C_MISC:C_MISC + LANES]
    k_rope = misc * cos_k + z[:, C_KSW:C_KSW + LANES] * sin_s
    lane = lax.broadcasted_iota(jnp.int32, (tm, HEAD_PAD), 1)
    ones_col = (lane == MLA_V_DIM).astype(F32)
    for h in range(N_MLA_HEADS):
        lo = h * HEAD_PAD
        k_ref[h] = (kv[:, lo:lo + HEAD_PAD] + k_rope).astype(BF16)
        v_ref[h] = (kv[:, hw + lo:hw + lo + HEAD_PAD] + ones_col).T[:VT_ROWS].astype(BF16)

    gq_ref[...] = z[:, C_GQ:C_GQ + D_GLA_QK] * (GLA_DK ** -0.5)
    gk_ref[...] = z[:, C_GK:C_GK + D_GLA_QK]
    gv_ref[...] = z[:, C_GV:C_GV + D_GLA_OUT]
    gg_ref[...] = z[:, C_GG:C_GG + D_GLA_OUT]
    xg = jnp.dot(misc.astype(BF16), wa_ref[...], preferred_element_type=F32) + ba_ref[...]
    la = (jnp.minimum(xg, 0.0) - jnp.log1p(jnp.exp(-jnp.abs(xg)))) * (1.0 / GLA_GATE_NORM)
    la_ref[0] = la[:, :D_GLA_QK]
    la_ref[1] = la[:, D_GLA_QK:]


def _mixin(h_lat, h_ctx, mod, g_pre, w_in, q_norm, w_q, kv_norm, w_kv, w_a, b_a, *, n_lat):
    n_all = h_lat.shape[0] + h_ctx.shape[0]
    tm = ROW_TILE
    nl = n_lat // tm
    row = lambda i: (i, 0)
    const = lambda i: (0, 0)
    hrow = lambda i: (0, i, 0)
    full = lambda arr: pl.BlockSpec(arr.shape, const)
    out_shape = (
        jax.ShapeDtypeStruct((N_MLA_HEADS, HEAD_PAD, n_all), BF16),
        jax.ShapeDtypeStruct((N_MLA_HEADS, n_all, HEAD_PAD), BF16),
        jax.ShapeDtypeStruct((N_MLA_HEADS, VT_ROWS, n_all), BF16),
        jax.ShapeDtypeStruct((n_all, D_GLA_QK), F32),
        jax.ShapeDtypeStruct((n_all, D_GLA_QK), F32),
        jax.ShapeDtypeStruct((n_all, D_GLA_OUT), F32),
        jax.ShapeDtypeStruct((n_all, D_GLA_OUT), F32),
        jax.ShapeDtypeStruct((2, n_all, D_GLA_QK), F32),
    )
    out_specs = (
        pl.BlockSpec((N_MLA_HEADS, HEAD_PAD, tm), lambda i: (0, 0, i)),
        pl.BlockSpec((N_MLA_HEADS, tm, HEAD_PAD), hrow),
        pl.BlockSpec((N_MLA_HEADS, VT_ROWS, tm), lambda i: (0, 0, i)),
        pl.BlockSpec((tm, D_GLA_QK), row),
        pl.BlockSpec((tm, D_GLA_QK), row),
        pl.BlockSpec((tm, D_GLA_OUT), row),
        pl.BlockSpec((tm, D_GLA_OUT), row),
        pl.BlockSpec((2, tm, D_GLA_QK), hrow),
    )
    return pl.pallas_call(
        functools.partial(_mixin_kernel, n_lat_blocks=n_lat // tm),
        grid=(n_all // tm,),
        in_specs=[pl.BlockSpec((tm, D_MODEL), lambda i: (jnp.minimum(i, nl - 1), 0)),
                  pl.BlockSpec((tm, D_MODEL), lambda i: (jnp.maximum(i - nl, 0), 0)),
                  _mod_spec(nl), full(g_pre), full(w_in), full(q_norm), full(w_q), full(kv_norm), full(w_kv),
                  full(w_a), full(b_a)],
        out_specs=out_specs,
        out_shape=out_shape,
        compiler_params=_params("arbitrary"),
        name="mixer_in",
    )(h_lat, h_ctx, mod, g_pre, w_in, q_norm, w_q, kv_norm, w_kv, w_a, b_a)


def _attn_kernel(q_ref, k_ref, vt_ref, o_ref, sa_ref, sb_ref, *, bk):
    n_keys = k_ref.shape[1]
    bq = q_ref.shape[2]
    qs = [q_ref[h] for h in range(2)]

    n_trips = n_keys // bk

    def scores(h, c):
        off = pl.multiple_of(c * bk, bk)
        return jnp.dot(k_ref[h, pl.ds(off, bk), :], qs[h], preferred_element_type=F32)

    def fill(s_ref, c):
        mx = []
        for h in range(2):
            st = scores(h, c)
            s_ref[h] = st
            mx.append(jnp.max(st, axis=0, keepdims=True))
        return tuple(mx)

    def update(carry, s_ref, mx, c):
        off = pl.multiple_of(c * bk, bk)
        new = []
        for h in range(2):
            m, acc = carry[h]
            m_new = jnp.maximum(m, mx[h])
            pt = jnp.exp2(s_ref[h] - m_new).astype(BF16)
            acc = jnp.exp2(m - m_new) * acc + jnp.dot(vt_ref[h, :, pl.ds(off, bk)], pt,
                                                      preferred_element_type=F32)
            new.append((m_new, acc))
        return tuple(new)

    bufs = (sa_ref, sb_ref)

    def run(carry, mx, first, count, fill_after_last):
        for u in range(count):
            nxt = None
            if u < count - 1 or fill_after_last:
                nxt = fill(bufs[(u + 1) % 2], first + u + 1)
            carry = update(carry, bufs[u % 2], mx, first + u)
            mx = nxt
        return carry, mx

    def group(i, state):
        return run(*state, ATTN_UNROLL * i, ATTN_UNROLL, True)

    carry = tuple((jnp.full((1, bq), -jnp.inf, F32), jnp.zeros((VT_ROWS, bq), F32)) for _ in range(2))
    n_groups = (n_trips - 1) // ATTN_UNROLL
    state = lax.fori_loop(0, n_groups, group, (carry, fill(sa_ref, 0)))
    carry, _ = run(*state, ATTN_UNROLL * n_groups, n_trips - ATTN_UNROLL * n_groups, False)
    outs = [acc[:MLA_V_DIM] / acc[MLA_V_DIM:MLA_V_DIM + 1] for _, acc in carry]
    o_ref[...] = jnp.concatenate(outs, axis=0).astype(o_ref.dtype)


def _attn(q, k, vt, *, n_lat, bq, bk):
    n_all = k.shape[1]
    return pl.pallas_call(
        functools.partial(_attn_kernel, bk=bk),
        grid=(N_MLA_HEADS // 2, n_lat // bq),
        in_specs=[pl.BlockSpec((2, HEAD_PAD, bq), lambda hp, i: (hp, 0, i)),
                  pl.BlockSpec((2, n_all, HEAD_PAD), lambda hp, i: (hp, 0, 0)),
                  pl.BlockSpec((2, VT_ROWS, n_all), lambda hp, i: (hp, 0, 0))],
        out_specs=pl.BlockSpec((2 * MLA_V_DIM, bq), lambda hp, i: (hp, i)),
        out_shape=jax.ShapeDtypeStruct((D_MLA_OUT, n_lat), BF16),
        scratch_shapes=[pltpu.VMEM((2, bk, bq), F32), pltpu.VMEM((2, bk, bq), F32)],
        compiler_params=_params("arbitrary", "arbitrary"),
        name="mla_attention",
    )(q, k, vt)


def _gla_level_matrix(fwd):
    c = GLA_CHUNK
    ri = lax.broadcasted_iota(jnp.int32, (c, c), 0)
    ci = lax.broadcasted_iota(jnp.int32, (c, c), 1)

    def sides(b):
        bs = ri & (~(b - 1))
        be = bs + (b - 1)
        q_f = ((ci >= bs) & (ci <= ri)).astype(F32)
        k_f = ((ci > ri) & (ci <= be)).astype(F32)
        q_b = ((ci >= ri) & (ci <= be)).astype(F32)
        k_b = ((ci < ri) & (ci >= bs)).astype(F32)
        return q_f, k_f, q_b, k_b

    q_f, k_f, q_b, k_b = sides(c)
    blocks = [jnp.where(fwd, q_f, q_b), jnp.where(fwd, k_f, k_b)]
    for b in GLA_SUB_LEVELS:
        q_f, k_f, q_b, k_b = sides(b)
        bit = (ri & b) != 0
        blocks.append(jnp.where(fwd, jnp.where(bit, q_f, k_f), jnp.where(bit, k_b, q_b)))
    return jnp.concatenate(blocks, axis=0)


def _gla_level_ids(fwd, n):
    ri = lax.broadcasted_iota(jnp.int32, (n, n), 0)
    ci = lax.broadcasted_iota(jnp.int32, (n, n), 1)
    xr = ri ^ ci
    later = jnp.where(fwd, ri, ci)
    ids = jnp.where(xr == 0, GLA_DIAG, 0)
    for b in GLA_SUB_LEVELS:
        ids = jnp.where(((xr & (-b)) == b) & ((later & b) != 0), b, ids)
    return ids


def _gla_kernel(q_ref, k_ref, v_ref, la_ref, o_ref, st_ref, w3_ref, ids_ref):
    d = pl.program_id(0)
    s = pl.program_id(1)
    fwd = d == 0
    c = GLA_CHUNK
    tm = q_ref.shape[0]
    n_chunks = tm // c
    n_pairs = N_GLA_HEADS // 2
    nt = (((1,), (1,)), ((), ()))
    tn = (((0,), (0,)), ((), ()))

    @pl.when(s == 0)
    def _():
        st_ref[...] = jnp.zeros_like(st_ref)
        w = _gla_level_matrix(fwd).astype(BF16)
        w3_ref[...] = jnp.concatenate([w, w, w], axis=1)
        ids_ref[...] = _gla_level_ids(fwd, tm)

    g = la_ref[0]
    g = jnp.concatenate([g[t * c:(t + 1) * c] for t in range(n_chunks)], axis=1)
    g_hi = g.astype(BF16)
    r1 = g - g_hi.astype(F32)
    g_mid = r1.astype(BF16)
    g_lo = (r1 - g_mid.astype(F32)).astype(BF16)
    e_all = jnp.exp(jnp.dot(w3_ref[...], jnp.concatenate([g_hi, g_mid, g_lo], axis=0),
                            preferred_element_type=F32))

    def e_block(blk, p):
        return jnp.concatenate(
            [e_all[blk * c:(blk + 1) * c, t * D_GLA_QK + p * LANES:t * D_GLA_QK + (p + 1) * LANES]
             for t in range(n_chunks)], axis=0)

    lane = lax.broadcasted_iota(jnp.int32, (tm, LANES), 1)
    lane_v = lax.broadcasted_iota(jnp.int32, (GLA_DV, LANES), 1)
    ids = ids_ref[...]
    ids2 = jnp.concatenate([ids, ids], axis=0)
    vb = v_ref[...].astype(BF16)

    for p in range(n_pairs):
        ls = slice(p * LANES, (p + 1) * LANES)
        qp, kp = q_ref[:, ls], k_ref[:, ls]
        qm = (jnp.where(lane < GLA_DK, qp, 0.0), jnp.where(lane >= GLA_DK, qp, 0.0))

        att = jnp.where(ids2 == GLA_DIAG,
                        lax.dot_general(jnp.concatenate(qm, axis=0).astype(BF16), kp.astype(BF16), nt,
                                        preferred_element_type=F32), 0.0)
        for li, b in enumerate(GLA_SUB_LEVELS):
            eb = e_block(2 + li, p)
            lhs = jnp.concatenate([qm[0] * eb, qm[1] * eb], axis=0).astype(BF16)
            part = lax.dot_general(lhs, (kp * eb).astype(BF16), nt, preferred_element_type=F32)
            att = jnp.where(ids2 == b, part, att)
        att = att.astype(BF16)

        eq, ek = e_block(0, p), e_block(1, p)
        k_inter = (kp * ek).astype(BF16)
        decay, upd = [], []
        for t in range(n_chunks):
            rows = slice(t * c, (t + 1) * c)
            e_q = e_all[0:c, t * D_GLA_QK + p * LANES:t * D_GLA_QK + (p + 1) * LANES]
            decay.append(jnp.where(fwd, e_q[c - 1:c], e_q[0:1]))
            u = lax.dot_general(vb[rows, 2 * p * GLA_DV:(2 * p + 2) * GLA_DV], k_inter[rows], tn,
                                preferred_element_type=F32)
            upd.append(jnp.where(lane_v < GLA_DK, u[:GLA_DV], u[GLA_DV:]))
        st = st_ref[p]
        before = []
        for t in range(n_chunks):
            r = n_chunks - 1 - t
            before.append(st)
            st = jnp.where(fwd, decay[t], decay[r]) * st + jnp.where(fwd, upd[t], upd[r])
        st_ref[p] = st

        for t in range(n_chunks):
            rows = slice(t * c, (t + 1) * c)
            s_t = jnp.where(fwd, before[t], before[n_chunks - 1 - t]).astype(BF16)
            lhs = jnp.concatenate([qm[0][rows] * eq[rows], qm[1][rows] * eq[rows]], axis=0).astype(BF16)
            o_inter = lax.dot_general(lhs, s_t, nt, preferred_element_type=F32)
            for hh in range(2):
                h = 2 * p + hh
                o_intra = jnp.dot(att[hh * tm + t * c:hh * tm + (t + 1) * c],
                                  vb[:, h * GLA_DV:(h + 1) * GLA_DV], preferred_element_type=F32)
                o_ref[0, rows, h * GLA_DV:(h + 1) * GLA_DV] = o_inter[hh * c:(hh + 1) * c] + o_intra


def _gla(gq, gk, gv, la, *, n_lat):
    n_all = gq.shape[0]
    tm = ROW_TILE
    nl = n_lat // tm
    nb = n_all // tm
    nc = nb - nl

    def blk(d, s):
        ctx = jnp.where(d == 0, nl + s, nb - 1 - s)
        lat = jnp.where(d == 0, s - nc, nb - 1 - s)
        return jnp.where(s < nc, ctx, lat)

    row = lambda d, s: (blk(d, s), 0)
    return pl.pallas_call(
        _gla_kernel,
        grid=(2, nb),
        in_specs=[pl.BlockSpec((tm, D_GLA_QK), row),
                  pl.BlockSpec((tm, D_GLA_QK), row),
                  pl.BlockSpec((tm, D_GLA_OUT), row),
                  pl.BlockSpec((1, tm, D_GLA_QK), lambda d, s: (d, blk(d, s), 0))],
        out_specs=pl.BlockSpec((1, tm, D_GLA_OUT), lambda d, s: (d, blk(d, s), 0)),
        out_shape=jax.ShapeDtypeStruct((2, n_all, D_GLA_OUT), F32),
        scratch_shapes=[pltpu.VMEM((N_GLA_HEADS // 2, GLA_DV, 2 * GLA_DK), F32),
                        pltpu.VMEM(((2 + len(GLA_SUB_LEVELS)) * GLA_CHUNK, 3 * GLA_CHUNK), BF16),
                        pltpu.VMEM((tm, tm), jnp.int32)],
        compiler_params=_params("arbitrary", "arbitrary"),
        name="gla_scan",
    )(gq, gk, gv, la)


def _mixout_ffn_kernel(h_ref, mlat_ref, og_ref, gg_ref, gn_ref, wmix_ref, mod_ref, gpost1_ref,
                       gpre2_ref, gpost2_ref, win_ref, wout_ref, o_ref):
    o = og_ref[0] + og_ref[1]
    gate = _silu(gg_ref[...])
    parts = []
    for h in range(N_GLA_HEADS):
        sl = slice(h * GLA_DV, (h + 1) * GLA_DV)
        parts.append((_rms(o[:, sl], gn_ref[...]) * gate[:, sl]).astype(BF16))
    tn = (((0,), (0,)), ((), ()))
    y = lax.dot_general(mlat_ref[...], wmix_ref[:D_MLA_OUT, :], tn, preferred_element_type=F32)
    y = y + jnp.dot(jnp.concatenate(parts, axis=-1), wmix_ref[D_MLA_OUT:, :], preferred_element_type=F32)
    m = mod_ref[0]
    x = h_ref[...] + m[5:6] * _rms(y, gpost1_ref[...])
    o_ref[...] = _ffn_tile(x, m, 6, gpre2_ref[...], gpost2_ref[...], win_ref, wout_ref)


def _mixout_ffn(h_lat, mla_t, og, gg, g_norm, w_mix, mod, g_post1, g_pre2, g_post2, w_in, w_out, *, n_lat):
    tm = FUSED_TILE
    row = lambda i: (i, 0)
    const = lambda i: (0, 0)
    vec = pl.BlockSpec((1, D_MODEL), const)
    return pl.pallas_call(
        _mixout_ffn_kernel,
        grid=(n_lat // tm,),
        in_specs=[pl.BlockSpec((tm, D_MODEL), row),
                  pl.BlockSpec((D_MLA_OUT, tm), lambda i: (0, i)),
                  pl.BlockSpec((2, tm, D_GLA_OUT), lambda i: (0, i, 0)),
                  pl.BlockSpec((tm, D_GLA_OUT), row),
                  pl.BlockSpec((1, GLA_DV), const),
                  pl.BlockSpec(w_mix.shape, const),
                  pl.BlockSpec((1, N_MOD, D_MODEL), lambda i: (0, 0, 0)),
                  vec, vec, vec,
                  pl.BlockSpec((D_MODEL, 2 * D_FF), const),
                  pl.BlockSpec((D_FF, D_MODEL), const)],
        out_specs=pl.BlockSpec((tm, D_MODEL), row),
        out_shape=jax.ShapeDtypeStruct((n_lat, D_MODEL), F32),
        compiler_params=_params("arbitrary"),
        name="mixer_out_ffn",
    )(h_lat, mla_t, og, gg, g_norm, w_mix, mod, g_post1, g_pre2, g_post2, w_in, w_out)


def _rope_half_swap(w):
    half = MLA_ROPE_DIM // 4
    g = w.reshape(w.shape[:-1] + (2, 2, half))
    return g[..., ::-1, :].reshape(w.shape)


def _layout_w_in(w):
    d = w.shape[0]
    o_kv = MLA_Q_RANK
    o_kr = o_kv + MLA_KV_RANK
    o_gq = o_kr + MLA_ROPE_DIM
    o_gk = o_gq + D_GLA_QK
    o_gv = o_gk + D_GLA_QK
    o_gg = o_gv + D_GLA_OUT
    o_ga = o_gg + D_GLA_OUT
    k_rope = w[:, o_kr:o_gq]
    z = lambda n: jnp.zeros((d, n), w.dtype)
    pad = LANES - ROPE_LANE0 - MLA_ROPE_DIM
    return jnp.concatenate([
        w[:, :o_kr], w[:, o_gq:o_ga],
        w[:, o_ga:o_ga + 2 * GLA_GATE_RANK], z(ROPE_LANE0 - 2 * GLA_GATE_RANK), k_rope, z(pad),
        z(ROPE_LANE0), _rope_half_swap(k_rope), z(pad)], axis=1)


def _layout_w_q(w):
    r = w.shape[0]
    w = w.reshape(r, N_MLA_HEADS, MLA_QK_DIM)
    nope, rope = w[..., :MLA_NOPE_DIM], w[..., MLA_NOPE_DIM:]
    zp = jnp.zeros((r, N_MLA_HEADS, HEAD_PAD - MLA_QK_DIM), w.dtype)
    zn = jnp.zeros_like(nope)
    plain = jnp.concatenate([nope, rope, zp], axis=-1).reshape(r, -1)
    swapped = jnp.concatenate([zn, _rope_half_swap(rope), zp], axis=-1).reshape(r, -1)
    return jnp.concatenate([plain, swapped], axis=1)


def _layout_w_kv(w):
    r = w.shape[0]
    w = w.reshape(r, N_MLA_HEADS, MLA_NOPE_DIM + MLA_V_DIM)
    k_nope, v = w[..., :MLA_NOPE_DIM], w[..., MLA_NOPE_DIM:]
    zk = jnp.zeros((r, N_MLA_HEADS, HEAD_PAD - MLA_NOPE_DIM), w.dtype)
    zv = jnp.zeros((r, N_MLA_HEADS, HEAD_PAD - MLA_V_DIM), w.dtype)
    return jnp.concatenate([jnp.concatenate([k_nope, zk], axis=-1).reshape(r, -1),
                            jnp.concatenate([v, zv], axis=-1).reshape(r, -1)], axis=1)


def _layout_w_a(w_f, w_b):
    z = jnp.zeros((LANES, 2 * D_GLA_QK), w_f.dtype)
    z = z.at[:GLA_GATE_RANK, :D_GLA_QK].set(w_f)
    return z.at[GLA_GATE_RANK:2 * GLA_GATE_RANK, D_GLA_QK:].set(w_b)


def _attn_key_block(n_all):
    for bk in (640, 512, 256):
        if n_all % bk == 0:
            return bk
    raise ValueError(f"unsupported key count {n_all}")


def kernel(x, c, ctx, c_ctx, w_ada, b_ada, norm_pre, norm_post, ffn1_w_in, ffn1_w_out, ffn2_w_in, ffn2_w_out, w_in, mla_q_norm, mla_w_qb, mla_kv_norm, mla_w_kvb, gla_w_a_fwd, gla_b_a_fwd, gla_w_a_bwd, gla_b_a_bwd, gla_norm, w_out):
    assert x.shape[0] == 1 and ctx.shape[0] == 1 and w_ada.shape[0] == 1
    n_lat, n_ctx = x.shape[1], ctx.shape[1]
    assert n_lat % ROW_TILE == 0 and n_ctx % ROW_TILE == 0 and n_lat % GRID_W == 0
    n_all = n_lat + n_ctx
    row2 = lambda a: a.reshape(1, -1)

    ct = jnp.stack([c[0], c_ctx], axis=1)
    mod = _ada(ct, w_ada[0], row2(b_ada[0])).reshape(2, N_MOD, D_MODEL)

    h_lat, h_ctx = _ffn1(x[0], ctx[0], mod, row2(norm_pre[0, 0]), row2(norm_post[0, 0]),
                  ffn1_w_in[0].astype(BF16), ffn1_w_out[0].astype(BF16))

    q, k, v, gq, gk, gv, gg, la = _mixin(
        h_lat, h_ctx, mod, row2(norm_pre[0, 1]), _layout_w_in(w_in[0]).astype(BF16),
        row2(mla_q_norm[0]), _layout_w_q(mla_w_qb[0]).astype(BF16),
        row2(mla_kv_norm[0]), _layout_w_kv(mla_w_kvb[0]).astype(BF16),
        _layout_w_a(gla_w_a_fwd[0], gla_w_a_bwd[0]).astype(BF16),
        jnp.concatenate([gla_b_a_fwd[0], gla_b_a_bwd[0]]).reshape(1, -1), n_lat=n_lat)

    mla = _attn(q, k, v, n_lat=n_lat, bq=ATTN_Q_TILE, bk=_attn_key_block(n_all))
    og = _gla(gq, gk, gv, la, n_lat=n_lat)

    out = _mixout_ffn(h_lat, mla, og, gg, row2(gla_norm[0]), w_out[0].astype(BF16), mod,
                      row2(norm_post[0, 1]), row2(norm_pre[0, 2]), row2(norm_post[0, 2]),
                      ffn2_w_in[0].astype(BF16), ffn2_w_out[0].astype(BF16), n_lat=n_lat)
    return out[None]
```

```python
import functools
import math

import jax
import jax.numpy as jnp
from jax import lax
from jax.experimental import pallas as pl
from jax.experimental.pallas import tpu as pltpu

F32 = jnp.float32
BF16 = jnp.bfloat16

D_MODEL = 1024
GRID_W = 64
N_MLA_HEADS = 8
MLA_Q_RANK = 256
MLA_KV_RANK = 128
MLA_NOPE_DIM = 64
MLA_ROPE_DIM = 32
MLA_QK_DIM = MLA_NOPE_DIM + MLA_ROPE_DIM
MLA_V_DIM = 64
N_GLA_HEADS = 4
GLA_DK = 64
GLA_DV = 128
GLA_GATE_RANK = 16
GLA_GATE_NORM = 16.0
GLA_CHUNK = 64
D_FF = 2816
MACARON_WEIGHT = 0.5
ROPE_BASE = 10000.0
EPS = 1e-6
N_MOD = 9

D_MLA_OUT = N_MLA_HEADS * MLA_V_DIM
D_GLA_OUT = N_GLA_HEADS * GLA_DV
D_GLA_QK = N_GLA_HEADS * GLA_DK

LANES = 128
HEAD_PAD = 128
VT_ROWS = 128
ROW_TILE = 256
FUSED_TILE = 512
ATTN_Q_TILE = 512
ATTN_UNROLL = 4
VMEM_LIMIT = 56 * 1024 * 1024
C_QA = 0
C_KV = C_QA + MLA_Q_RANK
C_GQ = C_KV + MLA_KV_RANK
C_GK = C_GQ + D_GLA_QK
C_GV = C_GK + D_GLA_QK
C_GG = C_GV + D_GLA_OUT
C_MISC = C_GG + D_GLA_OUT
C_KSW = C_MISC + LANES
C_TOTAL = C_KSW + LANES
ROPE_LANE0 = MLA_NOPE_DIM
GLA_SUB_LEVELS = (32, 16, 8, 4, 2, 1)
GLA_DIAG = 64


def _rms(x, g):
    return x * lax.rsqrt(jnp.mean(x * x, axis=-1, keepdims=True) + EPS) * g


def _silu(x):
    return x * jax.nn.sigmoid(x)


def _params(*sem):
    return pltpu.CompilerParams(dimension_semantics=sem, vmem_limit_bytes=VMEM_LIMIT)


def _ada_kernel(ct_ref, w_ref, b_ref, o_ref):
    s = _silu(ct_ref[...])
    w = w_ref[...]
    r0 = jnp.sum(s[:, 0:1] * w, axis=0, keepdims=True)
    r1 = jnp.sum(s[:, 1:2] * w, axis=0, keepdims=True)
    o_ref[...] = jnp.concatenate([r0, r1], axis=0) + b_ref[...]


def _ada(ct, w, b):
    d, n = w.shape
    tn = n // 8
    return pl.pallas_call(
        _ada_kernel,
        grid=(n // tn,),
        in_specs=[pl.BlockSpec((d, 2), lambda j: (0, 0)),
                  pl.BlockSpec((d, tn), lambda j: (0, j)),
                  pl.BlockSpec((1, tn), lambda j: (0, j))],
        out_specs=pl.BlockSpec((2, tn), lambda j: (0, j)),
        out_shape=jax.ShapeDtypeStruct((2, n), F32),
        compiler_params=_params("arbitrary"),
        name="adaln",
    )(ct, w, b)


def _ffn_kernel(x_ref, ctx_ref, mod_ref, gpre_ref, gpost_ref, win_ref, wout_ref, o_ref, octx_ref):
    args = (gpre_ref[...], gpost_ref[...], win_ref, wout_ref)
    o_ref[...] = _ffn_tile(x_ref[...], mod_ref[0], 0, *args)

    @pl.when(pl.program_id(0) == pl.num_programs(0) - 1)
    def _():
        octx_ref[...] = _ffn_tile(ctx_ref[...], mod_ref[1], 0, *args)


def _ffn_tile(x, m, mod0, g_pre, g_post, win_ref, wout_ref):
    shift, scale, gate = m[mod0:mod0 + 1], m[mod0 + 1:mod0 + 2], m[mod0 + 2:mod0 + 3]
    h = _rms(x, g_pre) * (1.0 + scale) + shift
    ab = jnp.dot(h.astype(BF16), win_ref[...], preferred_element_type=F32)
    a, b = ab[:, :D_FF], ab[:, D_FF:]
    g = (_silu(a) * b).astype(BF16)
    y = jnp.dot(g, wout_ref[...], preferred_element_type=F32)
    return x + (MACARON_WEIGHT * gate) * _rms(y, g_post)


def _mod_spec(n_lat_blocks):
    return pl.BlockSpec((1, N_MOD, D_MODEL), lambda i: (jnp.where(i < n_lat_blocks, 0, 1), 0, 0))


def _ffn1(x, ctx, mod, g_pre, g_post, w_in, w_out):
    tm = FUSED_TILE
    const = lambda i: (0, 0)
    return pl.pallas_call(
        _ffn_kernel,
        grid=(x.shape[0] // tm,),
        in_specs=[pl.BlockSpec((tm, D_MODEL), lambda i: (i, 0)),
                  pl.BlockSpec(ctx.shape, const),
                  pl.BlockSpec(mod.shape, lambda i: (0, 0, 0)),
                  pl.BlockSpec((1, D_MODEL), const),
                  pl.BlockSpec((1, D_MODEL), const),
                  pl.BlockSpec((D_MODEL, 2 * D_FF), const),
                  pl.BlockSpec((D_FF, D_MODEL), const)],
        out_specs=(pl.BlockSpec((tm, D_MODEL), lambda i: (i, 0)), pl.BlockSpec(ctx.shape, const)),
        out_shape=(jax.ShapeDtypeStruct(x.shape, F32), jax.ShapeDtypeStruct(ctx.shape, F32)),
        compiler_params=_params("arbitrary"),
        name="ffn",
    )(x, ctx, mod, g_pre, g_post, w_in, w_out)


def _rope_tables(i, tm, is_lat):
    n_rows = tm // GRID_W
    assert tm % GRID_W == 0 and n_rows <= 8
    lane = lax.broadcasted_iota(jnp.int32, (GRID_W, LANES), 1)
    sub = lax.broadcasted_iota(jnp.int32, (GRID_W, LANES), 0)
    e = lane - ROPE_LANE0
    in_rope = (e >= 0) & (e < MLA_ROPE_DIM)
    axis_dim = MLA_ROPE_DIM // 2
    row_axis = (e >> 4) == 0
    w = e & (axis_dim - 1)
    j = w & (axis_dim // 2 - 1)
    first_half = w < axis_dim // 2
    inv_freq = jnp.exp(j.astype(F32) * (-math.log(ROPE_BASE) * 2.0 / axis_dim))
    col_ang = sub.astype(F32) * inv_freq
    row_ang = (i * n_rows + sub[:8]).astype(F32) * inv_freq[:8]
    sign = jnp.where(first_half, -1.0, 1.0)

    def table(fn, off_value, scale):
        col_t, row_t = fn(col_ang), fn(row_ang)
        blocks = [jnp.where(row_axis, jnp.broadcast_to(row_t[r:r + 1], (GRID_W, LANES)), col_t)
                  for r in range(n_rows)]
        blocks = [jnp.where(in_rope, jnp.where(is_lat, b, off_value) * scale, 0.0) for b in blocks]
        return jnp.concatenate(blocks, axis=0)

    cos_k = table(jnp.cos, 1.0, 1.0)
    sin_s = table(jnp.sin, 0.0, sign)
    lane_t = lax.broadcasted_iota(jnp.int32, (tm, LANES), 1)
    cos_q = jnp.where(lane_t < ROPE_LANE0, 1.0, cos_k)
    return cos_q, cos_k, sin_s


def _mixin_kernel(h_ref, hctx_ref, mod_ref, gpre_ref, win_ref, qn_ref, wq_ref, kvn_ref, wkv_ref, wa_ref, ba_ref,
                  q_ref, k_ref, v_ref, gq_ref, gk_ref, gv_ref, gg_ref, la_ref, *, n_lat_blocks):
    i = pl.program_id(0)
    tm = h_ref.shape[0]
    m = mod_ref[0]
    shift, scale = m[3:4], m[4:5]
    h = jnp.where(i < n_lat_blocks, h_ref[...], hctx_ref[...])
    a = _rms(h, gpre_ref[...]) * (1.0 + scale) + shift
    z = jnp.dot(a.astype(BF16), win_ref[...], preferred_element_type=F32)

    cos_q, cos_k, sin_s = _rope_tables(i, tm, i < n_lat_blocks)

    qn = _rms(z[:, C_QA:C_QA + MLA_Q_RANK], qn_ref[...]).astype(BF16)
    qq = jnp.dot(qn, wq_ref[...], preferred_element_type=F32)
    hw = N_MLA_HEADS * HEAD_PAD
    sm_scale = MLA_QK_DIM ** -0.5 * math.log2(math.e)
    for h in range(N_MLA_HEADS):
        lo = h * HEAD_PAD
        qh = qq[:, lo:lo + HEAD_PAD] * cos_q + qq[:, hw + lo:hw + lo + HEAD_PAD] * sin_s
        q_ref[h] = (qh * sm_scale).astype(BF16).T

    kvn = _rms(z[:, C_KV:C_KV + MLA_KV_RANK], kvn_ref[...]).astype(BF16)
    kv = jnp.dot(kvn, wkv_ref[...], preferred_element_type=F32)
    misc = z[:, C_MISC:C_MISC + LANES]
    k_rope = misc * cos_k + z[:, C_KSW:C_KSW + LANES] * sin_s
    lane = lax.broadcasted_iota(jnp.int32, (tm, HEAD_PAD), 1)
    sub = lax.broadcasted_iota(jnp.int32, (VT_ROWS - MLA_V_DIM, tm), 0)
    ones_rows = (sub == 0).astype(BF16)
    for h in range(N_MLA_HEADS):
        kvh = kv[:, h * HEAD_PAD:(h + 1) * HEAD_PAD]
        k_ref[h] = jnp.where(lane < MLA_NOPE_DIM, kvh, k_rope).astype(BF16)
        v_ref[h, :MLA_V_DIM, :] = kvh.astype(BF16).T[MLA_NOPE_DIM:]
        v_ref[h, MLA_V_DIM:, :] = ones_rows

    gq_ref[...] = z[:, C_GQ:C_GQ + D_GLA_QK] * (GLA_DK ** -0.5)
    gk_ref[...] = z[:, C_GK:C_GK + D_GLA_QK]
    gv_ref[...] = z[:, C_GV:C_GV + D_GLA_OUT]
    gg_ref[...] = z[:, C_GG:C_GG + D_GLA_OUT]
    xg = jnp.dot(misc.astype(BF16), wa_ref[...], preferred_element_type=F32) + ba_ref[...]
    la = (jnp.minimum(xg, 0.0) - jnp.log1p(jnp.exp(-jnp.abs(xg)))) * (1.0 / GLA_GATE_NORM)
    la_ref[0] = la[:, :D_GLA_QK]
    la_ref[1] = la[:, D_GLA_QK:]


def _mixin(h_lat, h_ctx, mod, g_pre, w_in, q_norm, w_q, kv_norm, w_kv, w_a, b_a, *, n_lat):
    n_all = h_lat.shape[0] + h_ctx.shape[0]
    tm = ROW_TILE
    nl = n_lat // tm
    row = lambda i: (i, 0)
    const = lambda i: (0, 0)
    hrow = lambda i: (0, i, 0)
    full = lambda arr: pl.BlockSpec(arr.shape, const)
    out_shape = (
        jax.ShapeDtypeStruct((N_MLA_HEADS, HEAD_PAD, n_all), BF16),
        jax.ShapeDtypeStruct((N_MLA_HEADS, n_all, HEAD_PAD), BF16),
        jax.ShapeDtypeStruct((N_MLA_HEADS, VT_ROWS, n_all), BF16),
        jax.ShapeDtypeStruct((n_all, D_GLA_QK), F32),
        jax.ShapeDtypeStruct((n_all, D_GLA_QK), F32),
        jax.ShapeDtypeStruct((n_all, D_GLA_OUT), F32),
        jax.ShapeDtypeStruct((n_all, D_GLA_OUT), F32),
        jax.ShapeDtypeStruct((2, n_all, D_GLA_QK), F32),
    )
    out_specs = (
        pl.BlockSpec((N_MLA_HEADS, HEAD_PAD, tm), lambda i: (0, 0, i)),
        pl.BlockSpec((N_MLA_HEADS, tm, HEAD_PAD), hrow),
        pl.BlockSpec((N_MLA_HEADS, VT_ROWS, tm), lambda i: (0, 0, i)),
        pl.BlockSpec((tm, D_GLA_QK), row),
        pl.BlockSpec((tm, D_GLA_QK), row),
        pl.BlockSpec((tm, D_GLA_OUT), row),
        pl.BlockSpec((tm, D_GLA_OUT), row),
        pl.BlockSpec((2, tm, D_GLA_QK), hrow),
    )
    return pl.pallas_call(
        functools.partial(_mixin_kernel, n_lat_blocks=n_lat // tm),
        grid=(n_all // tm,),
        in_specs=[pl.BlockSpec((tm, D_MODEL), lambda i: (jnp.minimum(i, nl - 1), 0)),
                  pl.BlockSpec((tm, D_MODEL), lambda i: (jnp.maximum(i - nl, 0), 0)),
                  _mod_spec(nl), full(g_pre), full(w_in), full(q_norm), full(w_q), full(kv_norm), full(w_kv),
                  full(w_a), full(b_a)],
        out_specs=out_specs,
        out_shape=out_shape,
        compiler_params=_params("arbitrary"),
        name="mixer_in",
    )(h_lat, h_ctx, mod, g_pre, w_in, q_norm, w_q, kv_norm, w_kv, w_a, b_a)


def _attn_kernel(q_ref, k_ref, vt_ref, o_ref, sa_ref, sb_ref, *, bk):
    n_keys = k_ref.shape[1]
    bq = q_ref.shape[2]
    qs = [q_ref[h] for h in range(2)]

    n_trips = n_keys // bk

    def scores(h, c):
        off = pl.multiple_of(c * bk, bk)
        return jnp.dot(k_ref[h, pl.ds(off, bk), :], qs[h], preferred_element_type=F32)

    def fill(s_ref, c):
        mx = []
        for h in range(2):
            st = scores(h, c)
            s_ref[h] = st
            mx.append(jnp.max(st, axis=0, keepdims=True))
        return tuple(mx)

    def update(carry, s_ref, mx, c):
        off = pl.multiple_of(c * bk, bk)
        new = []
        for h in range(2):
            m, acc = carry[h]
            m_new = jnp.maximum(m, mx[h])
            pt = jnp.exp2(s_ref[h] - m_new).astype(BF16)
            acc = jnp.exp2(m - m_new) * acc + jnp.dot(vt_ref[h, :, pl.ds(off, bk)], pt,
                                                      preferred_element_type=F32)
            new.append((m_new, acc))
        return tuple(new)

    bufs = (sa_ref, sb_ref)

    def run(carry, mx, first, count, fill_after_last):
        for u in range(count):
            nxt = None
            if u < count - 1 or fill_after_last:
                nxt = fill(bufs[(u + 1) % 2], first + u + 1)
            carry = update(carry, bufs[u % 2], mx, first + u)
            mx = nxt
        return carry, mx

    def group(i, state):
        return run(*state, ATTN_UNROLL * i, ATTN_UNROLL, True)

    carry = tuple((jnp.full((1, bq), -jnp.inf, F32), jnp.zeros((VT_ROWS, bq), F32)) for _ in range(2))
    n_groups = (n_trips - 1) // ATTN_UNROLL
    state = lax.fori_loop(0, n_groups, group, (carry, fill(sa_ref, 0)))
    carry, _ = run(*state, ATTN_UNROLL * n_groups, n_trips - ATTN_UNROLL * n_groups, False)
    outs = [acc[:MLA_V_DIM] / acc[MLA_V_DIM:MLA_V_DIM + 1] for _, acc in carry]
    o_ref[...] = jnp.concatenate(outs, axis=0).astype(o_ref.dtype)


def _attn(q, k, vt, *, n_lat, bq, bk):
    n_all = k.shape[1]
    return pl.pallas_call(
        functools.partial(_attn_kernel, bk=bk),
        grid=(N_MLA_HEADS // 2, n_lat // bq),
        in_specs=[pl.BlockSpec((2, HEAD_PAD, bq), lambda hp, i: (hp, 0, i)),
                  pl.BlockSpec((2, n_all, HEAD_PAD), lambda hp, i: (hp, 0, 0)),
                  pl.BlockSpec((2, VT_ROWS, n_all), lambda hp, i: (hp, 0, 0))],
        out_specs=pl.BlockSpec((2 * MLA_V_DIM, bq), lambda hp, i: (hp, i)),
        out_shape=jax.ShapeDtypeStruct((D_MLA_OUT, n_lat), BF16),
        scratch_shapes=[pltpu.VMEM((2, bk, bq), F32), pltpu.VMEM((2, bk, bq), F32)],
        compiler_params=_params("arbitrary", "arbitrary"),
        name="mla_attention",
    )(q, k, vt)


def _gla_level_matrix(fwd):
    c = GLA_CHUNK
    ri = lax.broadcasted_iota(jnp.int32, (c, c), 0)
    ci = lax.broadcasted_iota(jnp.int32, (c, c), 1)

    def sides(b):
        bs = ri & (~(b - 1))
        be = bs + (b - 1)
        q_f = ((ci >= bs) & (ci <= ri)).astype(F32)
        k_f = ((ci > ri) & (ci <= be)).astype(F32)
        q_b = ((ci >= ri) & (ci <= be)).astype(F32)
        k_b = ((ci < ri) & (ci >= bs)).astype(F32)
        return q_f, k_f, q_b, k_b

    q_f, k_f, q_b, k_b = sides(c)
    blocks = [jnp.where(fwd, q_f, q_b), jnp.where(fwd, k_f, k_b)]
    for b in GLA_SUB_LEVELS:
        q_f, k_f, q_b, k_b = sides(b)
        bit = (ri & b) != 0
        blocks.append(jnp.where(fwd, jnp.where(bit, q_f, k_f), jnp.where(bit, k_b, q_b)))
    return jnp.concatenate(blocks, axis=0)


def _gla_level_ids(fwd, n):
    ri = lax.broadcasted_iota(jnp.int32, (n, n), 0)
    ci = lax.broadcasted_iota(jnp.int32, (n, n), 1)
    xr = ri ^ ci
    later = jnp.where(fwd, ri, ci)
    ids = jnp.where(xr == 0, GLA_DIAG, 0)
    for b in GLA_SUB_LEVELS:
        ids = jnp.where(((xr & (-b)) == b) & ((later & b) != 0), b, ids)
    return ids


def _gla_kernel(q_ref, k_ref, v_ref, la_ref, o_ref, st_ref, w3_ref, ids_ref):
    d = pl.program_id(0)
    s = pl.program_id(1)
    fwd = d == 0
    c = GLA_CHUNK
    tm = q_ref.shape[0]
    n_chunks = tm // c
    n_pairs = N_GLA_HEADS // 2
    nt = (((1,), (1,)), ((), ()))
    tn = (((0,), (0,)), ((), ()))

    @pl.when(s == 0)
    def _():
        st_ref[...] = jnp.zeros_like(st_ref)
        w = _gla_level_matrix(fwd).astype(BF16)
        w3_ref[...] = jnp.concatenate([w, w, w], axis=1)
        ids_ref[...] = _gla_level_ids(fwd, tm)

    g = la_ref[0]
    g = jnp.concatenate([g[t * c:(t + 1) * c] for t in range(n_chunks)], axis=1)
    g_hi = g.astype(BF16)
    r1 = g - g_hi.astype(F32)
    g_mid = r1.astype(BF16)
    g_lo = (r1 - g_mid.astype(F32)).astype(BF16)
    e_all = jnp.exp(jnp.dot(w3_ref[...], jnp.concatenate([g_hi, g_mid, g_lo], axis=0),
                            preferred_element_type=F32))

    def e_block(blk, p):
        return jnp.concatenate(
            [e_all[blk * c:(blk + 1) * c, t * D_GLA_QK + p * LANES:t * D_GLA_QK + (p + 1) * LANES]
             for t in range(n_chunks)], axis=0)

    lane = lax.broadcasted_iota(jnp.int32, (tm, LANES), 1)
    lane_v = lax.broadcasted_iota(jnp.int32, (GLA_DV, LANES), 1)
    ids = ids_ref[...]
    ids2 = jnp.concatenate([ids, ids], axis=0)
    vb = v_ref[...].astype(BF16)

    for p in range(n_pairs):
        ls = slice(p * LANES, (p + 1) * LANES)
        qp, kp = q_ref[:, ls], k_ref[:, ls]
        qm = (jnp.where(lane < GLA_DK, qp, 0.0), jnp.where(lane >= GLA_DK, qp, 0.0))

        att = jnp.where(ids2 == GLA_DIAG,
                        lax.dot_general(jnp.concatenate(qm, axis=0).astype(BF16), kp.astype(BF16), nt,
                                        preferred_element_type=F32), 0.0)
        for li, b in enumerate(GLA_SUB_LEVELS):
            eb = e_block(2 + li, p)
            lhs = jnp.concatenate([qm[0] * eb, qm[1] * eb], axis=0).astype(BF16)
            part = lax.dot_general(lhs, (kp * eb).astype(BF16), nt, preferred_element_type=F32)
            att = jnp.where(ids2 == b, part, att)
        att = att.astype(BF16)

        eq, ek = e_block(0, p), e_block(1, p)
        k_inter = (kp * ek).astype(BF16)
        decay, upd = [], []
        for t in range(n_chunks):
            rows = slice(t * c, (t + 1) * c)
            e_q = e_all[0:c, t * D_GLA_QK + p * LANES:t * D_GLA_QK + (p + 1) * LANES]
            decay.append(jnp.where(fwd, e_q[c - 1:c], e_q[0:1]))
            u = lax.dot_general(vb[rows, 2 * p * GLA_DV:(2 * p + 2) * GLA_DV], k_inter[rows], tn,
                                preferred_element_type=F32)
            upd.append(jnp.where(lane_v < GLA_DK, u[:GLA_DV], u[GLA_DV:]))
        st = st_ref[p]
        before = []
        for t in range(n_chunks):
            r = n_chunks - 1 - t
            before.append(st)
            st = jnp.where(fwd, decay[t], decay[r]) * st + jnp.where(fwd, upd[t], upd[r])
        st_ref[p] = st

        for t in range(n_chunks):
            rows = slice(t * c, (t + 1) * c)
            s_t = jnp.where(fwd, before[t], before[n_chunks - 1 - t]).astype(BF16)
            lhs = jnp.concatenate([qm[0][rows] * eq[rows], qm[1][rows] * eq[rows]], axis=0).astype(BF16)
            o_inter = lax.dot_general(lhs, s_t, nt, preferred_element_type=F32)
            for hh in range(2):
                h = 2 * p + hh
                o_intra = jnp.dot(att[hh * tm + t * c:hh * tm + (t + 1) * c],
                                  vb[:, h * GLA_DV:(h + 1) * GLA_DV], preferred_element_type=F32)
                o_ref[0, rows, h * GLA_DV:(h + 1) * GLA_DV] = o_inter[hh * c:(hh + 1) * c] + o_intra


def _gla(gq, gk, gv, la, *, n_lat):
    n_all = gq.shape[0]
    tm = ROW_TILE
    nl = n_lat // tm
    nb = n_all // tm
    nc = nb - nl

    def blk(d, s):
        ctx = jnp.where(d == 0, nl + s, nb - 1 - s)
        lat = jnp.where(d == 0, s - nc, nb - 1 - s)
        return jnp.where(s < nc, ctx, lat)

    row = lambda d, s: (blk(d, s), 0)
    return pl.pallas_call(
        _gla_kernel,
        grid=(2, nb),
        in_specs=[pl.BlockSpec((tm, D_GLA_QK), row),
                  pl.BlockSpec((tm, D_GLA_QK), row),
                  pl.BlockSpec((tm, D_GLA_OUT), row),
                  pl.BlockSpec((1, tm, D_GLA_QK), lambda d, s: (d, blk(d, s), 0))],
        out_specs=pl.BlockSpec((1, tm, D_GLA_OUT), lambda d, s: (d, blk(d, s), 0)),
        out_shape=jax.ShapeDtypeStruct((2, n_all, D_GLA_OUT), F32),
        scratch_shapes=[pltpu.VMEM((N_GLA_HEADS // 2, GLA_DV, 2 * GLA_DK), F32),
                        pltpu.VMEM(((2 + len(GLA_SUB_LEVELS)) * GLA_CHUNK, 3 * GLA_CHUNK), BF16),
                        pltpu.VMEM((tm, tm), jnp.int32)],
        compiler_params=_params("arbitrary", "arbitrary"),
        name="gla_scan",
    )(gq, gk, gv, la)


def _mixout_ffn_kernel(h_ref, mlat_ref, og_ref, gg_ref, gn_ref, wmix_ref, mod_ref, gpost1_ref,
                       gpre2_ref, gpost2_ref, win_ref, wout_ref, o_ref):
    o = og_ref[0] + og_ref[1]
    gate = _silu(gg_ref[...])
    parts = []
    for h in range(N_GLA_HEADS):
        sl = slice(h * GLA_DV, (h + 1) * GLA_DV)
        parts.append((_rms(o[:, sl], gn_ref[...]) * gate[:, sl]).astype(BF16))
    tn = (((0,), (0,)), ((), ()))
    y = lax.dot_general(mlat_ref[...], wmix_ref[:D_MLA_OUT, :], tn, preferred_element_type=F32)
    y = y + jnp.dot(jnp.concatenate(parts, axis=-1), wmix_ref[D_MLA_OUT:, :], preferred_element_type=F32)
    m = mod_ref[0]
    x = h_ref[...] + m[5:6] * _rms(y, gpost1_ref[...])
    o_ref[...] = _ffn_tile(x, m, 6, gpre2_ref[...], gpost2_ref[...], win_ref, wout_ref)


def _mixout_ffn(h_lat, mla_t, og, gg, g_norm, w_mix, mod, g_post1, g_pre2, g_post2, w_in, w_out, *, n_lat):
    tm = FUSED_TILE
    row = lambda i: (i, 0)
    const = lambda i: (0, 0)
    vec = pl.BlockSpec((1, D_MODEL), const)
    return pl.pallas_call(
        _mixout_ffn_kernel,
        grid=(n_lat // tm,),
        in_specs=[pl.BlockSpec((tm, D_MODEL), row),
                  pl.BlockSpec((D_MLA_OUT, tm), lambda i: (0, i)),
                  pl.BlockSpec((2, tm, D_GLA_OUT), lambda i: (0, i, 0)),
                  pl.BlockSpec((tm, D_GLA_OUT), row),
                  pl.BlockSpec((1, GLA_DV), const),
                  pl.BlockSpec(w_mix.shape, const),
                  pl.BlockSpec((1, N_MOD, D_MODEL), lambda i: (0, 0, 0)),
                  vec, vec, vec,
                  pl.BlockSpec((D_MODEL, 2 * D_FF), const),
                  pl.BlockSpec((D_FF, D_MODEL), const)],
        out_specs=pl.BlockSpec((tm, D_MODEL), row),
        out_shape=jax.ShapeDtypeStruct((n_lat, D_MODEL), F32),
        compiler_params=_params("arbitrary"),
        name="mixer_out_ffn",
    )(h_lat, mla_t, og, gg, g_norm, w_mix, mod, g_post1, g_pre2, g_post2, w_in, w_out)


def _rope_half_swap(w):
    half = MLA_ROPE_DIM // 4
    g = w.reshape(w.shape[:-1] + (2, 2, half))
    return g[..., ::-1, :].reshape(w.shape)


def _layout_w_in(w):
    d = w.shape[0]
    o_kv = MLA_Q_RANK
    o_kr = o_kv + MLA_KV_RANK
    o_gq = o_kr + MLA_ROPE_DIM
    o_gk = o_gq + D_GLA_QK
    o_gv = o_gk + D_GLA_QK
    o_gg = o_gv + D_GLA_OUT
    o_ga = o_gg + D_GLA_OUT
    k_rope = w[:, o_kr:o_gq]
    z = lambda n: jnp.zeros((d, n), w.dtype)
    pad = LANES - ROPE_LANE0 - MLA_ROPE_DIM
    return jnp.concatenate([
        w[:, :o_kr], w[:, o_gq:o_ga],
        w[:, o_ga:o_ga + 2 * GLA_GATE_RANK], z(ROPE_LANE0 - 2 * GLA_GATE_RANK), k_rope, z(pad),
        z(ROPE_LANE0), _rope_half_swap(k_rope), z(pad)], axis=1)


def _layout_w_q(w):
    r = w.shape[0]
    w = w.reshape(r, N_MLA_HEADS, MLA_QK_DIM)
    nope, rope = w[..., :MLA_NOPE_DIM], w[..., MLA_NOPE_DIM:]
    zp = jnp.zeros((r, N_MLA_HEADS, HEAD_PAD - MLA_QK_DIM), w.dtype)
    zn = jnp.zeros_like(nope)
    plain = jnp.concatenate([nope, rope, zp], axis=-1).reshape(r, -1)
    swapped = jnp.concatenate([zn, _rope_half_swap(rope), zp], axis=-1).reshape(r, -1)
    return jnp.concatenate([plain, swapped], axis=1)


def _layout_w_a(w_f, w_b):
    z = jnp.zeros((LANES, 2 * D_GLA_QK), w_f.dtype)
    z = z.at[:GLA_GATE_RANK, :D_GLA_QK].set(w_f)
    return z.at[GLA_GATE_RANK:2 * GLA_GATE_RANK, D_GLA_QK:].set(w_b)


def _attn_key_block(n_all):
    for bk in (640, 512, 256):
        if n_all % bk == 0:
            return bk
    raise ValueError(f"unsupported key count {n_all}")


def kernel(x, c, ctx, c_ctx, w_ada, b_ada, norm_pre, norm_post, ffn1_w_in, ffn1_w_out, ffn2_w_in, ffn2_w_out, w_in, mla_q_norm, mla_w_qb, mla_kv_norm, mla_w_kvb, gla_w_a_fwd, gla_b_a_fwd, gla_w_a_bwd, gla_b_a_bwd, gla_norm, w_out):
    assert x.shape[0] == 1 and ctx.shape[0] == 1 and w_ada.shape[0] == 1
    n_lat, n_ctx = x.shape[1], ctx.shape[1]
    assert n_lat % ROW_TILE == 0 and n_ctx % ROW_TILE == 0 and n_lat % GRID_W == 0
    n_all = n_lat + n_ctx
    row2 = lambda a: a.reshape(1, -1)

    ct = jnp.stack([c[0], c_ctx], axis=1)
    mod = _ada(ct, w_ada[0], row2(b_ada[0])).reshape(2, N_MOD, D_MODEL)

    h_lat, h_ctx = _ffn1(x[0], ctx[0], mod, row2(norm_pre[0, 0]), row2(norm_post[0, 0]),
                  ffn1_w_in[0].astype(BF16), ffn1_w_out[0].astype(BF16))

    q, k, v, gq, gk, gv, gg, la = _mixin(
        h_lat, h_ctx, mod, row2(norm_pre[0, 1]), _layout_w_in(w_in[0]).astype(BF16),
        row2(mla_q_norm[0]), _layout_w_q(mla_w_qb[0]).astype(BF16),
        row2(mla_kv_norm[0]), mla_w_kvb[0].astype(BF16),
        _layout_w_a(gla_w_a_fwd[0], gla_w_a_bwd[0]).astype(BF16),
        jnp.concatenate([gla_b_a_fwd[0], gla_b_a_bwd[0]]).reshape(1, -1), n_lat=n_lat)

    mla = _attn(q, k, v, n_lat=n_lat, bq=ATTN_Q_TILE, bk=_attn_key_block(n_all))
    og = _gla(gq, gk, gv, la, n_lat=n_lat)

    out = _mixout_ffn(h_lat, mla, og, gg, row2(gla_norm[0]), w_out[0].astype(BF16), mod,
                      row2(norm_post[0, 1]), row2(norm_pre[0, 2]), row2(norm_post[0, 2]),
                      ffn2_w_in[0].astype(BF16), ffn2_w_out[0].astype(BF16), n_lat=n_lat)
    return out[None]
```

```python
import functools
import math

import jax
import jax.numpy as jnp
from jax import lax
from jax.experimental import pallas as pl
from jax.experimental.pallas import tpu as pltpu

F32 = jnp.float32
BF16 = jnp.bfloat16

D_MODEL = 1024
GRID_W = 64
N_MLA_HEADS = 8
MLA_Q_RANK = 256
MLA_KV_RANK = 128
MLA_NOPE_DIM = 64
MLA_ROPE_DIM = 32
MLA_QK_DIM = MLA_NOPE_DIM + MLA_ROPE_DIM
MLA_V_DIM = 64
N_GLA_HEADS = 4
GLA_DK = 64
GLA_DV = 128
GLA_GATE_RANK = 16
GLA_GATE_NORM = 16.0
GLA_CHUNK = 64
D_FF = 2816
MACARON_WEIGHT = 0.5
ROPE_BASE = 10000.0
EPS = 1e-6
N_MOD = 9

D_MLA_OUT = N_MLA_HEADS * MLA_V_DIM
D_GLA_OUT = N_GLA_HEADS * GLA_DV
D_GLA_QK = N_GLA_HEADS * GLA_DK

LANES = 128
HEAD_PAD = 128
VT_ROWS = 128
ROW_TILE = 256
FUSED_TILE = 512
ATTN_Q_TILE = 512
ATTN_UNROLL = 8
VMEM_LIMIT = 56 * 1024 * 1024
C_QA = 0
C_KV = C_QA + MLA_Q_RANK
C_GQ = C_KV + MLA_KV_RANK
C_GK = C_GQ + D_GLA_QK
C_GV = C_GK + D_GLA_QK
C_GG = C_GV + D_GLA_OUT
C_MISC = C_GG + D_GLA_OUT
C_KSW = C_MISC + LANES
C_TOTAL = C_KSW + LANES
ROPE_LANE0 = MLA_NOPE_DIM
GLA_SUB_LEVELS = (32, 16, 8, 4, 2, 1)
GLA_DIAG = 64


def _rms(x, g):
    return x * lax.rsqrt(jnp.mean(x * x, axis=-1, keepdims=True) + EPS) * g


def _silu(x):
    return x * jax.nn.sigmoid(x)


def _params(*sem):
    return pltpu.CompilerParams(dimension_semantics=sem, vmem_limit_bytes=VMEM_LIMIT)


def _ada_kernel(ct_ref, w_ref, b_ref, o_ref):
    s = _silu(ct_ref[...])
    w = w_ref[...]
    r0 = jnp.sum(s[:, 0:1] * w, axis=0, keepdims=True)
    r1 = jnp.sum(s[:, 1:2] * w, axis=0, keepdims=True)
    o_ref[...] = jnp.concatenate([r0, r1], axis=0) + b_ref[...]


def _ada(ct, w, b):
    d, n = w.shape
    tn = n // 8
    return pl.pallas_call(
        _ada_kernel,
        grid=(n // tn,),
        in_specs=[pl.BlockSpec((d, 2), lambda j: (0, 0)),
                  pl.BlockSpec((d, tn), lambda j: (0, j)),
                  pl.BlockSpec((1, tn), lambda j: (0, j))],
        out_specs=pl.BlockSpec((2, tn), lambda j: (0, j)),
        out_shape=jax.ShapeDtypeStruct((2, n), F32),
        compiler_params=_params("arbitrary"),
        name="adaln",
    )(ct, w, b)


def _ffn_kernel(x_ref, ctx_ref, mod_ref, gpre_ref, gpost_ref, win_ref, wout_ref, o_ref, octx_ref):
    args = (gpre_ref[...], gpost_ref[...], win_ref, wout_ref)
    o_ref[...] = _ffn_tile(x_ref[...], mod_ref[0], 0, *args)

    @pl.when(pl.program_id(0) == pl.num_programs(0) - 1)
    def _():
        octx_ref[...] = _ffn_tile(ctx_ref[...], mod_ref[1], 0, *args)


def _ffn_tile(x, m, mod0, g_pre, g_post, win_ref, wout_ref):
    shift, scale, gate = m[mod0:mod0 + 1], m[mod0 + 1:mod0 + 2], m[mod0 + 2:mod0 + 3]
    h = _rms(x, g_pre) * (1.0 + scale) + shift
    ab = jnp.dot(h.astype(BF16), win_ref[...], preferred_element_type=F32)
    a, b = ab[:, :D_FF], ab[:, D_FF:]
    g = (_silu(a) * b).astype(BF16)
    y = jnp.dot(g, wout_ref[...], preferred_element_type=F32)
    return x + (MACARON_WEIGHT * gate) * _rms(y, g_post)


def _mod_spec(n_lat_blocks):
    return pl.BlockSpec((1, N_MOD, D_MODEL), lambda i: (jnp.where(i < n_lat_blocks, 0, 1), 0, 0))


def _ffn1(x, ctx, mod, g_pre, g_post, w_in, w_out):
    tm = FUSED_TILE
    const = lambda i: (0, 0)
    return pl.pallas_call(
        _ffn_kernel,
        grid=(x.shape[0] // tm,),
        in_specs=[pl.BlockSpec((tm, D_MODEL), lambda i: (i, 0)),
                  pl.BlockSpec(ctx.shape, const),
                  pl.BlockSpec(mod.shape, lambda i: (0, 0, 0)),
                  pl.BlockSpec((1, D_MODEL), const),
                  pl.BlockSpec((1, D_MODEL), const),
                  pl.BlockSpec((D_MODEL, 2 * D_FF), const),
                  pl.BlockSpec((D_FF, D_MODEL), const)],
        out_specs=(pl.BlockSpec((tm, D_MODEL), lambda i: (i, 0)), pl.BlockSpec(ctx.shape, const)),
        out_shape=(jax.ShapeDtypeStruct(x.shape, F32), jax.ShapeDtypeStruct(ctx.shape, F32)),
        compiler_params=_params("arbitrary"),
        name="ffn",
    )(x, ctx, mod, g_pre, g_post, w_in, w_out)


def _rope_tables(i, tm, is_lat):
    n_rows = tm // GRID_W
    assert tm % GRID_W == 0 and n_rows <= 8
    lane = lax.broadcasted_iota(jnp.int32, (GRID_W, LANES), 1)
    sub = lax.broadcasted_iota(jnp.int32, (GRID_W, LANES), 0)
    e = lane - ROPE_LANE0
    in_rope = (e >= 0) & (e < MLA_ROPE_DIM)
    axis_dim = MLA_ROPE_DIM // 2
    row_axis = (e >> 4) == 0
    w = e & (axis_dim - 1)
    j = w & (axis_dim // 2 - 1)
    first_half = w < axis_dim // 2
    inv_freq = jnp.exp(j.astype(F32) * (-math.log(ROPE_BASE) * 2.0 / axis_dim))
    col_ang = sub.astype(F32) * inv_freq
    row_ang = (i * n_rows + sub[:8]).astype(F32) * inv_freq[:8]
    sign = jnp.where(first_half, -1.0, 1.0)

    def table(fn, off_value, scale):
        col_t, row_t = fn(col_ang), fn(row_ang)
        blocks = [jnp.where(row_axis, jnp.broadcast_to(row_t[r:r + 1], (GRID_W, LANES)), col_t)
                  for r in range(n_rows)]
        blocks = [jnp.where(in_rope, jnp.where(is_lat, b, off_value) * scale, 0.0) for b in blocks]
        return jnp.concatenate(blocks, axis=0)

    cos_k = table(jnp.cos, 1.0, 1.0)
    sin_s = table(jnp.sin, 0.0, sign)
    lane_t = lax.broadcasted_iota(jnp.int32, (tm, LANES), 1)
    cos_q = jnp.where(lane_t < ROPE_LANE0, 1.0, cos_k)
    return cos_q, cos_k, sin_s


def _mixin_kernel(h_ref, hctx_ref, mod_ref, gpre_ref, win_ref, qn_ref, wq_ref, kvn_ref, wkv_ref, wa_ref, ba_ref,
                  q_ref, k_ref, v_ref, gq_ref, gk_ref, gv_ref, gg_ref, la_ref, *, n_lat_blocks):
    i = pl.program_id(0)
    tm = h_ref.shape[0]
    m = mod_ref[0]
    shift, scale = m[3:4], m[4:5]
    h = jnp.where(i < n_lat_blocks, h_ref[...], hctx_ref[...])
    a = _rms(h, gpre_ref[...]) * (1.0 + scale) + shift
    z = jnp.dot(a.astype(BF16), win_ref[...], preferred_element_type=F32)

    cos_q, cos_k, sin_s = _rope_tables(i, tm, i < n_lat_blocks)

    qn = _rms(z[:, C_QA:C_QA + MLA_Q_RANK], qn_ref[...]).astype(BF16)
    qq = jnp.dot(qn, wq_ref[...], preferred_element_type=F32)
    hw = N_MLA_HEADS * HEAD_PAD
    sm_scale = MLA_QK_DIM ** -0.5 * math.log2(math.e)
    for h in range(N_MLA_HEADS):
        lo = h * HEAD_PAD
        qh = qq[:, lo:lo + HEAD_PAD] * cos_q + qq[:, hw + lo:hw + lo + HEAD_PAD] * sin_s
        q_ref[h] = (qh * sm_scale).astype(BF16).T

    kvn = _rms(z[:, C_KV:C_KV + MLA_KV_RANK], kvn_ref[...]).astype(BF16)
    kv = jnp.dot(kvn, wkv_ref[...], preferred_element_type=F32)
    misc = z[:, C_MISC:C_MISC + LANES]
    k_rope = misc * cos_k + z[:, C_KSW:C_KSW + LANES] * sin_s
    lane = lax.broadcasted_iota(jnp.int32, (tm, HEAD_PAD), 1)
    sub = lax.broadcasted_iota(jnp.int32, (VT_ROWS - MLA_V_DIM, tm), 0)
    ones_rows = (sub == 0).astype(BF16)
    for h in range(N_MLA_HEADS):
        kvh = kv[:, h * HEAD_PAD:(h + 1) * HEAD_PAD]
        k_ref[h] = jnp.where(lane < MLA_NOPE_DIM, kvh, k_rope).astype(BF16)
        v_ref[h, :MLA_V_DIM, :] = kvh.astype(BF16).T[MLA_NOPE_DIM:]
        v_ref[h, MLA_V_DIM:, :] = ones_rows

    gq_ref[...] = z[:, C_GQ:C_GQ + D_GLA_QK] * (GLA_DK ** -0.5)
    gk_ref[...] = z[:, C_GK:C_GK + D_GLA_QK]
    gv_ref[...] = z[:, C_GV:C_GV + D_GLA_OUT]
    gg_ref[...] = z[:, C_GG:C_GG + D_GLA_OUT]
    xg = jnp.dot(misc.astype(BF16), wa_ref[...], preferred_element_type=F32) + ba_ref[...]
    la = (jnp.minimum(xg, 0.0) - jnp.log1p(jnp.exp(-jnp.abs(xg)))) * (1.0 / GLA_GATE_NORM)
    la_ref[0] = la[:, :D_GLA_QK]
    la_ref[1] = la[:, D_GLA_QK:]


def _mixin(h_lat, h_ctx, mod, g_pre, w_in, q_norm, w_q, kv_norm, w_kv, w_a, b_a, *, n_lat):
    n_all = h_lat.shape[0] + h_ctx.shape[0]
    tm = ROW_TILE
    nl = n_lat // tm
    row = lambda i: (i, 0)
    const = lambda i: (0, 0)
    hrow = lambda i: (0, i, 0)
    full = lambda arr: pl.BlockSpec(arr.shape, const)
    out_shape = (
        jax.ShapeDtypeStruct((N_MLA_HEADS, HEAD_PAD, n_all), BF16),
        jax.ShapeDtypeStruct((N_MLA_HEADS, n_all, HEAD_PAD), BF16),
        jax.ShapeDtypeStruct((N_MLA_HEADS, VT_ROWS, n_all), BF16),
        jax.ShapeDtypeStruct((n_all, D_GLA_QK), F32),
        jax.ShapeDtypeStruct((n_all, D_GLA_QK), F32),
        jax.ShapeDtypeStruct((n_all, D_GLA_OUT), F32),
        jax.ShapeDtypeStruct((n_all, D_GLA_OUT), F32),
        jax.ShapeDtypeStruct((2, n_all, D_GLA_QK), F32),
    )
    out_specs = (
        pl.BlockSpec((N_MLA_HEADS, HEAD_PAD, tm), lambda i: (0, 0, i)),
        pl.BlockSpec((N_MLA_HEADS, tm, HEAD_PAD), hrow),
        pl.BlockSpec((N_MLA_HEADS, VT_ROWS, tm), lambda i: (0, 0, i)),
        pl.BlockSpec((tm, D_GLA_QK), row),
        pl.BlockSpec((tm, D_GLA_QK), row),
        pl.BlockSpec((tm, D_GLA_OUT), row),
        pl.BlockSpec((tm, D_GLA_OUT), row),
        pl.BlockSpec((2, tm, D_GLA_QK), hrow),
    )
    return pl.pallas_call(
        functools.partial(_mixin_kernel, n_lat_blocks=n_lat // tm),
        grid=(n_all // tm,),
        in_specs=[pl.BlockSpec((tm, D_MODEL), lambda i: (jnp.minimum(i, nl - 1), 0)),
                  pl.BlockSpec((tm, D_MODEL), lambda i: (jnp.maximum(i - nl, 0), 0)),
                  _mod_spec(nl), full(g_pre), full(w_in), full(q_norm), full(w_q), full(kv_norm), full(w_kv),
                  full(w_a), full(b_a)],
        out_specs=out_specs,
        out_shape=out_shape,
        compiler_params=_params("arbitrary"),
        name="mixer_in",
    )(h_lat, h_ctx, mod, g_pre, w_in, q_norm, w_q, kv_norm, w_kv, w_a, b_a)


def _attn_kernel(q_ref, k_ref, vt_ref, o_ref, sa_ref, sb_ref, *, bk):
    n_keys = k_ref.shape[1]
    bq = q_ref.shape[2]
    qs = [q_ref[h] for h in range(2)]

    n_trips = n_keys // bk

    def scores(h, c):
        off = pl.multiple_of(c * bk, bk)
        return jnp.dot(k_ref[h, pl.ds(off, bk), :], qs[h], preferred_element_type=F32)

    def fill(s_ref, h, c):
        st = scores(h, c)
        s_ref[h] = st
        return jnp.max(st, axis=0, keepdims=True)

    def update(carry, s_ref, mx, h, c):
        off = pl.multiple_of(c * bk, bk)
        m, acc = carry
        m_new = jnp.maximum(m, mx)
        pt = jnp.exp2(s_ref[h] - m_new).astype(BF16)
        acc = jnp.exp2(m - m_new) * acc + jnp.dot(vt_ref[h, :, pl.ds(off, bk)], pt,
                                                  preferred_element_type=F32)
        return m_new, acc

    bufs = (sa_ref, sb_ref)

    def run(carry, mx, first, count, fill_after_last):
        carry, mx = list(carry), list(mx)
        for u in range(count):
            for h in range(2):
                nxt = None
                if u < count - 1 or fill_after_last:
                    nxt = fill(bufs[(u + 1) % 2], h, first + u + 1)
                carry[h] = update(carry[h], bufs[u % 2], mx[h], h, first + u)
                mx[h] = nxt
        if not fill_after_last:
            mx = [jnp.zeros((1, bq), F32)] * 2
        return tuple(carry), tuple(mx)

    def group(i, state):
        return run(*state, ATTN_UNROLL * i, ATTN_UNROLL, True)

    carry = tuple((jnp.full((1, bq), -jnp.inf, F32), jnp.zeros((VT_ROWS, bq), F32)) for _ in range(2))
    n_groups = (n_trips - 1) // ATTN_UNROLL
    state = lax.fori_loop(0, n_groups, group, (carry, tuple(fill(sa_ref, h, 0) for h in range(2))))
    carry, _ = run(*state, ATTN_UNROLL * n_groups, n_trips - ATTN_UNROLL * n_groups, False)
    outs = [acc[:MLA_V_DIM] / acc[MLA_V_DIM:MLA_V_DIM + 1] for _, acc in carry]
    o_ref[...] = jnp.concatenate(outs, axis=0).astype(o_ref.dtype)


def _attn(q, k, vt, *, n_lat, bq, bk):
    n_all = k.shape[1]
    return pl.pallas_call(
        functools.partial(_attn_kernel, bk=bk),
        grid=(N_MLA_HEADS // 2, n_lat // bq),
        in_specs=[pl.BlockSpec((2, HEAD_PAD, bq), lambda hp, i: (hp, 0, i)),
                  pl.BlockSpec((2, n_all, HEAD_PAD), lambda hp, i: (hp, 0, 0)),
                  pl.BlockSpec((2, VT_ROWS, n_all), lambda hp, i: (hp, 0, 0))],
        out_specs=pl.BlockSpec((2 * MLA_V_DIM, bq), lambda hp, i: (hp, i)),
        out_shape=jax.ShapeDtypeStruct((D_MLA_OUT, n_lat), BF16),
        scratch_shapes=[pltpu.VMEM((2, bk, bq), F32), pltpu.VMEM((2, bk, bq), F32)],
        compiler_params=_params("arbitrary", "arbitrary"),
        name="mla_attention",
    )(q, k, vt)


def _gla_level_matrix(fwd):
    c = GLA_CHUNK
    ri = lax.broadcasted_iota(jnp.int32, (c, c), 0)
    ci = lax.broadcasted_iota(jnp.int32, (c, c), 1)

    def sides(b):
        bs = ri & (~(b - 1))
        be = bs + (b - 1)
        q_f = ((ci >= bs) & (ci <= ri)).astype(F32)
        k_f = ((ci > ri) & (ci <= be)).astype(F32)
        q_b = ((ci >= ri) & (ci <= be)).astype(F32)
        k_b = ((ci < ri) & (ci >= bs)).astype(F32)
        return q_f, k_f, q_b, k_b

    q_f, k_f, q_b, k_b = sides(c)
    blocks = [jnp.where(fwd, q_f, q_b), jnp.where(fwd, k_f, k_b)]
    for b in GLA_SUB_LEVELS:
        q_f, k_f, q_b, k_b = sides(b)
        bit = (ri & b) != 0
        blocks.append(jnp.where(fwd, jnp.where(bit, q_f, k_f), jnp.where(bit, k_b, q_b)))
    return jnp.concatenate(blocks, axis=0)


def _gla_level_ids(fwd, n):
    ri = lax.broadcasted_iota(jnp.int32, (n, n), 0)
    ci = lax.broadcasted_iota(jnp.int32, (n, n), 1)
    xr = ri ^ ci
    later = jnp.where(fwd, ri, ci)
    ids = jnp.where(xr == 0, GLA_DIAG, 0)
    for b in GLA_SUB_LEVELS:
        ids = jnp.where(((xr & (-b)) == b) & ((later & b) != 0), b, ids)
    return ids


def _gla_kernel(q_ref, k_ref, v_ref, la_ref, o_ref, st_ref, w3_ref, ids_ref):
    d = pl.program_id(0)
    s = pl.program_id(1)
    fwd = d == 0
    c = GLA_CHUNK
    tm = q_ref.shape[0]
    n_chunks = tm // c
    n_pairs = N_GLA_HEADS // 2
    nt = (((1,), (1,)), ((), ()))
    tn = (((0,), (0,)), ((), ()))

    @pl.when(s == 0)
    def _():
        st_ref[...] = jnp.zeros_like(st_ref)
        w = _gla_level_matrix(fwd).astype(BF16)
        w3_ref[...] = jnp.concatenate([w, w, w], axis=1)
        ids_ref[...] = _gla_level_ids(fwd, tm)

    g = la_ref[0]
    g = jnp.concatenate([g[t * c:(t + 1) * c] for t in range(n_chunks)], axis=1)
    g_hi = g.astype(BF16)
    r1 = g - g_hi.astype(F32)
    g_mid = r1.astype(BF16)
    g_lo = (r1 - g_mid.astype(F32)).astype(BF16)
    e_all = jnp.exp(jnp.dot(w3_ref[...], jnp.concatenate([g_hi, g_mid, g_lo], axis=0),
                            preferred_element_type=F32))

    def e_block(blk, p):
        return jnp.concatenate(
            [e_all[blk * c:(blk + 1) * c, t * D_GLA_QK + p * LANES:t * D_GLA_QK + (p + 1) * LANES]
             for t in range(n_chunks)], axis=0)

    lane = lax.broadcasted_iota(jnp.int32, (tm, LANES), 1)
    lane_v = lax.broadcasted_iota(jnp.int32, (GLA_DV, LANES), 1)
    ids = ids_ref[...]
    ids2 = jnp.concatenate([ids, ids], axis=0)
    vb = v_ref[...].astype(BF16)

    for p in range(n_pairs):
        ls = slice(p * LANES, (p + 1) * LANES)
        qp, kp = q_ref[:, ls], k_ref[:, ls]
        qm = (jnp.where(lane < GLA_DK, qp, 0.0), jnp.where(lane >= GLA_DK, qp, 0.0))

        att = jnp.where(ids2 == GLA_DIAG,
                        lax.dot_general(jnp.concatenate(qm, axis=0).astype(BF16), kp.astype(BF16), nt,
                                        preferred_element_type=F32), 0.0)
        for li, b in enumerate(GLA_SUB_LEVELS):
            eb = e_block(2 + li, p)
            lhs = jnp.concatenate([qm[0] * eb, qm[1] * eb], axis=0).astype(BF16)
            part = lax.dot_general(lhs, (kp * eb).astype(BF16), nt, preferred_element_type=F32)
            att = jnp.where(ids2 == b, part, att)
        att = att.astype(BF16)

        eq, ek = e_block(0, p), e_block(1, p)
        k_inter = (kp * ek).astype(BF16)
        decay, upd = [], []
        for t in range(n_chunks):
            rows = slice(t * c, (t + 1) * c)
            e_q = e_all[0:c, t * D_GLA_QK + p * LANES:t * D_GLA_QK + (p + 1) * LANES]
            decay.append(jnp.where(fwd, e_q[c - 1:c], e_q[0:1]))
            u = lax.dot_general(vb[rows, 2 * p * GLA_DV:(2 * p + 2) * GLA_DV], k_inter[rows], tn,
                                preferred_element_type=F32)
            upd.append(jnp.where(lane_v < GLA_DK, u[:GLA_DV], u[GLA_DV:]))
        st = st_ref[p]
        before = []
        for t in range(n_chunks):
            r = n_chunks - 1 - t
            before.append(st)
            st = jnp.where(fwd, decay[t], decay[r]) * st + jnp.where(fwd, upd[t], upd[r])
        st_ref[p] = st

        for t in range(n_chunks):
            rows = slice(t * c, (t + 1) * c)
            s_t = jnp.where(fwd, before[t], before[n_chunks - 1 - t]).astype(BF16)
            lhs = jnp.concatenate([qm[0][rows] * eq[rows], qm[1][rows] * eq[rows]], axis=0).astype(BF16)
            o_inter = lax.dot_general(lhs, s_t, nt, preferred_element_type=F32)
            for hh in range(2):
                h = 2 * p + hh
                o_intra = jnp.dot(att[hh * tm + t * c:hh * tm + (t + 1) * c],
                                  vb[:, h * GLA_DV:(h + 1) * GLA_DV], preferred_element_type=F32)
                o_ref[0, rows, h * GLA_DV:(h + 1) * GLA_DV] = o_inter[hh * c:(hh + 1) * c] + o_intra


def _gla(gq, gk, gv, la, *, n_lat):
    n_all = gq.shape[0]
    tm = ROW_TILE
    nl = n_lat // tm
    nb = n_all // tm
    nc = nb - nl

    def blk(d, s):
        ctx = jnp.where(d == 0, nl + s, nb - 1 - s)
        lat = jnp.where(d == 0, s - nc, nb - 1 - s)
        return jnp.where(s < nc, ctx, lat)

    row = lambda d, s: (blk(d, s), 0)
    return pl.pallas_call(
        _gla_kernel,
        grid=(2, nb),
        in_specs=[pl.BlockSpec((tm, D_GLA_QK), row),
                  pl.BlockSpec((tm, D_GLA_QK), row),
                  pl.BlockSpec((tm, D_GLA_OUT), row),
                  pl.BlockSpec((1, tm, D_GLA_QK), lambda d, s: (d, blk(d, s), 0))],
        out_specs=pl.BlockSpec((1, tm, D_GLA_OUT), lambda d, s: (d, blk(d, s), 0)),
        out_shape=jax.ShapeDtypeStruct((2, n_all, D_GLA_OUT), F32),
        scratch_shapes=[pltpu.VMEM((N_GLA_HEADS // 2, GLA_DV, 2 * GLA_DK), F32),
                        pltpu.VMEM(((2 + len(GLA_SUB_LEVELS)) * GLA_CHUNK, 3 * GLA_CHUNK), BF16),
                        pltpu.VMEM((tm, tm), jnp.int32)],
        compiler_params=_params("arbitrary", "arbitrary"),
        name="gla_scan",
    )(gq, gk, gv, la)


def _mixout_ffn_kernel(h_ref, mlat_ref, og_ref, gg_ref, gn_ref, wmix_ref, mod_ref, gpost1_ref,
                       gpre2_ref, gpost2_ref, win_ref, wout_ref, o_ref):
    o = og_ref[0] + og_ref[1]
    gate = _silu(gg_ref[...])
    parts = []
    for h in range(N_GLA_HEADS):
        sl = slice(h * GLA_DV, (h + 1) * GLA_DV)
        parts.append((_rms(o[:, sl], gn_ref[...]) * gate[:, sl]).astype(BF16))
    tn = (((0,), (0,)), ((), ()))
    y = lax.dot_general(mlat_ref[...], wmix_ref[:D_MLA_OUT, :], tn, preferred_element_type=F32)
    y = y + jnp.dot(jnp.concatenate(parts, axis=-1), wmix_ref[D_MLA_OUT:, :], preferred_element_type=F32)
    m = mod_ref[0]
    x = h_ref[...] + m[5:6] * _rms(y, gpost1_ref[...])
    o_ref[...] = _ffn_tile(x, m, 6, gpre2_ref[...], gpost2_ref[...], win_ref, wout_ref)


def _mixout_ffn(h_lat, mla_t, og, gg, g_norm, w_mix, mod, g_post1, g_pre2, g_post2, w_in, w_out, *, n_lat):
    tm = FUSED_TILE
    row = lambda i: (i, 0)
    const = lambda i: (0, 0)
    vec = pl.BlockSpec((1, D_MODEL), const)
    return pl.pallas_call(
        _mixout_ffn_kernel,
        grid=(n_lat // tm,),
        in_specs=[pl.BlockSpec((tm, D_MODEL), row),
                  pl.BlockSpec((D_MLA_OUT, tm), lambda i: (0, i)),
                  pl.BlockSpec((2, tm, D_GLA_OUT), lambda i: (0, i, 0)),
                  pl.BlockSpec((tm, D_GLA_OUT), row),
                  pl.BlockSpec((1, GLA_DV), const),
                  pl.BlockSpec(w_mix.shape, const),
                  pl.BlockSpec((1, N_MOD, D_MODEL), lambda i: (0, 0, 0)),
                  vec, vec, vec,
                  pl.BlockSpec((D_MODEL, 2 * D_FF), const),
                  pl.BlockSpec((D_FF, D_MODEL), const)],
        out_specs=pl.BlockSpec((tm, D_MODEL), row),
        out_shape=jax.ShapeDtypeStruct((n_lat, D_MODEL), F32),
        compiler_params=_params("arbitrary"),
        name="mixer_out_ffn",
    )(h_lat, mla_t, og, gg, g_norm, w_mix, mod, g_post1, g_pre2, g_post2, w_in, w_out)


def _rope_half_swap(w):
    half = MLA_ROPE_DIM // 4
    g = w.reshape(w.shape[:-1] + (2, 2, half))
    return g[..., ::-1, :].reshape(w.shape)


def _layout_w_in(w):
    d = w.shape[0]
    o_kv = MLA_Q_RANK
    o_kr = o_kv + MLA_KV_RANK
    o_gq = o_kr + MLA_ROPE_DIM
    o_gk = o_gq + D_GLA_QK
    o_gv = o_gk + D_GLA_QK
    o_gg = o_gv + D_GLA_OUT
    o_ga = o_gg + D_GLA_OUT
    k_rope = w[:, o_kr:o_gq]
    z = lambda n: jnp.zeros((d, n), w.dtype)
    pad = LANES - ROPE_LANE0 - MLA_ROPE_DIM
    return jnp.concatenate([
        w[:, :o_kr], w[:, o_gq:o_ga],
        w[:, o_ga:o_ga + 2 * GLA_GATE_RANK], z(ROPE_LANE0 - 2 * GLA_GATE_RANK), k_rope, z(pad),
        z(ROPE_LANE0), _rope_half_swap(k_rope), z(pad)], axis=1)


def _layout_w_q(w):
    r = w.shape[0]
    w = w.reshape(r, N_MLA_HEADS, MLA_QK_DIM)
    nope, rope = w[..., :MLA_NOPE_DIM], w[..., MLA_NOPE_DIM:]
    zp = jnp.zeros((r, N_MLA_HEADS, HEAD_PAD - MLA_QK_DIM), w.dtype)
    zn = jnp.zeros_like(nope)
    plain = jnp.concatenate([nope, rope, zp], axis=-1).reshape(r, -1)
    swapped = jnp.concatenate([zn, _rope_half_swap(rope), zp], axis=-1).reshape(r, -1)
    return jnp.concatenate([plain, swapped], axis=1)


def _layout_w_a(w_f, w_b):
    z = jnp.zeros((LANES, 2 * D_GLA_QK), w_f.dtype)
    z = z.at[:GLA_GATE_RANK, :D_GLA_QK].set(w_f)
    return z.at[GLA_GATE_RANK:2 * GLA_GATE_RANK, D_GLA_QK:].set(w_b)


def _attn_key_block(n_all):
    for bk in (640, 512, 256):
        if n_all % bk == 0:
            return bk
    raise ValueError(f"unsupported key count {n_all}")


def kernel(x, c, ctx, c_ctx, w_ada, b_ada, norm_pre, norm_post, ffn1_w_in, ffn1_w_out, ffn2_w_in, ffn2_w_out, w_in, mla_q_norm, mla_w_qb, mla_kv_norm, mla_w_kvb, gla_w_a_fwd, gla_b_a_fwd, gla_w_a_bwd, gla_b_a_bwd, gla_norm, w_out):
    assert x.shape[0] == 1 and ctx.shape[0] == 1 and w_ada.shape[0] == 1
    n_lat, n_ctx = x.shape[1], ctx.shape[1]
    assert n_lat % ROW_TILE == 0 and n_ctx % ROW_TILE == 0 and n_lat % GRID_W == 0
    n_all = n_lat + n_ctx
    row2 = lambda a: a.reshape(1, -1)

    ct = jnp.stack([c[0], c_ctx], axis=1)
    mod = _ada(ct, w_ada[0], row2(b_ada[0])).reshape(2, N_MOD, D_MODEL)

    h_lat, h_ctx = _ffn1(x[0], ctx[0], mod, row2(norm_pre[0, 0]), row2(norm_post[0, 0]),
                  ffn1_w_in[0].astype(BF16), ffn1_w_out[0].astype(BF16))

    q, k, v, gq, gk, gv, gg, la = _mixin(
        h_lat, h_ctx, mod, row2(norm_pre[0, 1]), _layout_w_in(w_in[0]).astype(BF16),
        row2(mla_q_norm[0]), _layout_w_q(mla_w_qb[0]).astype(BF16),
        row2(mla_kv_norm[0]), mla_w_kvb[0].astype(BF16),
        _layout_w_a(gla_w_a_fwd[0], gla_w_a_bwd[0]).astype(BF16),
        jnp.concatenate([gla_b_a_fwd[0], gla_b_a_bwd[0]]).reshape(1, -1), n_lat=n_lat)

    mla = _attn(q, k, v, n_lat=n_lat, bq=ATTN_Q_TILE, bk=_attn_key_block(n_all))
    og = _gla(gq, gk, gv, la, n_lat=n_lat)

    out = _mixout_ffn(h_lat, mla, og, gg, row2(gla_norm[0]), w_out[0].astype(BF16), mod,
                      row2(norm_post[0, 1]), row2(norm_pre[0, 2]), row2(norm_post[0, 2]),
                      ffn2_w_in[0].astype(BF16), ffn2_w_out[0].astype(BF16), n_lat=n_lat)
    return out[None]
```

```python
import functools
import math

import jax
import jax.numpy as jnp
from jax import lax
from jax.experimental import pallas as pl
from jax.experimental.pallas import tpu as pltpu

F32 = jnp.float32
BF16 = jnp.bfloat16

D_MODEL = 1024
GRID_W = 64
N_MLA_HEADS = 8
MLA_Q_RANK = 256
MLA_KV_RANK = 128
MLA_NOPE_DIM = 64
MLA_ROPE_DIM = 32
MLA_QK_DIM = MLA_NOPE_DIM + MLA_ROPE_DIM
MLA_V_DIM = 64
N_GLA_HEADS = 4
GLA_DK = 64
GLA_DV = 128
GLA_GATE_RANK = 16
GLA_GATE_NORM = 16.0
GLA_CHUNK = 64
D_FF = 2816
MACARON_WEIGHT = 0.5
ROPE_BASE = 10000.0
EPS = 1e-6
N_MOD = 9

D_MLA_OUT = N_MLA_HEADS * MLA_V_DIM
D_GLA_OUT = N_GLA_HEADS * GLA_DV
D_GLA_QK = N_GLA_HEADS * GLA_DK

LANES = 128
HEAD_PAD = 128
VT_ROWS = 128
ROW_TILE = 256
FUSED_TILE = 512
ATTN_Q_TILE = 512
ATTN_UNROLL = 12
VMEM_LIMIT = 56 * 1024 * 1024
C_QA = 0
C_KV = C_QA + MLA_Q_RANK
C_GQ = C_KV + MLA_KV_RANK
C_GK = C_GQ + D_GLA_QK
C_GV = C_GK + D_GLA_QK
C_GG = C_GV + D_GLA_OUT
C_MISC = C_GG + D_GLA_OUT
C_KSW = C_MISC + LANES
C_TOTAL = C_KSW + LANES
ROPE_LANE0 = MLA_NOPE_DIM
GLA_SUB_LEVELS = (32, 16, 8, 4, 2, 1)
GLA_DIAG = 64


def _rms(x, g):
    return x * lax.rsqrt(jnp.mean(x * x, axis=-1, keepdims=True) + EPS) * g


def _silu(x):
    return x * jax.nn.sigmoid(x)


def _params(*sem):
    return pltpu.CompilerParams(dimension_semantics=sem, vmem_limit_bytes=VMEM_LIMIT)


def _ada_kernel(ct_ref, w_ref, b_ref, o_ref):
    s = _silu(ct_ref[...])
    w = w_ref[...]
    r0 = jnp.sum(s[:, 0:1] * w, axis=0, keepdims=True)
    r1 = jnp.sum(s[:, 1:2] * w, axis=0, keepdims=True)
    o_ref[...] = jnp.concatenate([r0, r1], axis=0) + b_ref[...]


def _ada(ct, w, b):
    d, n = w.shape
    tn = n // 8
    return pl.pallas_call(
        _ada_kernel,
        grid=(n // tn,),
        in_specs=[pl.BlockSpec((d, 2), lambda j: (0, 0)),
                  pl.BlockSpec((d, tn), lambda j: (0, j)),
                  pl.BlockSpec((1, tn), lambda j: (0, j))],
        out_specs=pl.BlockSpec((2, tn), lambda j: (0, j)),
        out_shape=jax.ShapeDtypeStruct((2, n), F32),
        compiler_params=_params("arbitrary"),
        name="adaln",
    )(ct, w, b)


def _ffn_kernel(x_ref, ctx_ref, mod_ref, gpre_ref, gpost_ref, win_ref, wout_ref, o_ref, octx_ref):
    args = (gpre_ref[...], gpost_ref[...], win_ref, wout_ref)
    o_ref[...] = _ffn_tile(x_ref[...], mod_ref[0], 0, *args)

    @pl.when(pl.program_id(0) == pl.num_programs(0) - 1)
    def _():
        octx_ref[...] = _ffn_tile(ctx_ref[...], mod_ref[1], 0, *args)


def _ffn_tile(x, m, mod0, g_pre, g_post, win_ref, wout_ref):
    shift, scale, gate = m[mod0:mod0 + 1], m[mod0 + 1:mod0 + 2], m[mod0 + 2:mod0 + 3]
    h = _rms(x, g_pre) * (1.0 + scale) + shift
    ab = jnp.dot(h.astype(BF16), win_ref[...], preferred_element_type=F32)
    a, b = ab[:, :D_FF], ab[:, D_FF:]
    g = (_silu(a) * b).astype(BF16)
    y = jnp.dot(g, wout_ref[...], preferred_element_type=F32)
    return x + (MACARON_WEIGHT * gate) * _rms(y, g_post)


def _mod_spec(n_lat_blocks):
    return pl.BlockSpec((1, N_MOD, D_MODEL), lambda i: (jnp.where(i < n_lat_blocks, 0, 1), 0, 0))


def _ffn1(x, ctx, mod, g_pre, g_post, w_in, w_out):
    tm = FUSED_TILE
    const = lambda i: (0, 0)
    return pl.pallas_call(
        _ffn_kernel,
        grid=(x.shape[0] // tm,),
        in_specs=[pl.BlockSpec((tm, D_MODEL), lambda i: (i, 0)),
                  pl.BlockSpec(ctx.shape, const),
                  pl.BlockSpec(mod.shape, lambda i: (0, 0, 0)),
                  pl.BlockSpec((1, D_MODEL), const),
                  pl.BlockSpec((1, D_MODEL), const),
                  pl.BlockSpec((D_MODEL, 2 * D_FF), const),
                  pl.BlockSpec((D_FF, D_MODEL), const)],
        out_specs=(pl.BlockSpec((tm, D_MODEL), lambda i: (i, 0)), pl.BlockSpec(ctx.shape, const)),
        out_shape=(jax.ShapeDtypeStruct(x.shape, F32), jax.ShapeDtypeStruct(ctx.shape, F32)),
        compiler_params=_params("arbitrary"),
        name="ffn",
    )(x, ctx, mod, g_pre, g_post, w_in, w_out)


def _rope_tables(i, tm, is_lat):
    n_rows = tm // GRID_W
    assert tm % GRID_W == 0 and n_rows <= 8
    lane = lax.broadcasted_iota(jnp.int32, (GRID_W, LANES), 1)
    sub = lax.broadcasted_iota(jnp.int32, (GRID_W, LANES), 0)
    e = lane - ROPE_LANE0
    in_rope = (e >= 0) & (e < MLA_ROPE_DIM)
    axis_dim = MLA_ROPE_DIM // 2
    row_axis = (e >> 4) == 0
    w = e & (axis_dim - 1)
    j = w & (axis_dim // 2 - 1)
    first_half = w < axis_dim // 2
    inv_freq = jnp.exp(j.astype(F32) * (-math.log(ROPE_BASE) * 2.0 / axis_dim))
    col_ang = sub.astype(F32) * inv_freq
    row_ang = (i * n_rows + sub[:8]).astype(F32) * inv_freq[:8]
    sign = jnp.where(first_half, -1.0, 1.0)

    def table(fn, off_value, scale):
        col_t, row_t = fn(col_ang), fn(row_ang)
        blocks = [jnp.where(row_axis, jnp.broadcast_to(row_t[r:r + 1], (GRID_W, LANES)), col_t)
                  for r in range(n_rows)]
        blocks = [jnp.where(in_rope, jnp.where(is_lat, b, off_value) * scale, 0.0) for b in blocks]
        return jnp.concatenate(blocks, axis=0)

    cos_k = table(jnp.cos, 1.0, 1.0)
    sin_s = table(jnp.sin, 0.0, sign)
    lane_t = lax.broadcasted_iota(jnp.int32, (tm, LANES), 1)
    cos_q = jnp.where(lane_t < ROPE_LANE0, 1.0, cos_k)
    return cos_q, cos_k, sin_s


def _mixin_kernel(h_ref, hctx_ref, mod_ref, gpre_ref, win_ref, qn_ref, wq_ref, kvn_ref, wkv_ref, wa_ref, ba_ref,
                  q_ref, k_ref, v_ref, gq_ref, gk_ref, gv_ref, gg_ref, la_ref, *, n_lat_blocks):
    i = pl.program_id(0)
    tm = h_ref.shape[0]
    m = mod_ref[0]
    shift, scale = m[3:4], m[4:5]
    h = jnp.where(i < n_lat_blocks, h_ref[...], hctx_ref[...])
    a = _rms(h, gpre_ref[...]) * (1.0 + scale) + shift
    z = jnp.dot(a.astype(BF16), win_ref[...], preferred_element_type=F32)

    cos_q, cos_k, sin_s = _rope_tables(i, tm, i < n_lat_blocks)

    qn = _rms(z[:, C_QA:C_QA + MLA_Q_RANK], qn_ref[...]).astype(BF16)
    qq = jnp.dot(qn, wq_ref[...], preferred_element_type=F32)
    hw = N_MLA_HEADS * HEAD_PAD
    sm_scale = MLA_QK_DIM ** -0.5 * math.log2(math.e)
    for h in range(N_MLA_HEADS):
        lo = h * HEAD_PAD
        qh = qq[:, lo:lo + HEAD_PAD] * cos_q + qq[:, hw + lo:hw + lo + HEAD_PAD] * sin_s
        q_ref[h] = (qh * sm_scale).astype(BF16).T

    kvn = _rms(z[:, C_KV:C_KV + MLA_KV_RANK], kvn_ref[...]).astype(BF16)
    kv = jnp.dot(kvn, wkv_ref[...], preferred_element_type=F32)
    misc = z[:, C_MISC:C_MISC + LANES]
    k_rope = misc * cos_k + z[:, C_KSW:C_KSW + LANES] * sin_s
    lane = lax.broadcasted_iota(jnp.int32, (tm, HEAD_PAD), 1)
    sub = lax.broadcasted_iota(jnp.int32, (VT_ROWS - MLA_V_DIM, tm), 0)
    ones_rows = (sub == 0).astype(BF16)
    for h in range(N_MLA_HEADS):
        kvh = kv[:, h * HEAD_PAD:(h + 1) * HEAD_PAD]
        k_ref[h] = jnp.where(lane < MLA_NOPE_DIM, kvh, k_rope).astype(BF16)
        v_ref[h, :MLA_V_DIM, :] = kvh.astype(BF16).T[MLA_NOPE_DIM:]
        v_ref[h, MLA_V_DIM:, :] = ones_rows

    gq_ref[...] = z[:, C_GQ:C_GQ + D_GLA_QK] * (GLA_DK ** -0.5)
    gk_ref[...] = z[:, C_GK:C_GK + D_GLA_QK]
    gv_ref[...] = z[:, C_GV:C_GV + D_GLA_OUT]
    gg_ref[...] = z[:, C_GG:C_GG + D_GLA_OUT]
    xg = jnp.dot(misc.astype(BF16), wa_ref[...], preferred_element_type=F32) + ba_ref[...]
    la = (jnp.minimum(xg, 0.0) - jnp.log1p(jnp.exp(-jnp.abs(xg)))) * (1.0 / GLA_GATE_NORM)
    la_ref[0] = la[:, :D_GLA_QK]
    la_ref[1] = la[:, D_GLA_QK:]


def _mixin(h_lat, h_ctx, mod, g_pre, w_in, q_norm, w_q, kv_norm, w_kv, w_a, b_a, *, n_lat):
    n_all = h_lat.shape[0] + h_ctx.shape[0]
    tm = ROW_TILE
    nl = n_lat // tm
    row = lambda i: (i, 0)
    const = lambda i: (0, 0)
    hrow = lambda i: (0, i, 0)
    full = lambda arr: pl.BlockSpec(arr.shape, const)
    out_shape = (
        jax.ShapeDtypeStruct((N_MLA_HEADS, HEAD_PAD, n_all), BF16),
        jax.ShapeDtypeStruct((N_MLA_HEADS, n_all, HEAD_PAD), BF16),
        jax.ShapeDtypeStruct((N_MLA_HEADS, VT_ROWS, n_all), BF16),
        jax.ShapeDtypeStruct((n_all, D_GLA_QK), F32),
        jax.ShapeDtypeStruct((n_all, D_GLA_QK), F32),
        jax.ShapeDtypeStruct((n_all, D_GLA_OUT), F32),
        jax.ShapeDtypeStruct((n_all, D_GLA_OUT), F32),
        jax.ShapeDtypeStruct((2, n_all, D_GLA_QK), F32),
    )
    out_specs = (
        pl.BlockSpec((N_MLA_HEADS, HEAD_PAD, tm), lambda i: (0, 0, i)),
        pl.BlockSpec((N_MLA_HEADS, tm, HEAD_PAD), hrow),
        pl.BlockSpec((N_MLA_HEADS, VT_ROWS, tm), lambda i: (0, 0, i)),
        pl.BlockSpec((tm, D_GLA_QK), row),
        pl.BlockSpec((tm, D_GLA_QK), row),
        pl.BlockSpec((tm, D_GLA_OUT), row),
        pl.BlockSpec((tm, D_GLA_OUT), row),
        pl.BlockSpec((2, tm, D_GLA_QK), hrow),
    )
    return pl.pallas_call(
        functools.partial(_mixin_kernel, n_lat_blocks=n_lat // tm),
        grid=(n_all // tm,),
        in_specs=[pl.BlockSpec((tm, D_MODEL), lambda i: (jnp.minimum(i, nl - 1), 0)),
                  pl.BlockSpec((tm, D_MODEL), lambda i: (jnp.maximum(i - nl, 0), 0)),
                  _mod_spec(nl), full(g_pre), full(w_in), full(q_norm), full(w_q), full(kv_norm), full(w_kv),
                  full(w_a), full(b_a)],
        out_specs=out_specs,
        out_shape=out_shape,
        compiler_params=_params("arbitrary"),
        name="mixer_in",
    )(h_lat, h_ctx, mod, g_pre, w_in, q_norm, w_q, kv_norm, w_kv, w_a, b_a)


def _attn_kernel(q_ref, k_ref, vt_ref, o_ref, sa_ref, sb_ref, *, bk):
    n_keys = k_ref.shape[1]
    bq = q_ref.shape[2]
    qs = [q_ref[h] for h in range(2)]

    n_trips = n_keys // bk

    def scores(h, c):
        off = pl.multiple_of(c * bk, bk)
        return jnp.dot(k_ref[h, pl.ds(off, bk), :], qs[h], preferred_element_type=F32)

    def fill(s_ref, h, c):
        st = scores(h, c)
        s_ref[h] = st
        return jnp.max(st, axis=0, keepdims=True)

    def update(carry, s_ref, mx, h, c):
        off = pl.multiple_of(c * bk, bk)
        m, acc = carry
        m_new = jnp.maximum(m, mx)
        pt = jnp.exp2(s_ref[h] - m_new).astype(BF16)
        acc = jnp.exp2(m - m_new) * acc + jnp.dot(vt_ref[h, :, pl.ds(off, bk)], pt,
                                                  preferred_element_type=F32)
        return m_new, acc

    bufs = (sa_ref, sb_ref)

    def run(carry, mx, first, count, fill_after_last):
        carry, mx = list(carry), list(mx)
        for u in range(count):
            for h in range(2):
                nxt = None
                if u < count - 1 or fill_after_last:
                    nxt = fill(bufs[(u + 1) % 2], h, first + u + 1)
                carry[h] = update(carry[h], bufs[u % 2], mx[h], h, first + u)
                mx[h] = nxt
        if not fill_after_last:
            mx = [jnp.zeros((1, bq), F32)] * 2
        return tuple(carry), tuple(mx)

    def group(i, state):
        return run(*state, ATTN_UNROLL * i, ATTN_UNROLL, True)

    carry = tuple((jnp.full((1, bq), -jnp.inf, F32), jnp.zeros((VT_ROWS, bq), F32)) for _ in range(2))
    n_groups = (n_trips - 1) // ATTN_UNROLL
    state = lax.fori_loop(0, n_groups, group, (carry, tuple(fill(sa_ref, h, 0) for h in range(2))))
    carry, _ = run(*state, ATTN_UNROLL * n_groups, n_trips - ATTN_UNROLL * n_groups, False)
    outs = [acc[:MLA_V_DIM] / acc[MLA_V_DIM:MLA_V_DIM + 1] for _, acc in carry]
    o_ref[...] = jnp.concatenate(outs, axis=0).astype(o_ref.dtype)


def _attn(q, k, vt, *, n_lat, bq, bk):
    n_all = k.shape[1]
    return pl.pallas_call(
        functools.partial(_attn_kernel, bk=bk),
        grid=(N_MLA_HEADS // 2, n_lat // bq),
        in_specs=[pl.BlockSpec((2, HEAD_PAD, bq), lambda hp, i: (hp, 0, i)),
                  pl.BlockSpec((2, n_all, HEAD_PAD), lambda hp, i: (hp, 0, 0)),
                  pl.BlockSpec((2, VT_ROWS, n_all), lambda hp, i: (hp, 0, 0))],
        out_specs=pl.BlockSpec((2 * MLA_V_DIM, bq), lambda hp, i: (hp, i)),
        out_shape=jax.ShapeDtypeStruct((D_MLA_OUT, n_lat), BF16),
        scratch_shapes=[pltpu.VMEM((2, bk, bq), F32), pltpu.VMEM((2, bk, bq), F32)],
        compiler_params=_params("arbitrary", "arbitrary"),
        name="mla_attention",
    )(q, k, vt)


def _gla_level_matrix(fwd):
    c = GLA_CHUNK
    ri = lax.broadcasted_iota(jnp.int32, (c, c), 0)
    ci = lax.broadcasted_iota(jnp.int32, (c, c), 1)

    def sides(b):
        bs = ri & (~(b - 1))
        be = bs + (b - 1)
        q_f = ((ci >= bs) & (ci <= ri)).astype(F32)
        k_f = ((ci > ri) & (ci <= be)).astype(F32)
        q_b = ((ci >= ri) & (ci <= be)).astype(F32)
        k_b = ((ci < ri) & (ci >= bs)).astype(F32)
        return q_f, k_f, q_b, k_b

    q_f, k_f, q_b, k_b = sides(c)
    blocks = [jnp.where(fwd, q_f, q_b), jnp.where(fwd, k_f, k_b)]
    for b in GLA_SUB_LEVELS:
        q_f, k_f, q_b, k_b = sides(b)
        bit = (ri & b) != 0
        blocks.append(jnp.where(fwd, jnp.where(bit, q_f, k_f), jnp.where(bit, k_b, q_b)))
    return jnp.concatenate(blocks, axis=0)


def _gla_level_ids(fwd, n):
    ri = lax.broadcasted_iota(jnp.int32, (n, n), 0)
    ci = lax.broadcasted_iota(jnp.int32, (n, n), 1)
    xr = ri ^ ci
    later = jnp.where(fwd, ri, ci)
    ids = jnp.where(xr == 0, GLA_DIAG, 0)
    for b in GLA_SUB_LEVELS:
        ids = jnp.where(((xr & (-b)) == b) & ((later & b) != 0), b, ids)
    return ids


def _gla_kernel(q_ref, k_ref, v_ref, la_ref, o_ref, st_ref, w3_ref, ids_ref):
    d = pl.program_id(0)
    s = pl.program_id(1)
    fwd = d == 0
    c = GLA_CHUNK
    tm = q_ref.shape[0]
    n_chunks = tm // c
    n_pairs = N_GLA_HEADS // 2
    nt = (((1,), (1,)), ((), ()))
    tn = (((0,), (0,)), ((), ()))

    @pl.when(s == 0)
    def _():
        st_ref[...] = jnp.zeros_like(st_ref)
        w = _gla_level_matrix(fwd).astype(BF16)
        w3_ref[...] = jnp.concatenate([w, w, w], axis=1)
        ids_ref[...] = _gla_level_ids(fwd, tm)

    g = la_ref[0]
    g = jnp.concatenate([g[t * c:(t + 1) * c] for t in range(n_chunks)], axis=1)
    g_hi = g.astype(BF16)
    r1 = g - g_hi.astype(F32)
    g_mid = r1.astype(BF16)
    g_lo = (r1 - g_mid.astype(F32)).astype(BF16)
    e_all = jnp.exp(jnp.dot(w3_ref[...], jnp.concatenate([g_hi, g_mid, g_lo], axis=0),
                            preferred_element_type=F32))

    def e_block(blk, p):
        return jnp.concatenate(
            [e_all[blk * c:(blk + 1) * c, t * D_GLA_QK + p * LANES:t * D_GLA_QK + (p + 1) * LANES]
             for t in range(n_chunks)], axis=0)

    lane = lax.broadcasted_iota(jnp.int32, (tm, LANES), 1)
    lane_v = lax.broadcasted_iota(jnp.int32, (GLA_DV, LANES), 1)
    ids = ids_ref[...]
    ids2 = jnp.concatenate([ids, ids], axis=0)
    vb = v_ref[...].astype(BF16)

    for p in range(n_pairs):
        ls = slice(p * LANES, (p + 1) * LANES)
        qp, kp = q_ref[:, ls], k_ref[:, ls]
        qm = (jnp.where(lane < GLA_DK, qp, 0.0), jnp.where(lane >= GLA_DK, qp, 0.0))

        att = jnp.where(ids2 == GLA_DIAG,
                        lax.dot_general(jnp.concatenate(qm, axis=0).astype(BF16), kp.astype(BF16), nt,
                                        preferred_element_type=F32), 0.0)
        for li, b in enumerate(GLA_SUB_LEVELS):
            eb = e_block(2 + li, p)
            lhs = jnp.concatenate([qm[0] * eb, qm[1] * eb], axis=0).astype(BF16)
            part = lax.dot_general(lhs, (kp * eb).astype(BF16), nt, preferred_element_type=F32)
            att = jnp.where(ids2 == b, part, att)
        att = att.astype(BF16)

        eq, ek = e_block(0, p), e_block(1, p)
        k_inter = (kp * ek).astype(BF16)
        decay, upd = [], []
        for t in range(n_chunks):
            rows = slice(t * c, (t + 1) * c)
            e_q = e_all[0:c, t * D_GLA_QK + p * LANES:t * D_GLA_QK + (p + 1) * LANES]
            decay.append(jnp.where(fwd, e_q[c - 1:c], e_q[0:1]))
            u = lax.dot_general(vb[rows, 2 * p * GLA_DV:(2 * p + 2) * GLA_DV], k_inter[rows], tn,
                                preferred_element_type=F32)
            upd.append(jnp.where(lane_v < GLA_DK, u[:GLA_DV], u[GLA_DV:]))
        st = st_ref[p]
        before = []
        for t in range(n_chunks):
            r = n_chunks - 1 - t
            before.append(st)
            st = jnp.where(fwd, decay[t], decay[r]) * st + jnp.where(fwd, upd[t], upd[r])
        st_ref[p] = st

        for t in range(n_chunks):
            rows = slice(t * c, (t + 1) * c)
            s_t = jnp.where(fwd, before[t], before[n_chunks - 1 - t]).astype(BF16)
            lhs = jnp.concatenate([qm[0][rows] * eq[rows], qm[1][rows] * eq[rows]], axis=0).astype(BF16)
            o_inter = lax.dot_general(lhs, s_t, nt, preferred_element_type=F32)
            for hh in range(2):
                h = 2 * p + hh
                o_intra = jnp.dot(att[hh * tm + t * c:hh * tm + (t + 1) * c],
                                  vb[:, h * GLA_DV:(h + 1) * GLA_DV], preferred_element_type=F32)
                o_ref[0, rows, h * GLA_DV:(h + 1) * GLA_DV] = o_inter[hh * c:(hh + 1) * c] + o_intra


def _gla(gq, gk, gv, la, *, n_lat):
    n_all = gq.shape[0]
    tm = ROW_TILE
    nl = n_lat // tm
    nb = n_all // tm
    nc = nb - nl

    def blk(d, s):
        ctx = jnp.where(d == 0, nl + s, nb - 1 - s)
        lat = jnp.where(d == 0, s - nc, nb - 1 - s)
        return jnp.where(s < nc, ctx, lat)

    row = lambda d, s: (blk(d, s), 0)
    return pl.pallas_call(
        _gla_kernel,
        grid=(2, nb),
        in_specs=[pl.BlockSpec((tm, D_GLA_QK), row),
                  pl.BlockSpec((tm, D_GLA_QK), row),
                  pl.BlockSpec((tm, D_GLA_OUT), row),
                  pl.BlockSpec((1, tm, D_GLA_QK), lambda d, s: (d, blk(d, s), 0))],
        out_specs=pl.BlockSpec((1, tm, D_GLA_OUT), lambda d, s: (d, blk(d, s), 0)),
        out_shape=jax.ShapeDtypeStruct((2, n_all, D_GLA_OUT), F32),
        scratch_shapes=[pltpu.VMEM((N_GLA_HEADS // 2, GLA_DV, 2 * GLA_DK), F32),
                        pltpu.VMEM(((2 + len(GLA_SUB_LEVELS)) * GLA_CHUNK, 3 * GLA_CHUNK), BF16),
                        pltpu.VMEM((tm, tm), jnp.int32)],
        compiler_params=_params("arbitrary", "arbitrary"),
        name="gla_scan",
    )(gq, gk, gv, la)


def _mixout_ffn_kernel(h_ref, mlat_ref, og_ref, gg_ref, gn_ref, wmix_ref, mod_ref, gpost1_ref,
                       gpre2_ref, gpost2_ref, win_ref, wout_ref, o_ref):
    o = og_ref[0] + og_ref[1]
    gate = _silu(gg_ref[...])
    parts = []
    for h in range(N_GLA_HEADS):
        sl = slice(h * GLA_DV, (h + 1) * GLA_DV)
        parts.append((_rms(o[:, sl], gn_ref[...]) * gate[:, sl]).astype(BF16))
    tn = (((0,), (0,)), ((), ()))
    y = lax.dot_general(mlat_ref[...], wmix_ref[:D_MLA_OUT, :], tn, preferred_element_type=F32)
    y = y + jnp.dot(jnp.concatenate(parts, axis=-1), wmix_ref[D_MLA_OUT:, :], preferred_element_type=F32)
    m = mod_ref[0]
    x = h_ref[...] + m[5:6] * _rms(y, gpost1_ref[...])
    o_ref[...] = _ffn_tile(x, m, 6, gpre2_ref[...], gpost2_ref[...], win_ref, wout_ref)


def _mixout_ffn(h_lat, mla_t, og, gg, g_norm, w_mix, mod, g_post1, g_pre2, g_post2, w_in, w_out, *, n_lat):
    tm = FUSED_TILE
    row = lambda i: (i, 0)
    const = lambda i: (0, 0)
    vec = pl.BlockSpec((1, D_MODEL), const)
    return pl.pallas_call(
        _mixout_ffn_kernel,
        grid=(n_lat // tm,),
        in_specs=[pl.BlockSpec((tm, D_MODEL), row),
                  pl.BlockSpec((D_MLA_OUT, tm), lambda i: (0, i)),
                  pl.BlockSpec((2, tm, D_GLA_OUT), lambda i: (0, i, 0)),
                  pl.BlockSpec((tm, D_GLA_OUT), row),
                  pl.BlockSpec((1, GLA_DV), const),
                  pl.BlockSpec(w_mix.shape, const),
                  pl.BlockSpec((1, N_MOD, D_MODEL), lambda i: (0, 0, 0)),
                  vec, vec, vec,
                  pl.BlockSpec((D_MODEL, 2 * D_FF), const),
                  pl.BlockSpec((D_FF, D_MODEL), const)],
        out_specs=pl.BlockSpec((tm, D_MODEL), row),
        out_shape=jax.ShapeDtypeStruct((n_lat, D_MODEL), F32),
        compiler_params=_params("arbitrary"),
        name="mixer_out_ffn",
    )(h_lat, mla_t, og, gg, g_norm, w_mix, mod, g_post1, g_pre2, g_post2, w_in, w_out)


def _rope_half_swap(w):
    half = MLA_ROPE_DIM // 4
    g = w.reshape(w.shape[:-1] + (2, 2, half))
    return g[..., ::-1, :].reshape(w.shape)


def _layout_w_in(w):
    d = w.shape[0]
    o_kv = MLA_Q_RANK
    o_kr = o_kv + MLA_KV_RANK
    o_gq = o_kr + MLA_ROPE_DIM
    o_gk = o_gq + D_GLA_QK
    o_gv = o_gk + D_GLA_QK
    o_gg = o_gv + D_GLA_OUT
    o_ga = o_gg + D_GLA_OUT
    k_rope = w[:, o_kr:o_gq]
    z = lambda n: jnp.zeros((d, n), w.dtype)
    pad = LANES - ROPE_LANE0 - MLA_ROPE_DIM
    return jnp.concatenate([
        w[:, :o_kr], w[:, o_gq:o_ga],
        w[:, o_ga:o_ga + 2 * GLA_GATE_RANK], z(ROPE_LANE0 - 2 * GLA_GATE_RANK), k_rope, z(pad),
        z(ROPE_LANE0), _rope_half_swap(k_rope), z(pad)], axis=1)


def _layout_w_q(w):
    r = w.shape[0]
    w = w.reshape(r, N_MLA_HEADS, MLA_QK_DIM)
    nope, rope = w[..., :MLA_NOPE_DIM], w[..., MLA_NOPE_DIM:]
    zp = jnp.zeros((r, N_MLA_HEADS, HEAD_PAD - MLA_QK_DIM), w.dtype)
    zn = jnp.zeros_like(nope)
    plain = jnp.concatenate([nope, rope, zp], axis=-1).reshape(r, -1)
    swapped = jnp.concatenate([zn, _rope_half_swap(rope), zp], axis=-1).reshape(r, -1)
    return jnp.concatenate([plain, swapped], axis=1)


def _layout_w_a(w_f, w_b):
    z = jnp.zeros((LANES, 2 * D_GLA_QK), w_f.dtype)
    z = z.at[:GLA_GATE_RANK, :D_GLA_QK].set(w_f)
    return z.at[GLA_GATE_RANK:2 * GLA_GATE_RANK, D_GLA_QK:].set(w_b)


def _attn_key_block(n_all):
    for bk in (640, 512, 256):
        if n_all % bk == 0:
            return bk
    raise ValueError(f"unsupported key count {n_all}")


def kernel(x, c, ctx, c_ctx, w_ada, b_ada, norm_pre, norm_post, ffn1_w_in, ffn1_w_out, ffn2_w_in, ffn2_w_out, w_in, mla_q_norm, mla_w_qb, mla_kv_norm, mla_w_kvb, gla_w_a_fwd, gla_b_a_fwd, gla_w_a_bwd, gla_b_a_bwd, gla_norm, w_out):
    assert x.shape[0] == 1 and ctx.shape[0] == 1 and w_ada.shape[0] == 1
    n_lat, n_ctx = x.shape[1], ctx.shape[1]
    assert n_lat % ROW_TILE == 0 and n_ctx % ROW_TILE == 0 and n_lat % GRID_W == 0
    n_all = n_lat + n_ctx
    row2 = lambda a: a.reshape(1, -1)

    ct = jnp.stack([c[0], c_ctx], axis=1)
    mod = _ada(ct, w_ada[0], row2(b_ada[0])).reshape(2, N_MOD, D_MODEL)

    h_lat, h_ctx = _ffn1(x[0], ctx[0], mod, row2(norm_pre[0, 0]), row2(norm_post[0, 0]),
                  ffn1_w_in[0].astype(BF16), ffn1_w_out[0].astype(BF16))

    q, k, v, gq, gk, gv, gg, la = _mixin(
        h_lat, h_ctx, mod, row2(norm_pre[0, 1]), _layout_w_in(w_in[0]).astype(BF16),
        row2(mla_q_norm[0]), _layout_w_q(mla_w_qb[0]).astype(BF16),
        row2(mla_kv_norm[0]), mla_w_kvb[0].astype(BF16),
        _layout_w_a(gla_w_a_fwd[0], gla_w_a_bwd[0]).astype(BF16),
        jnp.concatenate([gla_b_a_fwd[0], gla_b_a_bwd[0]]).reshape(1, -1), n_lat=n_lat)

    mla = _attn(q, k, v, n_lat=n_lat, bq=ATTN_Q_TILE, bk=_attn_key_block(n_all))
    og = _gla(gq, gk, gv, la, n_lat=n_lat)

    out = _mixout_ffn(h_lat, mla, og, gg, row2(gla_norm[0]), w_out[0].astype(BF16), mod,
                      row2(norm_post[0, 1]), row2(norm_pre[0, 2]), row2(norm_post[0, 2]),
                      ffn2_w_in[0].astype(BF16), ffn2_w_out[0].astype(BF16), n_lat=n_lat)
    return out[None]
```

```python
import functools
import math

import jax
import jax.numpy as jnp
from jax import lax
from jax.experimental import pallas as pl
from jax.experimental.pallas import tpu as pltpu

F32 = jnp.float32
BF16 = jnp.bfloat16

D_MODEL = 1024
GRID_W = 64
N_MLA_HEADS = 8
MLA_Q_RANK = 256
MLA_KV_RANK = 128
MLA_NOPE_DIM = 64
MLA_ROPE_DIM = 32
MLA_QK_DIM = MLA_NOPE_DIM + MLA_ROPE_DIM
MLA_V_DIM = 64
N_GLA_HEADS = 4
GLA_DK = 64
GLA_DV = 128
GLA_GATE_RANK = 16
GLA_GATE_NORM = 16.0
GLA_CHUNK = 64
D_FF = 2816
MACARON_WEIGHT = 0.5
ROPE_BASE = 10000.0
EPS = 1e-6
N_MOD = 9

D_MLA_OUT = N_MLA_HEADS * MLA_V_DIM
D_GLA_OUT = N_GLA_HEADS * GLA_DV
D_GLA_QK = N_GLA_HEADS * GLA_DK

LANES = 128
HEAD_PAD = 128
VT_ROWS = 128
ROW_TILE = 256
FUSED_TILE = 512
ATTN_Q_TILE = 512
ATTN_UNROLL = 12
VMEM_LIMIT = 56 * 1024 * 1024
C_QA = 0
C_KV = C_QA + MLA_Q_RANK
C_GQ = C_KV + MLA_KV_RANK
C_GK = C_GQ + D_GLA_QK
C_GV = C_GK + D_GLA_QK
C_GG = C_GV + D_GLA_OUT
C_MISC = C_GG + D_GLA_OUT
C_KSW = C_MISC + LANES
C_TOTAL = C_KSW + LANES
ROPE_LANE0 = MLA_NOPE_DIM
GLA_SUB_LEVELS = (32, 16, 8, 4, 2, 1)
GLA_DIAG = 64


def _rms(x, g):
    return x * lax.rsqrt(jnp.mean(x * x, axis=-1, keepdims=True) + EPS) * g


def _silu(x):
    return x * jax.nn.sigmoid(x)


def _params(*sem):
    return pltpu.CompilerParams(dimension_semantics=sem, vmem_limit_bytes=VMEM_LIMIT)


def _ada_kernel(ct_ref, w_ref, b_ref, o_ref):
    s = _silu(ct_ref[...])
    w = w_ref[...]
    r0 = jnp.sum(s[:, 0:1] * w, axis=0, keepdims=True)
    r1 = jnp.sum(s[:, 1:2] * w, axis=0, keepdims=True)
    o_ref[...] = jnp.concatenate([r0, r1], axis=0) + b_ref[...]


def _ada(ct, w, b):
    d, n = w.shape
    tn = n // 8
    return pl.pallas_call(
        _ada_kernel,
        grid=(n // tn,),
        in_specs=[pl.BlockSpec((d, 2), lambda j: (0, 0)),
                  pl.BlockSpec((d, tn), lambda j: (0, j)),
                  pl.BlockSpec((1, tn), lambda j: (0, j))],
        out_specs=pl.BlockSpec((2, tn), lambda j: (0, j)),
        out_shape=jax.ShapeDtypeStruct((2, n), F32),
        compiler_params=_params("arbitrary"),
        name="adaln",
    )(ct, w, b)


def _ffn_kernel(x_ref, ctx_ref, mod_ref, gpre_ref, gpost_ref, win_ref, wout_ref, o_ref, octx_ref):
    args = (gpre_ref[...], gpost_ref[...], win_ref, wout_ref)
    o_ref[...] = _ffn_tile(x_ref[...], mod_ref[0], 0, *args)

    @pl.when(pl.program_id(0) == pl.num_programs(0) - 1)
    def _():
        octx_ref[...] = _ffn_tile(ctx_ref[...], mod_ref[1], 0, *args)


def _ffn_tile(x, m, mod0, g_pre, g_post, win_ref, wout_ref):
    shift, scale, gate = m[mod0:mod0 + 1], m[mod0 + 1:mod0 + 2], m[mod0 + 2:mod0 + 3]
    h = _rms(x, g_pre) * (1.0 + scale) + shift
    ab = jnp.dot(h.astype(BF16), win_ref[...], preferred_element_type=F32)
    a, b = ab[:, :D_FF], ab[:, D_FF:]
    g = (_silu(a) * b).astype(BF16)
    y = jnp.dot(g, wout_ref[...], preferred_element_type=F32)
    return x + (MACARON_WEIGHT * gate) * _rms(y, g_post)


def _mod_spec(n_lat_blocks):
    return pl.BlockSpec((1, N_MOD, D_MODEL), lambda i: (jnp.where(i < n_lat_blocks, 0, 1), 0, 0))


def _ffn1(x, ctx, mod, g_pre, g_post, w_in, w_out):
    tm = FUSED_TILE
    const = lambda i: (0, 0)
    return pl.pallas_call(
        _ffn_kernel,
        grid=(x.shape[0] // tm,),
        in_specs=[pl.BlockSpec((tm, D_MODEL), lambda i: (i, 0)),
                  pl.BlockSpec(ctx.shape, const),
                  pl.BlockSpec(mod.shape, lambda i: (0, 0, 0)),
                  pl.BlockSpec((1, D_MODEL), const),
                  pl.BlockSpec((1, D_MODEL), const),
                  pl.BlockSpec((D_MODEL, 2 * D_FF), const),
                  pl.BlockSpec((D_FF, D_MODEL), const)],
        out_specs=(pl.BlockSpec((tm, D_MODEL), lambda i: (i, 0)), pl.BlockSpec(ctx.shape, const)),
        out_shape=(jax.ShapeDtypeStruct(x.shape, F32), jax.ShapeDtypeStruct(ctx.shape, F32)),
        compiler_params=_params("arbitrary"),
        name="ffn",
    )(x, ctx, mod, g_pre, g_post, w_in, w_out)


def _rope_tables(i, tm, is_lat):
    n_rows = tm // GRID_W
    assert tm % GRID_W == 0 and n_rows <= 8
    lane = lax.broadcasted_iota(jnp.int32, (GRID_W, LANES), 1)
    sub = lax.broadcasted_iota(jnp.int32, (GRID_W, LANES), 0)
    e = lane - ROPE_LANE0
    in_rope = (e >= 0) & (e < MLA_ROPE_DIM)
    axis_dim = MLA_ROPE_DIM // 2
    row_axis = (e >> 4) == 0
    w = e & (axis_dim - 1)
    j = w & (axis_dim // 2 - 1)
    first_half = w < axis_dim // 2
    inv_freq = jnp.exp(j.astype(F32) * (-math.log(ROPE_BASE) * 2.0 / axis_dim))
    col_ang = sub.astype(F32) * inv_freq
    row_ang = (i * n_rows + sub[:8]).astype(F32) * inv_freq[:8]
    sign = jnp.where(first_half, -1.0, 1.0)

    def table(fn, off_value, scale):
        col_t, row_t = fn(col_ang), fn(row_ang)
        blocks = [jnp.where(row_axis, jnp.broadcast_to(row_t[r:r + 1], (GRID_W, LANES)), col_t)
                  for r in range(n_rows)]
        blocks = [jnp.where(in_rope, jnp.where(is_lat, b, off_value) * scale, 0.0) for b in blocks]
        return jnp.concatenate(blocks, axis=0)

    cos_k = table(jnp.cos, 1.0, 1.0)
    sin_s = table(jnp.sin, 0.0, sign)
    lane_t = lax.broadcasted_iota(jnp.int32, (tm, LANES), 1)
    cos_q = jnp.where(lane_t < ROPE_LANE0, 1.0, cos_k)
    return cos_q, cos_k, sin_s


def _mixin_kernel(h_ref, hctx_ref, mod_ref, gpre_ref, win_ref, qn_ref, wq_ref, kvn_ref, wkv_ref, wa_ref, ba_ref,
                  q_ref, k_ref, v_ref, gq_ref, gk_ref, gv_ref, gg_ref, la_ref, *, n_lat_blocks):
    i = pl.program_id(0)
    tm = h_ref.shape[0]
    m = mod_ref[0]
    shift, scale = m[3:4], m[4:5]
    h = jnp.where(i < n_lat_blocks, h_ref[...], hctx_ref[...])
    a = _rms(h, gpre_ref[...]) * (1.0 + scale) + shift
    z = jnp.dot(a.astype(BF16), win_ref[...], preferred_element_type=F32)

    cos_q, cos_k, sin_s = _rope_tables(i, tm, i < n_lat_blocks)

    qn = _rms(z[:, C_QA:C_QA + MLA_Q_RANK], qn_ref[...]).astype(BF16)
    qq = jnp.dot(qn, wq_ref[...], preferred_element_type=F32)
    hw = N_MLA_HEADS * HEAD_PAD
    sm_scale = MLA_QK_DIM ** -0.5 * math.log2(math.e)
    for h in range(N_MLA_HEADS):
        lo = h * HEAD_PAD
        qh = qq[:, lo:lo + HEAD_PAD] * cos_q + qq[:, hw + lo:hw + lo + HEAD_PAD] * sin_s
        q_ref[h] = (qh * sm_scale).astype(BF16).T

    kvn = _rms(z[:, C_KV:C_KV + MLA_KV_RANK], kvn_ref[...]).astype(BF16)
    kv = jnp.dot(kvn, wkv_ref[...], preferred_element_type=F32)
    misc = z[:, C_MISC:C_MISC + LANES]
    k_rope = misc * cos_k + z[:, C_KSW:C_KSW + LANES] * sin_s
    lane = lax.broadcasted_iota(jnp.int32, (tm, HEAD_PAD), 1)
    sub = lax.broadcasted_iota(jnp.int32, (VT_ROWS - MLA_V_DIM, tm), 0)
    ones_rows = (sub == 0).astype(BF16)
    for h in range(N_MLA_HEADS):
        kvh = kv[:, h * HEAD_PAD:(h + 1) * HEAD_PAD]
        k_ref[h] = jnp.where(lane < MLA_NOPE_DIM, kvh, k_rope).astype(BF16)
        v_ref[h, :MLA_V_DIM, :] = kvh.astype(BF16).T[MLA_NOPE_DIM:]
        v_ref[h, MLA_V_DIM:, :] = ones_rows

    gq_ref[...] = z[:, C_GQ:C_GQ + D_GLA_QK] * (GLA_DK ** -0.5)
    gk_ref[...] = z[:, C_GK:C_GK + D_GLA_QK]
    gv_ref[...] = z[:, C_GV:C_GV + D_GLA_OUT]
    gg_ref[...] = z[:, C_GG:C_GG + D_GLA_OUT]
    xg = jnp.dot(misc.astype(BF16), wa_ref[...], preferred_element_type=F32) + ba_ref[...]
    la = (jnp.minimum(xg, 0.0) - jnp.log1p(jnp.exp(-jnp.abs(xg)))) * (1.0 / GLA_GATE_NORM)
    la_ref[0] = la[:, :D_GLA_QK]
    la_ref[1] = la[:, D_GLA_QK:]


def _mixin(h_lat, h_ctx, mod, g_pre, w_in, q_norm, w_q, kv_norm, w_kv, w_a, b_a, *, n_lat):
    n_all = h_lat.shape[0] + h_ctx.shape[0]
    tm = ROW_TILE
    nl = n_lat // tm
    row = lambda i: (i, 0)
    const = lambda i: (0, 0)
    hrow = lambda i: (0, i, 0)
    full = lambda arr: pl.BlockSpec(arr.shape, const)
    out_shape = (
        jax.ShapeDtypeStruct((N_MLA_HEADS, HEAD_PAD, n_all), BF16),
        jax.ShapeDtypeStruct((N_MLA_HEADS, n_all, HEAD_PAD), BF16),
        jax.ShapeDtypeStruct((N_MLA_HEADS, VT_ROWS, n_all), BF16),
        jax.ShapeDtypeStruct((n_all, D_GLA_QK), F32),
        jax.ShapeDtypeStruct((n_all, D_GLA_QK), F32),
        jax.ShapeDtypeStruct((n_all, D_GLA_OUT), F32),
        jax.ShapeDtypeStruct((n_all, D_GLA_OUT), F32),
        jax.ShapeDtypeStruct((2, n_all, D_GLA_QK), F32),
    )
    out_specs = (
        pl.BlockSpec((N_MLA_HEADS, HEAD_PAD, tm), lambda i: (0, 0, i)),
        pl.BlockSpec((N_MLA_HEADS, tm, HEAD_PAD), hrow),
        pl.BlockSpec((N_MLA_HEADS, VT_ROWS, tm), lambda i: (0, 0, i)),
        pl.BlockSpec((tm, D_GLA_QK), row),
        pl.BlockSpec((tm, D_GLA_QK), row),
        pl.BlockSpec((tm, D_GLA_OUT), row),
        pl.BlockSpec((tm, D_GLA_OUT), row),
        pl.BlockSpec((2, tm, D_GLA_QK), hrow),
    )
    return pl.pallas_call(
        functools.partial(_mixin_kernel, n_lat_blocks=n_lat // tm),
        grid=(n_all // tm,),
        in_specs=[pl.BlockSpec((tm, D_MODEL), lambda i: (jnp.minimum(i, nl - 1), 0)),
                  pl.BlockSpec((tm, D_MODEL), lambda i: (jnp.maximum(i - nl, 0), 0)),
                  _mod_spec(nl), full(g_pre), full(w_in), full(q_norm), full(w_q), full(kv_norm), full(w_kv),
                  full(w_a), full(b_a)],
        out_specs=out_specs,
        out_shape=out_shape,
        compiler_params=_params("arbitrary"),
        name="mixer_in",
    )(h_lat, h_ctx, mod, g_pre, w_in, q_norm, w_q, kv_norm, w_kv, w_a, b_a)


def _attn_kernel(q_ref, qnext_ref, k_ref, vt_ref, o_ref, sa_ref, sb_ref, mx_ref, *, bk):
    n_keys = k_ref.shape[1]
    bq = q_ref.shape[2]
    q_now = [q_ref[h] for h in range(2)]
    q_next = [qnext_ref[h] for h in range(2)]

    n_trips = n_keys // bk
    carry_over = n_trips % 2 == 0

    def fill(s_ref, h, c, qs=q_now):
        off = pl.multiple_of(c * bk, bk)
        st = jnp.dot(k_ref[h, pl.ds(off, bk), :], qs[h], preferred_element_type=F32)
        s_ref[h] = st
        return jnp.max(st, axis=0, keepdims=True)

    def update(carry, s_ref, mx, h, c):
        off = pl.multiple_of(c * bk, bk)
        m, acc = carry
        m_new = jnp.maximum(m, mx)
        pt = jnp.exp2(s_ref[h] - m_new).astype(BF16)
        acc = jnp.exp2(m - m_new) * acc + jnp.dot(vt_ref[h, :, pl.ds(off, bk)], pt,
                                                  preferred_element_type=F32)
        return m_new, acc

    bufs = (sa_ref, sb_ref)

    def run(carry, mx, first, count, fill_after_last):
        carry, mx = list(carry), list(mx)
        for u in range(count):
            for h in range(2):
                nxt = None
                if u < count - 1 or fill_after_last:
                    nxt = fill(bufs[(u + 1) % 2], h, first + u + 1)
                elif carry_over:
                    nxt = fill(bufs[(u + 1) % 2], h, 0, q_next)
                carry[h] = update(carry[h], bufs[u % 2], mx[h], h, first + u)
                mx[h] = nxt
        if not fill_after_last and not carry_over:
            mx = [jnp.zeros((1, bq), F32)] * 2
        return tuple(carry), tuple(mx)

    def group(i, state):
        return run(*state, ATTN_UNROLL * i, ATTN_UNROLL, True)

    def first_fill():
        for h in range(2):
            mx_ref[h] = fill(sa_ref, h, 0)

    if carry_over:
        pl.when(pl.program_id(1) == 0)(first_fill)
    else:
        first_fill()
    carry = tuple((jnp.full((1, bq), -jnp.inf, F32), jnp.zeros((VT_ROWS, bq), F32)) for _ in range(2))
    n_groups = (n_trips - 1) // ATTN_UNROLL
    state = lax.fori_loop(0, n_groups, group, (carry, (mx_ref[0], mx_ref[1])))
    carry, mx = run(*state, ATTN_UNROLL * n_groups, n_trips - ATTN_UNROLL * n_groups, False)
    if carry_over:
        for h in range(2):
            mx_ref[h] = mx[h]
    outs = [acc[:MLA_V_DIM] / acc[MLA_V_DIM:MLA_V_DIM + 1] for _, acc in carry]
    o_ref[...] = jnp.concatenate(outs, axis=0).astype(o_ref.dtype)


def _attn(q, k, vt, *, n_lat, bq, bk):
    n_all = k.shape[1]
    return pl.pallas_call(
        functools.partial(_attn_kernel, bk=bk),
        grid=(N_MLA_HEADS // 2, n_lat // bq),
        in_specs=[pl.BlockSpec((2, HEAD_PAD, bq), lambda hp, i: (hp, 0, i)),
                  pl.BlockSpec((2, HEAD_PAD, bq), lambda hp, i: (hp, 0, jnp.minimum(i + 1, n_lat // bq - 1))),
                  pl.BlockSpec((2, n_all, HEAD_PAD), lambda hp, i: (hp, 0, 0)),
                  pl.BlockSpec((2, VT_ROWS, n_all), lambda hp, i: (hp, 0, 0))],
        out_specs=pl.BlockSpec((2 * MLA_V_DIM, bq), lambda hp, i: (hp, i)),
        out_shape=jax.ShapeDtypeStruct((D_MLA_OUT, n_lat), BF16),
        scratch_shapes=[pltpu.VMEM((2, bk, bq), F32), pltpu.VMEM((2, bk, bq), F32),
                        pltpu.VMEM((2, 1, bq), F32)],
        compiler_params=_params("arbitrary", "arbitrary"),
        name="mla_attention",
    )(q, q, k, vt)


def _gla_level_matrix(fwd):
    c = GLA_CHUNK
    ri = lax.broadcasted_iota(jnp.int32, (c, c), 0)
    ci = lax.broadcasted_iota(jnp.int32, (c, c), 1)

    def sides(b):
        bs = ri & (~(b - 1))
        be = bs + (b - 1)
        q_f = ((ci >= bs) & (ci <= ri)).astype(F32)
        k_f = ((ci > ri) & (ci <= be)).astype(F32)
        q_b = ((ci >= ri) & (ci <= be)).astype(F32)
        k_b = ((ci < ri) & (ci >= bs)).astype(F32)
        return q_f, k_f, q_b, k_b

    q_f, k_f, q_b, k_b = sides(c)
    blocks = [jnp.where(fwd, q_f, q_b), jnp.where(fwd, k_f, k_b)]
    for b in GLA_SUB_LEVELS:
        q_f, k_f, q_b, k_b = sides(b)
        bit = (ri & b) != 0
        blocks.append(jnp.where(fwd, jnp.where(bit, q_f, k_f), jnp.where(bit, k_b, q_b)))
    return jnp.concatenate(blocks, axis=0)


def _gla_level_ids(fwd, n):
    ri = lax.broadcasted_iota(jnp.int32, (n, n), 0)
    ci = lax.broadcasted_iota(jnp.int32, (n, n), 1)
    xr = ri ^ ci
    later = jnp.where(fwd, ri, ci)
    ids = jnp.where(xr == 0, GLA_DIAG, 0)
    for b in GLA_SUB_LEVELS:
        ids = jnp.where(((xr & (-b)) == b) & ((later & b) != 0), b, ids)
    return ids


def _gla_kernel(q_ref, k_ref, v_ref, la_ref, o_ref, st_ref, w3_ref, ids_ref):
    d = pl.program_id(0)
    s = pl.program_id(1)
    fwd = d == 0
    c = GLA_CHUNK
    tm = q_ref.shape[0]
    n_chunks = tm // c
    n_pairs = N_GLA_HEADS // 2
    nt = (((1,), (1,)), ((), ()))
    tn = (((0,), (0,)), ((), ()))

    @pl.when(s == 0)
    def _():
        st_ref[...] = jnp.zeros_like(st_ref)
        w = _gla_level_matrix(fwd).astype(BF16)
        w3_ref[...] = jnp.concatenate([w, w, w], axis=1)
        ids_ref[...] = _gla_level_ids(fwd, tm)

    g = la_ref[0]
    g = jnp.concatenate([g[t * c:(t + 1) * c] for t in range(n_chunks)], axis=1)
    g_hi = g.astype(BF16)
    r1 = g - g_hi.astype(F32)
    g_mid = r1.astype(BF16)
    g_lo = (r1 - g_mid.astype(F32)).astype(BF16)
    e_all = jnp.exp(jnp.dot(w3_ref[...], jnp.concatenate([g_hi, g_mid, g_lo], axis=0),
                            preferred_element_type=F32))

    def e_block(blk, p):
        return jnp.concatenate(
            [e_all[blk * c:(blk + 1) * c, t * D_GLA_QK + p * LANES:t * D_GLA_QK + (p + 1) * LANES]
             for t in range(n_chunks)], axis=0)

    lane = lax.broadcasted_iota(jnp.int32, (tm, LANES), 1)
    lane_v = lax.broadcasted_iota(jnp.int32, (GLA_DV, LANES), 1)
    ids = ids_ref[...]
    ids2 = jnp.concatenate([ids, ids], axis=0)
    vb = v_ref[...].astype(BF16)

    for p in range(n_pairs):
        ls = slice(p * LANES, (p + 1) * LANES)
        qp, kp = q_ref[:, ls], k_ref[:, ls]
        qm = (jnp.where(lane < GLA_DK, qp, 0.0), jnp.where(lane >= GLA_DK, qp, 0.0))

        att = jnp.where(ids2 == GLA_DIAG,
                        lax.dot_general(jnp.concatenate(qm, axis=0).astype(BF16), kp.astype(BF16), nt,
                                        preferred_element_type=F32), 0.0)
        for li, b in enumerate(GLA_SUB_LEVELS):
            eb = e_block(2 + li, p)
            lhs = jnp.concatenate([qm[0] * eb, qm[1] * eb], axis=0).astype(BF16)
            part = lax.dot_general(lhs, (kp * eb).astype(BF16), nt, preferred_element_type=F32)
            att = jnp.where(ids2 == b, part, att)
        att = att.astype(BF16)

        eq, ek = e_block(0, p), e_block(1, p)
        k_inter = (kp * ek).astype(BF16)
        decay, upd = [], []
        for t in range(n_chunks):
            rows = slice(t * c, (t + 1) * c)
            e_q = e_all[0:c, t * D_GLA_QK + p * LANES:t * D_GLA_QK + (p + 1) * LANES]
            decay.append(jnp.where(fwd, e_q[c - 1:c], e_q[0:1]))
            u = lax.dot_general(vb[rows, 2 * p * GLA_DV:(2 * p + 2) * GLA_DV], k_inter[rows], tn,
                                preferred_element_type=F32)
            upd.append(jnp.where(lane_v < GLA_DK, u[:GLA_DV], u[GLA_DV:]))
        st = st_ref[p]
        before = []
        for t in range(n_chunks):
            r = n_chunks - 1 - t
            before.append(st)
            st = jnp.where(fwd, decay[t], decay[r]) * st + jnp.where(fwd, upd[t], upd[r])
        st_ref[p] = st

        for t in range(n_chunks):
            rows = slice(t * c, (t + 1) * c)
            s_t = jnp.where(fwd, before[t], before[n_chunks - 1 - t]).astype(BF16)
            lhs = jnp.concatenate([qm[0][rows] * eq[rows], qm[1][rows] * eq[rows]], axis=0).astype(BF16)
            o_inter = lax.dot_general(lhs, s_t, nt, preferred_element_type=F32)
            for hh in range(2):
                h = 2 * p + hh
                o_intra = jnp.dot(att[hh * tm + t * c:hh * tm + (t + 1) * c],
                                  vb[:, h * GLA_DV:(h + 1) * GLA_DV], preferred_element_type=F32)
                o_ref[0, rows, h * GLA_DV:(h + 1) * GLA_DV] = o_inter[hh * c:(hh + 1) * c] + o_intra


def _gla(gq, gk, gv, la, *, n_lat):
    n_all = gq.shape[0]
    tm = ROW_TILE
    nl = n_lat // tm
    nb = n_all // tm
    nc = nb - nl

    def blk(d, s):
        ctx = jnp.where(d == 0, nl + s, nb - 1 - s)
        lat = jnp.where(d == 0, s - nc, nb - 1 - s)
        return jnp.where(s < nc, ctx, lat)

    row = lambda d, s: (blk(d, s), 0)
    return pl.pallas_call(
        _gla_kernel,
        grid=(2, nb),
        in_specs=[pl.BlockSpec((tm, D_GLA_QK), row),
                  pl.BlockSpec((tm, D_GLA_QK), row),
                  pl.BlockSpec((tm, D_GLA_OUT), row),
                  pl.BlockSpec((1, tm, D_GLA_QK), lambda d, s: (d, blk(d, s), 0))],
        out_specs=pl.BlockSpec((1, tm, D_GLA_OUT), lambda d, s: (d, blk(d, s), 0)),
        out_shape=jax.ShapeDtypeStruct((2, n_all, D_GLA_OUT), F32),
        scratch_shapes=[pltpu.VMEM((N_GLA_HEADS // 2, GLA_DV, 2 * GLA_DK), F32),
                        pltpu.VMEM(((2 + len(GLA_SUB_LEVELS)) * GLA_CHUNK, 3 * GLA_CHUNK), BF16),
                        pltpu.VMEM((tm, tm), jnp.int32)],
        compiler_params=_params("arbitrary", "arbitrary"),
        name="gla_scan",
    )(gq, gk, gv, la)


def _mixout_ffn_kernel(h_ref, mlat_ref, og_ref, gg_ref, gn_ref, wmix_ref, mod_ref, gpost1_ref,
                       gpre2_ref, gpost2_ref, win_ref, wout_ref, o_ref):
    o = og_ref[0] + og_ref[1]
    gate = _silu(gg_ref[...])
    parts = []
    for h in range(N_GLA_HEADS):
        sl = slice(h * GLA_DV, (h + 1) * GLA_DV)
        parts.append((_rms(o[:, sl], gn_ref[...]) * gate[:, sl]).astype(BF16))
    tn = (((0,), (0,)), ((), ()))
    y = lax.dot_general(mlat_ref[...], wmix_ref[:D_MLA_OUT, :], tn, preferred_element_type=F32)
    y = y + jnp.dot(jnp.concatenate(parts, axis=-1), wmix_ref[D_MLA_OUT:, :], preferred_element_type=F32)
    m = mod_ref[0]
    x = h_ref[...] + m[5:6] * _rms(y, gpost1_ref[...])
    o_ref[...] = _ffn_tile(x, m, 6, gpre2_ref[...], gpost2_ref[...], win_ref, wout_ref)


def _mixout_ffn(h_lat, mla_t, og, gg, g_norm, w_mix, mod, g_post1, g_pre2, g_post2, w_in, w_out, *, n_lat):
    tm = FUSED_TILE
    row = lambda i: (i, 0)
    const = lambda i: (0, 0)
    vec = pl.BlockSpec((1, D_MODEL), const)
    return pl.pallas_call(
        _mixout_ffn_kernel,
        grid=(n_lat // tm,),
        in_specs=[pl.BlockSpec((tm, D_MODEL), row),
                  pl.BlockSpec((D_MLA_OUT, tm), lambda i: (0, i)),
                  pl.BlockSpec((2, tm, D_GLA_OUT), lambda i: (0, i, 0)),
                  pl.BlockSpec((tm, D_GLA_OUT), row),
                  pl.BlockSpec((1, GLA_DV), const),
                  pl.BlockSpec(w_mix.shape, const),
                  pl.BlockSpec((1, N_MOD, D_MODEL), lambda i: (0, 0, 0)),
                  vec, vec, vec,
                  pl.BlockSpec((D_MODEL, 2 * D_FF), const),
                  pl.BlockSpec((D_FF, D_MODEL), const)],
        out_specs=pl.BlockSpec((tm, D_MODEL), row),
        out_shape=jax.ShapeDtypeStruct((n_lat, D_MODEL), F32),
        compiler_params=_params("arbitrary"),
        name="mixer_out_ffn",
    )(h_lat, mla_t, og, gg, g_norm, w_mix, mod, g_post1, g_pre2, g_post2, w_in, w_out)


def _rope_half_swap(w):
    half = MLA_ROPE_DIM // 4
    g = w.reshape(w.shape[:-1] + (2, 2, half))
    return g[..., ::-1, :].reshape(w.shape)


def _layout_w_in(w):
    d = w.shape[0]
    o_kv = MLA_Q_RANK
    o_kr = o_kv + MLA_KV_RANK
    o_gq = o_kr + MLA_ROPE_DIM
    o_gk = o_gq + D_GLA_QK
    o_gv = o_gk + D_GLA_QK
    o_gg = o_gv + D_GLA_OUT
    o_ga = o_gg + D_GLA_OUT
    k_rope = w[:, o_kr:o_gq]
    z = lambda n: jnp.zeros((d, n), w.dtype)
    pad = LANES - ROPE_LANE0 - MLA_ROPE_DIM
    return jnp.concatenate([
        w[:, :o_kr], w[:, o_gq:o_ga],
        w[:, o_ga:o_ga + 2 * GLA_GATE_RANK], z(ROPE_LANE0 - 2 * GLA_GATE_RANK), k_rope, z(pad),
        z(ROPE_LANE0), _rope_half_swap(k_rope), z(pad)], axis=1)


def _layout_w_q(w):
    r = w.shape[0]
    w = w.reshape(r, N_MLA_HEADS, MLA_QK_DIM)
    nope, rope = w[..., :MLA_NOPE_DIM], w[..., MLA_NOPE_DIM:]
    zp = jnp.zeros((r, N_MLA_HEADS, HEAD_PAD - MLA_QK_DIM), w.dtype)
    zn = jnp.zeros_like(nope)
    plain = jnp.concatenate([nope, rope, zp], axis=-1).reshape(r, -1)
    swapped = jnp.concatenate([zn, _rope_half_swap(rope), zp], axis=-1).reshape(r, -1)
    return jnp.concatenate([plain, swapped], axis=1)


def _layout_w_a(w_f, w_b):
    z = jnp.zeros((LANES, 2 * D_GLA_QK), w_f.dtype)
    z = z.at[:GLA_GATE_RANK, :D_GLA_QK].set(w_f)
    return z.at[GLA_GATE_RANK:2 * GLA_GATE_RANK, D_GLA_QK:].set(w_b)


def _attn_key_block(n_all):
    for bk in (640, 512, 256):
        if n_all % bk == 0:
            return bk
    raise ValueError(f"unsupported key count {n_all}")


def kernel(x, c, ctx, c_ctx, w_ada, b_ada, norm_pre, norm_post, ffn1_w_in, ffn1_w_out, ffn2_w_in, ffn2_w_out, w_in, mla_q_norm, mla_w_qb, mla_kv_norm, mla_w_kvb, gla_w_a_fwd, gla_b_a_fwd, gla_w_a_bwd, gla_b_a_bwd, gla_norm, w_out):
    assert x.shape[0] == 1 and ctx.shape[0] == 1 and w_ada.shape[0] == 1
    n_lat, n_ctx = x.shape[1], ctx.shape[1]
    assert n_lat % ROW_TILE == 0 and n_ctx % ROW_TILE == 0 and n_lat % GRID_W == 0
    n_all = n_lat + n_ctx
    row2 = lambda a: a.reshape(1, -1)

    ct = jnp.stack([c[0], c_ctx], axis=1)
    mod = _ada(ct, w_ada[0], row2(b_ada[0])).reshape(2, N_MOD, D_MODEL)

    h_lat, h_ctx = _ffn1(x[0], ctx[0], mod, row2(norm_pre[0, 0]), row2(norm_post[0, 0]),
                  ffn1_w_in[0].astype(BF16), ffn1_w_out[0].astype(BF16))

    q, k, v, gq, gk, gv, gg, la = _mixin(
        h_lat, h_ctx, mod, row2(norm_pre[0, 1]), _layout_w_in(w_in[0]).astype(BF16),
        row2(mla_q_norm[0]), _layout_w_q(mla_w_qb[0]).astype(BF16),
        row2(mla_kv_norm[0]), mla_w_kvb[0].astype(BF16),
        _layout_w_a(gla_w_a_fwd[0], gla_w_a_bwd[0]).astype(BF16),
        jnp.concatenate([gla_b_a_fwd[0], gla_b_a_bwd[0]]).reshape(1, -1), n_lat=n_lat)

    mla = _attn(q, k, v, n_lat=n_lat, bq=ATTN_Q_TILE, bk=_attn_key_block(n_all))
    og = _gla(gq, gk, gv, la, n_lat=n_lat)

    out = _mixout_ffn(h_lat, mla, og, gg, row2(gla_norm[0]), w_out[0].astype(BF16), mod,
                      row2(norm_post[0, 1]), row2(norm_pre[0, 2]), row2(norm_post[0, 2]),
                      ffn2_w_in[0].astype(BF16), ffn2_w_out[0].astype(BF16), n_lat=n_lat)
    return out[None]
```
